```python
import jax, jax.numpy as jnp
from jax import lax
import numpy as np

D_MODEL = 2048
BATCH = 8
SEQ = 2048
DEPTH = 4
DEC_BATCH = 8
DEC_SEQ = 64
PAST_LEN = 1024

CHUNK = 64
N_MIXERS = 2
N_RGLRU = (DEPTH + 1) // 2
N_GLA = DEPTH // 2
EPS = 1e-6
D_RNN = D_MODEL
RG_BLOCKS = 8
RG_BW = D_RNN // RG_BLOCKS
CONV_W = 4
RG_C = 8.0
GLA_HEADS = 4
HEAD_K = (D_MODEL // 2) // GLA_HEADS
HEAD_V = D_MODEL // GLA_HEADS
GATE_RANK = 16
GATE_NORM = 16.0
GLA_DQ = GLA_HEADS * HEAD_K
GLA_DV = GLA_HEADS * HEAD_V
GLA_IN = 2 * GLA_DQ + 2 * GLA_DV + GATE_RANK
D_FF = ((8 * D_MODEL // 3 + 255) // 256) * 256

kernel_name = "hybrid_rglru_gla_streaming_step"


def rms_norm(x, w):
    xf = x.astype(jnp.float32)
    y = xf * lax.rsqrt(jnp.mean(xf * xf, axis=-1, keepdims=True) + EPS)
    return (y * w.astype(jnp.float32)).astype(x.dtype)


def rglru_mixer(x, h0, conv0, w_in, conv_w, conv_b, w_a, b_a, w_x, b_x, lam, w_out):
    B, T, _ = x.shape
    gate, u = jnp.split(x @ w_in, 2, axis=-1)
    upad = jnp.concatenate([conv0.astype(u.dtype), u], axis=1)
    new_conv = upad[:, T:]
    uc = conv_b + sum(upad[:, k:k + T] * conv_w[k] for k in range(CONV_W))
    ub = uc.reshape(B, T, RG_BLOCKS, RG_BW)
    r = jax.nn.sigmoid(jnp.einsum('btnc,ncd->btnd', ub, w_a).reshape(B, T, D_RNN) + b_a)
    i = jax.nn.sigmoid(jnp.einsum('btnc,ncd->btnd', ub, w_x).reshape(B, T, D_RNN) + b_x)
    log_a = RG_C * r.astype(jnp.float32) * jax.nn.log_sigmoid(lam.astype(jnp.float32))
    a = jnp.exp(log_a)
    bterm = jnp.sqrt(-jnp.expm1(2.0 * log_a)) * (i * uc).astype(jnp.float32)
    bterm = bterm.at[:, 0].add(a[:, 0] * h0.astype(jnp.float32))

    def combine(left, right):
        a1, b1 = left
        a2, b2 = right
        return a1 * a2, a2 * b1 + b2

    _, h = lax.associative_scan(combine, (a, bterm), axis=1)
    y = (h.astype(x.dtype) * jax.nn.gelu(gate)) @ w_out
    return y, h[:, -1].astype(x.dtype), new_conv


def gla_mixer(x, S0, w_in, w_gk2, b_gk, g_norm_w, w_out):
    B, T, _ = x.shape
    f32 = jnp.float32
    q, k, v, g, glr = jnp.split(x @ w_in, [GLA_DQ, 2 * GLA_DQ, 2 * GLA_DQ + GLA_DV,
                                           2 * GLA_DQ + 2 * GLA_DV], axis=-1)
    gk = jax.nn.log_sigmoid((glr @ w_gk2 + b_gk).astype(f32)) / GATE_NORM
    C = CHUNK if T % CHUNK == 0 else T
    N = T // C
    qh = q.astype(f32).reshape(B, N, C, GLA_HEADS, HEAD_K) * (HEAD_K ** -0.5)
    kh = k.astype(f32).reshape(B, N, C, GLA_HEADS, HEAD_K)
    vh = v.astype(f32).reshape(B, N, C, GLA_HEADS, HEAD_V)
    bcum = jnp.cumsum(gk.reshape(B, N, C, GLA_HEADS, HEAD_K), axis=2)
    blast = bcum[:, :, -1]
    qe = qh * jnp.exp(bcum)
    ke = kh * jnp.exp(-bcum)
    kd = kh * jnp.exp(blast[:, :, None] - bcum)
    mask = jnp.tril(jnp.ones((C, C), dtype=bool))
    att = jnp.where(mask, jnp.einsum('bnchd,bnshd->bnhcs', qe, ke), 0.0)
    o_intra = jnp.einsum('bnhcs,bnshv->bnchv', att, vh)

    def step(S, inp):
        qe_n, kd_n, v_n, bl_n = inp
        o_n = jnp.einsum('bchd,bhdv->bchv', qe_n, S)
        S = jnp.exp(bl_n)[..., None] * S + jnp.einsum('bchd,bchv->bhdv', kd_n, v_n)
        return S, o_n

    xs = (jnp.moveaxis(qe, 1, 0), jnp.moveaxis(kd, 1, 0), jnp.moveaxis(vh, 1, 0), jnp.moveaxis(blast, 1, 0))
    S_fin, o_inter = lax.scan(step, S0.astype(f32), xs)
    o = (o_intra + jnp.moveaxis(o_inter, 0, 1)).reshape(B, T, GLA_HEADS, HEAD_V)
    o = o * lax.rsqrt(jnp.mean(o * o, axis=-1, keepdims=True) + EPS) * g_norm_w.astype(f32)
    o = o.reshape(B, T, GLA_DV) * jax.nn.silu(g.astype(f32))
    y = o.astype(x.dtype) @ w_out
    return y, S_fin.astype(x.dtype)


def trunk(x, h_all, conv_all, S_all, norm_mix, norm_ffn, norm_final,
          rg_w_in, rg_conv_w, rg_conv_b, rg_w_a, rg_b_a, rg_w_x, rg_b_x, rg_lambda, rg_w_out,
          gla_w_in, gla_w_gk2, gla_b_gk, gla_norm_w, gla_w_out, ffn_w_up, ffn_w_down):
    hs, cs, Ss = [], [], []
    for layer in range(DEPTH):
        j = layer // N_MIXERS
        hn = rms_norm(x, norm_mix[layer])
        if layer % N_MIXERS == 0:
            y, h_new, c_new = rglru_mixer(hn, h_all[j], conv_all[j], rg_w_in[j], rg_conv_w[j], rg_conv_b[j],
                                          rg_w_a[j], rg_b_a[j], rg_w_x[j], rg_b_x[j], rg_lambda[j], rg_w_out[j])
            hs.append(h_new)
            cs.append(c_new)
        else:
            y, S_new = gla_mixer(hn, S_all[j], gla_w_in[j], gla_w_gk2[j], gla_b_gk[j], gla_norm_w[j], gla_w_out[j])
            Ss.append(S_new)
        x = x + y
        hn = rms_norm(x, norm_ffn[layer])
        gt, up = jnp.split(hn @ ffn_w_up[layer], 2, axis=-1)
        x = x + (jax.nn.silu(gt) * up) @ ffn_w_down[layer]
    return rms_norm(x, norm_final), jnp.stack(hs), jnp.stack(cs), jnp.stack(Ss)


def setup_inputs(seed: int = 0) -> dict:
    key = jax.random.key(seed)
    ks = jax.random.split(key, 24)
    nrm = jax.random.normal
    f = jnp.float32
    u = jax.random.uniform(ks[10], (N_RGLRU, D_RNN), f, 0.81, 0.998)
    s = u ** (1.0 / RG_C)
    return {
        "x_prompt": nrm(ks[0], (BATCH, SEQ, D_MODEL), f),
        "x_sample": nrm(ks[1], (DEC_BATCH, DEC_SEQ, D_MODEL), f),
        "state_rglru_h": 0.5 * nrm(ks[2], (N_RGLRU, DEC_BATCH, D_RNN), f),
        "state_rglru_conv": nrm(ks[3], (N_RGLRU, DEC_BATCH, CONV_W - 1, D_RNN), f),
        "state_gla": nrm(ks[4], (N_GLA, DEC_BATCH, GLA_HEADS, HEAD_K, HEAD_V), f),
        "norm_mix": 1.0 + 0.05 * nrm(ks[5], (DEPTH, D_MODEL), f),
        "norm_ffn": 1.0 + 0.05 * nrm(ks[6], (DEPTH, D_MODEL), f),
        "norm_final": 1.0 + 0.05 * nrm(ks[7], (D_MODEL,), f),
        "rg_w_in": nrm(ks[8], (N_RGLRU, D_MODEL, 2 * D_RNN), f) * D_MODEL ** -0.5,
        "rg_conv_w": nrm(ks[9], (N_RGLRU, CONV_W, D_RNN), f) * CONV_W ** -0.5,
        "rg_conv_b": 0.01 * nrm(ks[11], (N_RGLRU, D_RNN), f),
        "rg_w_a": nrm(ks[12], (N_RGLRU, RG_BLOCKS, RG_BW, RG_BW), f) * RG_BW ** -0.5,
        "rg_b_a": 0.1 * nrm(ks[13], (N_RGLRU, D_RNN), f),
        "rg_w_x": nrm(ks[14], (N_RGLRU, RG_BLOCKS, RG_BW, RG_BW), f) * RG_BW ** -0.5,
        "rg_b_x": 0.1 * nrm(ks[15], (N_RGLRU, D_RNN), f),
        "rg_lambda": jnp.log(s) - jnp.log1p(-s),
        "rg_w_out": nrm(ks[16], (N_RGLRU, D_RNN, D_MODEL), f) * D_RNN ** -0.5,
        "gla_w_in": nrm(ks[17], (N_GLA, D_MODEL, GLA_IN), f) * D_MODEL ** -0.5,
        "gla_w_gk2": nrm(ks[18], (N_GLA, GATE_RANK, GLA_DQ), f) * GATE_RANK ** -0.5,
        "gla_b_gk": 0.1 * nrm(ks[19], (N_GLA, GLA_DQ), f),
        "gla_norm_w": 1.0 + 0.05 * nrm(ks[20], (N_GLA, HEAD_V), f),
        "gla_w_out": nrm(ks[21], (N_GLA, GLA_DV, D_MODEL), f) * GLA_DV ** -0.5,
        "ffn_w_up": nrm(ks[22], (DEPTH, D_MODEL, 2 * D_FF), f) * D_MODEL ** -0.5,
        "ffn_w_down": nrm(ks[23], (DEPTH, D_FF, D_MODEL), f) * D_FF ** -0.5,
    }


def reference(x_prompt, x_sample, state_rglru_h, state_rglru_conv, state_gla,
              norm_mix, norm_ffn, norm_final,
              rg_w_in, rg_conv_w, rg_conv_b, rg_w_a, rg_b_a, rg_w_x, rg_b_x, rg_lambda, rg_w_out,
              gla_w_in, gla_w_gk2, gla_b_gk, gla_norm_w, gla_w_out, ffn_w_up, ffn_w_down):
    Bp = x_prompt.shape[0]
    dt = x_prompt.dtype
    h0 = jnp.zeros((N_RGLRU, Bp, D_RNN), dt)
    c0 = jnp.zeros((N_RGLRU, Bp, CONV_W - 1, D_RNN), dt)
    S0 = jnp.zeros((N_GLA, Bp, GLA_HEADS, HEAD_K, HEAD_V), dt)
    y_prompt, h_p, c_p, S_p = trunk(x_prompt, h0, c0, S0, norm_mix, norm_ffn, norm_final,
                                    rg_w_in, rg_conv_w, rg_conv_b, rg_w_a, rg_b_a, rg_w_x, rg_b_x, rg_lambda, rg_w_out,
                                    gla_w_in, gla_w_gk2, gla_b_gk, gla_norm_w, gla_w_out, ffn_w_up, ffn_w_down)
    y_sample, h_s, c_s, S_s = trunk(x_sample, state_rglru_h, state_rglru_conv, state_gla, norm_mix, norm_ffn, norm_final,
                                    rg_w_in, rg_conv_w, rg_conv_b, rg_w_a, rg_b_a, rg_w_x, rg_b_x, rg_lambda, rg_w_out,
                                    gla_w_in, gla_w_gk2, gla_b_gk, gla_norm_w, gla_w_out, ffn_w_up, ffn_w_down)
    return (y_prompt, y_sample, h_p, c_p, S_p, h_s, c_s, S_s)
```

```python
import functools

import jax
import jax.numpy as jnp
from jax import lax
from jax.experimental import pallas as pl
from jax.experimental.pallas import tpu as pltpu

F32 = jnp.float32
BF16 = jnp.bfloat16

D_MODEL = 2048
DEPTH = 4
CHUNK = 64
EPS = 1e-6
D_RNN = D_MODEL
RG_BLOCKS = 8
RG_BW = D_RNN // RG_BLOCKS
CONV_W = 4
RG_C = 8.0
GLA_HEADS = 4
HEAD_K = 256
HEAD_V = 512
GATE_RANK = 16
GATE_NORM = 16.0
GLA_DQ = GLA_HEADS * HEAD_K
GLA_DV = GLA_HEADS * HEAD_V
GLA_MAIN = 2 * GLA_DQ + 2 * GLA_DV
D_FF = 5632

LANES = 128
SUBLANES = 8
NORM_ROWS = 16
MIB = 1024 * 1024


def _params(dims, vmem_mib):
    return pltpu.CompilerParams(dimension_semantics=dims, vmem_limit_bytes=vmem_mib * MIB)


def _rms(x, w):
    ms = jnp.mean(x * x, axis=-1, keepdims=True)
    return x * lax.rsqrt(ms + EPS) * w


def _log_sigmoid(x):
    return jnp.minimum(x, 0.0) - jnp.log1p(jnp.exp(-jnp.abs(x)))


def _expm1(x):
    small = x * (1.0 + x * (0.5 + x * (1.0 / 6 + x * (1.0 / 24 + x * (1.0 / 120 + x * (1.0 / 720))))))
    return jnp.where(jnp.abs(x) < 0.1, small, jnp.exp(x) - 1.0)


def _norm_into(x_ref, nw_ref, hn_ref, rows):
    nw = nw_ref[...]

    def body(r, carry):
        sl = pl.ds(pl.multiple_of(r * NORM_ROWS, NORM_ROWS), NORM_ROWS)
        hn_ref[sl, :] = _rms(x_ref[sl, :], nw).astype(BF16)
        return carry

    lax.fori_loop(0, rows // NORM_ROWS, body, 0)


def _norm_matmul_kernel(x_ref, nw_ref, w_ref, o_ref, hn_ref, *, tm):
    @pl.when(pl.program_id(1) == 0)
    def _():
        _norm_into(x_ref, nw_ref, hn_ref, tm)

    o_ref[...] = jnp.dot(hn_ref[...], w_ref[...], preferred_element_type=F32)


def _norm_matmul2_kernel(x_ref, nw_ref, w_ref, w2_ref, o_ref, o2_ref, hn_ref, *, tm):
    @pl.when(pl.program_id(1) == 0)
    def _():
        _norm_into(x_ref, nw_ref, hn_ref, tm)
        o2_ref[...] = jnp.dot(hn_ref[...], w2_ref[...], preferred_element_type=F32)

    o_ref[...] = jnp.dot(hn_ref[...], w_ref[...], preferred_element_type=F32)


def _norm_matmul(x, nw, w, w2=None):
    m, d = x.shape
    n = w.shape[1]
    tm = min(m, 1024)
    tn = 1024
    grid = (m // tm, n // tn)
    x_spec = pl.BlockSpec((tm, d), lambda i, j: (i, 0))
    nw_spec = pl.BlockSpec((1, d), lambda i, j: (0, 0))
    w_spec = pl.BlockSpec((d, tn), lambda i, j: (0, j))
    o_spec = pl.BlockSpec((tm, tn), lambda i, j: (i, j))
    scratch = [pltpu.VMEM((tm, d), BF16)]
    cp = _params(("parallel", "arbitrary"), 48)
    if w2 is None:
        return pl.pallas_call(
            functools.partial(_norm_matmul_kernel, tm=tm),
            grid=grid, in_specs=[x_spec, nw_spec, w_spec], out_specs=o_spec,
            out_shape=jax.ShapeDtypeStruct((m, n), F32), scratch_shapes=scratch,
            compiler_params=cp, name="norm_matmul")(x, nw, w)
    n2 = w2.shape[1]
    return pl.pallas_call(
        functools.partial(_norm_matmul2_kernel, tm=tm),
        grid=grid,
        in_specs=[x_spec, nw_spec, w_spec, pl.BlockSpec((d, n2), lambda i, j: (0, 0))],
        out_specs=[o_spec, pl.BlockSpec((tm, n2), lambda i, j: (i, 0))],
        out_shape=[jax.ShapeDtypeStruct((m, n), F32), jax.ShapeDtypeStruct((m, n2), F32)],
        scratch_shapes=scratch, compiler_params=cp, name="norm_matmul2")(x, nw, w, w2)


def _matmul_res_kernel(a_ref, w_ref, x_ref, o_ref):
    o_ref[...] = x_ref[...] + jnp.dot(a_ref[...], w_ref[...], preferred_element_type=F32)


def _matmul_res(a, w, x):
    m, k = a.shape
    n = w.shape[1]
    tm = min(m, 1024)
    tn = 1024
    return pl.pallas_call(
        _matmul_res_kernel,
        grid=(m // tm, n // tn),
        in_specs=[pl.BlockSpec((tm, k), lambda i, j: (i, 0)),
                  pl.BlockSpec((k, tn), lambda i, j: (0, j)),
                  pl.BlockSpec((tm, tn), lambda i, j: (i, j))],
        out_specs=pl.BlockSpec((tm, tn), lambda i, j: (i, j)),
        out_shape=jax.ShapeDtypeStruct((m, n), F32),
        compiler_params=_params(("parallel", "arbitrary"), 48), name="matmul_res")(a, w, x)


def _ffn_kernel(x_ref, nw_ref, wg_ref, wu_ref, wd_ref, fw_ref, o_ref, hn_ref, *, tm, nf, final):
    f = pl.program_id(1)

    @pl.when(f == 0)
    def _():
        _norm_into(x_ref, nw_ref, hn_ref, tm)
        o_ref[...] = x_ref[...]

    hn = hn_ref[...]
    gt = jnp.dot(hn, wg_ref[...], preferred_element_type=F32)
    up = jnp.dot(hn, wu_ref[...], preferred_element_type=F32)
    act = ((gt * jax.nn.sigmoid(gt)) * up).astype(BF16)
    o_ref[...] += jnp.dot(act, wd_ref[...], preferred_element_type=F32)

    if final:
        @pl.when(f == nf - 1)
        def _():
            fw = fw_ref[...]

            def body(r, carry):
                sl = pl.ds(pl.multiple_of(r * NORM_ROWS, NORM_ROWS), NORM_ROWS)
                o_ref[sl, :] = _rms(o_ref[sl, :], fw)
                return carry

            lax.fori_loop(0, tm // NORM_ROWS, body, 0)


def _ffn(x, nw, w_up, w_down, fw, final):
    m, d = x.shape
    tm = min(m, 512)
    tf = 512
    nf = D_FF // tf
    return pl.pallas_call(
        functools.partial(_ffn_kernel, tm=tm, nf=nf, final=final),
        grid=(m // tm, nf),
        in_specs=[pl.BlockSpec((tm, d), lambda i, f: (i, 0)),
                  pl.BlockSpec((1, d), lambda i, f: (0, 0)),
                  pl.BlockSpec((d, tf), lambda i, f: (0, f)),
                  pl.BlockSpec((d, tf), lambda i, f: (0, f + nf)),
                  pl.BlockSpec((tf, d), lambda i, f: (f, 0)),
                  pl.BlockSpec((1, d), lambda i, f: (0, 0))],
        out_specs=pl.BlockSpec((tm, d), lambda i, f: (i, 0)),
        out_shape=jax.ShapeDtypeStruct((m, d), F32),
        scratch_shapes=[pltpu.VMEM((tm, d), BF16)],
        compiler_params=_params(("parallel", "arbitrary"), 48), name="ffn")(x, nw, w_up, w_up, w_down, fw)


def _rglru_kernel(gate_ref, u_ref, h0_ref, c0_ref, cw_ref, cb_ref, wa_ref, ba_ref, wx_ref, bx_ref, lam_ref,
                  hg_ref, hout_ref, cout_ref, uext_ref, *, tc, nt):
    t = pl.program_id(2)
    pad = SUBLANES

    @pl.when(t == 0)
    def _():
        uext_ref[pad - (CONV_W - 1):pad, :] = c0_ref[...]
        hout_ref[...] = h0_ref[...]

    uext_ref[pad:pad + tc, :] = u_ref[...]
    cw = cw_ref[...]
    acc = uext_ref[pl.ds(pad - 3, tc), :] * cw[0:1, :]
    acc = acc + uext_ref[pl.ds(pad - 2, tc), :] * cw[1:2, :]
    acc = acc + uext_ref[pl.ds(pad - 1, tc), :] * cw[2:3, :]
    acc = acc + uext_ref[pl.ds(pad, tc), :] * cw[3:4, :]
    uc = cb_ref[...] + acc
    ucb = uc.astype(BF16)
    r = jax.nn.sigmoid(jnp.dot(ucb, wa_ref[...], preferred_element_type=F32) + ba_ref[...])
    i = jax.nn.sigmoid(jnp.dot(ucb, wx_ref[...], preferred_element_type=F32) + bx_ref[...])
    log_a = (RG_C * r) * _log_sigmoid(lam_ref[...])
    a = jnp.exp(log_a)
    b = jnp.sqrt(-_expm1(2.0 * log_a)) * (i * uc)

    groups = tc // SUBLANES
    a3 = a.reshape(groups, SUBLANES, RG_BW)
    b3 = b.reshape(groups, SUBLANES, RG_BW)
    row = lax.broadcasted_iota(jnp.int32, (groups, SUBLANES, RG_BW), 1)
    for s in (1, 2, 4):
        keep = row >= s
        a_prev = pltpu.roll(a3, s, 1)
        b_prev = pltpu.roll(b3, s, 1)
        b3 = jnp.where(keep, b3 + a3 * b_prev, b3)
        a3 = jnp.where(keep, a3 * a_prev, a3)
    h_prev = jnp.broadcast_to(hout_ref[...], (SUBLANES, RG_BW))
    hs = []
    for g in range(groups):
        h_g = a3[g] * h_prev + b3[g]
        hs.append(h_g)
        h_prev = jnp.broadcast_to(h_g[SUBLANES - 1:SUBLANES, :], (SUBLANES, RG_BW))
    h = jnp.concatenate(hs, axis=0)
    hout_ref[...] = hs[-1][SUBLANES - 1:SUBLANES, :]
    hg_ref[...] = (h * jax.nn.gelu(gate_ref[...])).astype(BF16)

    uext_ref[0:pad, :] = uext_ref[tc:tc + pad, :]

    @pl.when(t == nt - 1)
    def _():
        cout_ref[...] = uext_ref[pad - (CONV_W - 1):pad, :]


def _rglru_core(gu, nseq, tlen, h0, c0, cw, cb, wa, ba, wx, bx, lam):
    m = nseq * tlen
    tc = min(tlen, 512)
    nt = tlen // tc
    bw = RG_BW
    row = lambda b, n, t: b * nt + t
    vec = pl.BlockSpec((1, bw), lambda b, n, t: (0, n))
    gate_spec = pl.BlockSpec((tc, bw), lambda b, n, t: (row(b, n, t), n))
    u_spec = pl.BlockSpec((tc, bw), lambda b, n, t: (row(b, n, t), RG_BLOCKS + n))
    h_spec = pl.BlockSpec((None, 1, bw), lambda b, n, t: (b, 0, n))
    c_spec = pl.BlockSpec((None, CONV_W - 1, bw), lambda b, n, t: (b, 0, n))
    w_spec = pl.BlockSpec((None, bw, bw), lambda b, n, t: (n, 0, 0))
    return pl.pallas_call(
        functools.partial(_rglru_kernel, tc=tc, nt=nt),
        grid=(nseq, RG_BLOCKS, nt),
        in_specs=[gate_spec, u_spec, h_spec, c_spec,
                  pl.BlockSpec((CONV_W, bw), lambda b, n, t: (0, n)), vec,
                  w_spec, vec, w_spec, vec, vec],
        out_specs=[pl.BlockSpec((tc, bw), lambda b, n, t: (row(b, n, t), n)), h_spec, c_spec],
        out_shape=[jax.ShapeDtypeStruct((m, D_RNN), BF16),
                   jax.ShapeDtypeStruct((nseq, 1, D_RNN), F32),
                   jax.ShapeDtypeStruct((nseq, CONV_W - 1, D_RNN), F32)],
        scratch_shapes=[pltpu.VMEM((tc + SUBLANES, bw), F32)],
        compiler_params=_params(("parallel", "parallel", "arbitrary"), 32),
        name="rglru_core")(gu, gu, h0, c0, cw, cb, wa, ba, wx, bx, lam)


def _gla_kernel(q_ref, k_ref, v_ref, g_ref, glr_ref, s0_ref, w2_ref, bgk_ref, gnw_ref,
                o_ref, sout_ref, st_ref, *, nt):
    t = pl.program_id(1)

    @pl.when(t == 0)
    def _():
        for h in range(GLA_HEADS):
            st_ref[h] = s0_ref[h].T

    c = CHUNK
    gk = jnp.dot(glr_ref[...].astype(BF16), w2_ref[...], preferred_element_type=F32) + bgk_ref[...]
    gk = _log_sigmoid(gk) * (1.0 / GATE_NORM)
    rows = lax.broadcasted_iota(jnp.int32, (c, c), 0)
    cols = lax.broadcasted_iota(jnp.int32, (c, c), 1)
    tri = rows >= cols
    bcum = jnp.dot(tri.astype(F32), gk, precision=lax.Precision.HIGHEST, preferred_element_type=F32)
    blast = bcum[c - 1:c, :]
    k = k_ref[...]
    qe = ((q_ref[...] * (HEAD_K ** -0.5)) * jnp.exp(bcum)).astype(BF16)
    ke = (k * jnp.exp(-bcum)).astype(BF16)
    kd = (k * jnp.exp(blast - bcum)).astype(BF16)
    decay = jnp.exp(blast)
    gnw = gnw_ref[...]
    nt_dims = (((1,), (1,)), ((), ()))
    tn_dims = (((0,), (0,)), ((), ()))
    for h in range(GLA_HEADS):
        ks = slice(h * HEAD_K, (h + 1) * HEAD_K)
        vs = slice(h * HEAD_V, (h + 1) * HEAD_V)
        v_h = v_ref[:, vs].astype(BF16)
        att = lax.dot_general(qe[:, ks], ke[:, ks], nt_dims, preferred_element_type=F32)
        att = jnp.where(tri, att, 0.0).astype(BF16)
        st = st_ref[h]
        o = jnp.dot(att, v_h, preferred_element_type=F32)
        o = o + lax.dot_general(qe[:, ks], st.astype(BF16), nt_dims, preferred_element_type=F32)
        st_ref[h] = st * decay[:, ks] + lax.dot_general(v_h, kd[:, ks], tn_dims, preferred_element_type=F32)
        on = o * lax.rsqrt(jnp.mean(o * o, axis=-1, keepdims=True) + EPS) * gnw
        g_h = g_ref[:, vs]
        o_ref[:, vs] = (on * (g_h * jax.nn.sigmoid(g_h))).astype(BF16)

    @pl.when(t == nt - 1)
    def _():
        for h in range(GLA_HEADS):
            sout_ref[h] = st_ref[h].T


def _gla_core(qkvg, glr, nseq, tlen, s0, w2, bgk, gnw):
    m = nseq * tlen
    c = CHUNK
    nt = tlen // c
    row = lambda b, t: b * nt + t
    s_spec = pl.BlockSpec((None, GLA_HEADS, HEAD_K, HEAD_V), lambda b, t: (b, 0, 0, 0))
    return pl.pallas_call(
        functools.partial(_gla_kernel, nt=nt),
        grid=(nseq, nt),
        in_specs=[pl.BlockSpec((c, GLA_DQ), lambda b, t: (row(b, t), 0)),
                  pl.BlockSpec((c, GLA_DQ), lambda b, t: (row(b, t), 1)),
                  pl.BlockSpec((c, GLA_DV), lambda b, t: (row(b, t), 1)),
                  pl.BlockSpec((c, GLA_DV), lambda b, t: (row(b, t), 2)),
                  pl.BlockSpec((c, LANES), lambda b, t: (row(b, t), 0)),
                  s_spec,
                  pl.BlockSpec((LANES, GLA_DQ), lambda b, t: (0, 0)),
                  pl.BlockSpec((1, GLA_DQ), lambda b, t: (0, 0)),
                  pl.BlockSpec((1, HEAD_V), lambda b, t: (0, 0))],
        out_specs=[pl.BlockSpec((c, GLA_DV), lambda b, t: (row(b, t), 0)), s_spec],
        out_shape=[jax.ShapeDtypeStruct((m, GLA_DV), BF16),
                   jax.ShapeDtypeStruct((nseq, GLA_HEADS, HEAD_K, HEAD_V), F32)],
        scratch_shapes=[pltpu.VMEM((GLA_HEADS, HEAD_V, HEAD_K), F32)],
        compiler_params=_params(("parallel", "arbitrary"), 32),
        name="gla_core")(qkvg, qkvg, qkvg, qkvg, glr, s0, w2, bgk, gnw)


def _trunk(x, h_all, conv_all, s_all, w):
    nseq, tlen, d = x.shape
    x = x.reshape(nseq * tlen, d)
    hs, cs, ss = [], [], []
    for layer in range(DEPTH):
        j = layer // 2
        nw = w["norm_mix"][layer]
        if layer % 2 == 0:
            gu = _norm_matmul(x, nw, w["rg_w_in"][j])
            hg, h_new, c_new = _rglru_core(gu, nseq, tlen, h_all[j][:, None, :], conv_all[j],
                                           w["rg_conv_w"][j], w["rg_conv_b"][j], w["rg_w_a"][j], w["rg_b_a"][j],
                                           w["rg_w_x"][j], w["rg_b_x"][j], w["rg_lambda"][j])
            hs.append(h_new[:, 0, :])
            cs.append(c_new)
            x = _matmul_res(hg, w["rg_w_out"][j], x)
        else:
            qkvg, glr = _norm_matmul(x, nw, w["gla_w_main"][j], w["gla_w_glr"][j])
            og, s_new = _gla_core(qkvg, glr, nseq, tlen, s_all[j], w["gla_w_gk2"][j], w["gla_b_gk"][j],
                                  w["gla_norm_w"][j])
            ss.append(s_new)
            x = _matmul_res(og, w["gla_w_out"][j], x)
        x = _ffn(x, w["norm_ffn"][layer], w["ffn_w_up"][layer], w["ffn_w_down"][layer], w["norm_final"],
                 final=(layer == DEPTH - 1))
    return x.reshape(nseq, tlen, d), jnp.stack(hs), jnp.stack(cs), jnp.stack(ss)


def _prepare_weights(norm_mix, norm_ffn, norm_final, rg_w_in, rg_conv_w, rg_conv_b, rg_w_a, rg_b_a, rg_w_x, rg_b_x,
                     rg_lambda, rg_w_out, gla_w_in, gla_w_gk2, gla_b_gk, gla_norm_w, gla_w_out, ffn_w_up, ffn_w_down):
    glr_pad = LANES - GATE_RANK
    return {
        "norm_mix": norm_mix[:, None, :],
        "norm_ffn": norm_ffn[:, None, :],
        "norm_final": norm_final[None, :],
        "rg_w_in": rg_w_in.astype(BF16),
        "rg_conv_w": rg_conv_w,
        "rg_conv_b": rg_conv_b[:, None, :],
        "rg_w_a": rg_w_a.astype(BF16),
        "rg_b_a": rg_b_a[:, None, :],
        "rg_w_x": rg_w_x.astype(BF16),
        "rg_b_x": rg_b_x[:, None, :],
        "rg_lambda": rg_lambda[:, None, :],
        "rg_w_out": rg_w_out.astype(BF16),
        "gla_w_main": gla_w_in[:, :, :GLA_MAIN].astype(BF16),
        "gla_w_glr": jnp.pad(gla_w_in[:, :, GLA_MAIN:], ((0, 0), (0, 0), (0, glr_pad))).astype(BF16),
        "gla_w_gk2": jnp.pad(gla_w_gk2, ((0, 0), (0, glr_pad), (0, 0))).astype(BF16),
        "gla_b_gk": gla_b_gk[:, None, :],
        "gla_norm_w": gla_norm_w[:, None, :],
        "gla_w_out": gla_w_out.astype(BF16),
        "ffn_w_up": ffn_w_up.astype(BF16),
        "ffn_w_down": ffn_w_down.astype(BF16),
    }


def kernel(x_prompt, x_sample, state_rglru_h, state_rglru_conv, state_gla, norm_mix, norm_ffn, norm_final, rg_w_in, rg_conv_w, rg_conv_b, rg_w_a, rg_b_a, rg_w_x, rg_b_x, rg_lambda, rg_w_out, gla_w_in, gla_w_gk2, gla_b_gk, gla_norm_w, gla_w_out, ffn_w_up, ffn_w_down):
    w = _prepare_weights(norm_mix, norm_ffn, norm_final, rg_w_in, rg_conv_w, rg_conv_b, rg_w_a, rg_b_a, rg_w_x,
                         rg_b_x, rg_lambda, rg_w_out, gla_w_in, gla_w_gk2, gla_b_gk, gla_norm_w, gla_w_out,
                         ffn_w_up, ffn_w_down)
    bp = x_prompt.shape[0]
    n_rg = state_rglru_h.shape[0]
    n_gla = state_gla.shape[0]
    h0 = jnp.zeros((n_rg, bp, D_RNN), F32)
    c0 = jnp.zeros((n_rg, bp, CONV_W - 1, D_RNN), F32)
    s0 = jnp.zeros((n_gla, bp, GLA_HEADS, HEAD_K, HEAD_V), F32)
    y_p, h_p, c_p, s_p = _trunk(x_prompt, h0, c0, s0, w)
    y_s, h_s, c_s, s_s = _trunk(x_sample, state_rglru_h, state_rglru_conv, state_gla, w)
    return (y_p, y_s, h_p, c_p, s_p, h_s, c_s, s_s)
```

```python
import functools

import jax
import jax.numpy as jnp
from jax import lax
from jax.experimental import pallas as pl
from jax.experimental.pallas import tpu as pltpu

F32 = jnp.float32
BF16 = jnp.bfloat16

D_MODEL = 2048
DEPTH = 4
CHUNK = 64
EPS = 1e-6
D_RNN = D_MODEL
RG_BLOCKS = 8
RG_BW = D_RNN // RG_BLOCKS
CONV_W = 4
RG_C = 8.0
GLA_HEADS = 4
HEAD_K = 256
HEAD_V = 512
GATE_RANK = 16
GATE_NORM = 16.0
GLA_DQ = GLA_HEADS * HEAD_K
GLA_DV = GLA_HEADS * HEAD_V
GLA_MAIN = 2 * GLA_DQ + 2 * GLA_DV
D_FF = 5632

LANES = 128
SUBLANES = 8
NORM_ROWS = 16
NORM_UNROLL = 8
MIB = 1024 * 1024

MATMUL_TM = 1024
MATMUL_TN = 1024
FFN_TM = 1024
FFN_TF = 512
RGLRU_TC = 512


def _params(dims, vmem_mib):
    return pltpu.CompilerParams(dimension_semantics=dims, vmem_limit_bytes=vmem_mib * MIB)


def _rms(x, w):
    ms = jnp.mean(x * x, axis=-1, keepdims=True)
    return x * lax.rsqrt(ms + EPS) * w


def _log_sigmoid(x):
    return jnp.minimum(x, 0.0) - jnp.log1p(jnp.exp(-jnp.abs(x)))


def _expm1(x):
    small = x * (1.0 + x * (0.5 + x * (1.0 / 6 + x * (1.0 / 24 + x * (1.0 / 120 + x * (1.0 / 720))))))
    return jnp.where(jnp.abs(x) < 0.1, small, jnp.exp(x) - 1.0)


def _norm_rows(src_ref, nw_ref, dst_ref, rows, dtype):
    nw = nw_ref[...]

    def body(r, carry):
        sl = pl.ds(pl.multiple_of(r * NORM_ROWS, NORM_ROWS), NORM_ROWS)
        dst_ref[sl, :] = _rms(src_ref[sl, :], nw).astype(dtype)
        return carry

    lax.fori_loop(0, rows // NORM_ROWS, body, 0, unroll=NORM_UNROLL)


def _norm_matmul_kernel(x_ref, nw_ref, w_ref, o_ref, hn_ref, *, tm):
    @pl.when(pl.program_id(1) == 0)
    def _():
        _norm_rows(x_ref, nw_ref, hn_ref, tm, BF16)

    o_ref[...] = jnp.dot(hn_ref[...], w_ref[...], preferred_element_type=F32)


def _norm_matmul2_kernel(x_ref, nw_ref, w_ref, w2_ref, o_ref, o2_ref, hn_ref, *, tm):
    @pl.when(pl.program_id(1) == 0)
    def _():
        _norm_rows(x_ref, nw_ref, hn_ref, tm, BF16)
        o2_ref[...] = jnp.dot(hn_ref[...], w2_ref[...], preferred_element_type=F32)

    o_ref[...] = jnp.dot(hn_ref[...], w_ref[...], preferred_element_type=F32)


def _norm_matmul(x, nws, layer, ws, j, w2s=None):
    m, d = x.shape
    n = ws.shape[2]
    tm = min(m, MATMUL_TM)
    tn = MATMUL_TN
    grid = (m // tm, n // tn)
    x_spec = pl.BlockSpec((tm, d), lambda i, c: (i, 0))
    nw_spec = pl.BlockSpec((None, 1, d), lambda i, c: (layer, 0, 0))
    w_spec = pl.BlockSpec((None, d, tn), lambda i, c: (j, 0, c))
    o_spec = pl.BlockSpec((tm, tn), lambda i, c: (i, c))
    scratch = [pltpu.VMEM((tm, d), BF16)]
    cp = _params(("parallel", "arbitrary"), 48)
    if w2s is None:
        return pl.pallas_call(
            functools.partial(_norm_matmul_kernel, tm=tm),
            grid=grid, in_specs=[x_spec, nw_spec, w_spec], out_specs=o_spec,
            out_shape=jax.ShapeDtypeStruct((m, n), F32), scratch_shapes=scratch,
            compiler_params=cp, name="norm_matmul")(x, nws, ws)
    n2 = w2s.shape[2]
    return pl.pallas_call(
        functools.partial(_norm_matmul2_kernel, tm=tm),
        grid=grid,
        in_specs=[x_spec, nw_spec, w_spec, pl.BlockSpec((None, d, n2), lambda i, c: (j, 0, 0))],
        out_specs=[o_spec, pl.BlockSpec((tm, n2), lambda i, c: (i, 0))],
        out_shape=[jax.ShapeDtypeStruct((m, n), F32), jax.ShapeDtypeStruct((m, n2), F32)],
        scratch_shapes=scratch, compiler_params=cp, name="norm_matmul2")(x, nws, ws, w2s)


def _matmul_res_kernel(a_ref, w_ref, x_ref, o_ref):
    o_ref[...] = x_ref[...] + jnp.dot(a_ref[...], w_ref[...], preferred_element_type=F32)


def _matmul_res(a, ws, j, x):
    m, k = a.shape
    n = ws.shape[2]
    tm = min(m, MATMUL_TM)
    tn = MATMUL_TN
    return pl.pallas_call(
        _matmul_res_kernel,
        grid=(m // tm, n // tn),
        in_specs=[pl.BlockSpec((tm, k), lambda i, c: (i, 0)),
                  pl.BlockSpec((None, k, tn), lambda i, c: (j, 0, c)),
                  pl.BlockSpec((tm, tn), lambda i, c: (i, c))],
        out_specs=pl.BlockSpec((tm, tn), lambda i, c: (i, c)),
        out_shape=jax.ShapeDtypeStruct((m, n), F32),
        compiler_params=_params(("parallel", "arbitrary"), 48), name="matmul_res")(a, ws, x)


def _ffn_kernel(x_ref, nw_ref, wg_ref, wu_ref, wd_ref, fw_ref, o_ref, hn_ref, *, tm, nf, final):
    f = pl.program_id(1)

    @pl.when(f == 0)
    def _():
        _norm_rows(x_ref, nw_ref, hn_ref, tm, BF16)
        o_ref[...] = x_ref[...]

    hn = hn_ref[...]
    gt = jnp.dot(hn, wg_ref[...], preferred_element_type=F32)
    up = jnp.dot(hn, wu_ref[...], preferred_element_type=F32)
    act = ((gt * jax.nn.sigmoid(gt)) * up).astype(BF16)
    o_ref[...] += jnp.dot(act, wd_ref[...], preferred_element_type=F32)

    if final:
        @pl.when(f == nf - 1)
        def _():
            _norm_rows(o_ref, fw_ref, o_ref, tm, F32)


def _ffn(x, nws, w_ups, w_downs, layer, fw, final):
    m, d = x.shape
    tm = min(m, FFN_TM)
    tf = FFN_TF
    nf = D_FF // tf
    return pl.pallas_call(
        functools.partial(_ffn_kernel, tm=tm, nf=nf, final=final),
        grid=(m // tm, nf),
        in_specs=[pl.BlockSpec((tm, d), lambda i, f: (i, 0), pipeline_mode=pl.Buffered(1)),
                  pl.BlockSpec((None, 1, d), lambda i, f: (layer, 0, 0)),
                  pl.BlockSpec((None, d, tf), lambda i, f: (layer, 0, f)),
                  pl.BlockSpec((None, d, tf), lambda i, f: (layer, 0, f + nf)),
                  pl.BlockSpec((None, tf, d), lambda i, f: (layer, f, 0)),
                  pl.BlockSpec((1, d), lambda i, f: (0, 0))],
        out_specs=pl.BlockSpec((tm, d), lambda i, f: (i, 0)),
        out_shape=jax.ShapeDtypeStruct((m, d), F32),
        scratch_shapes=[pltpu.VMEM((tm, d), BF16)],
        compiler_params=_params(("parallel", "arbitrary"), 56), name="ffn")(x, nws, w_ups, w_ups, w_downs, fw)


def _rglru_kernel(gate_ref, u_ref, h0_ref, c0_ref, cw_ref, cb_ref, wa_ref, ba_ref, wx_ref, bx_ref, lam_ref,
                  hg_ref, hout_ref, cout_ref, uext_ref, *, tc, nt):
    t = pl.program_id(2)
    pad = SUBLANES

    @pl.when(t == 0)
    def _():
        uext_ref[pad - (CONV_W - 1):pad, :] = c0_ref[...]
        hout_ref[...] = h0_ref[...]

    uext_ref[pad:pad + tc, :] = u_ref[...]
    cw = cw_ref[...]
    acc = uext_ref[pl.ds(pad - 3, tc), :] * cw[0:1, :]
    acc = acc + uext_ref[pl.ds(pad - 2, tc), :] * cw[1:2, :]
    acc = acc + uext_ref[pl.ds(pad - 1, tc), :] * cw[2:3, :]
    acc = acc + uext_ref[pl.ds(pad, tc), :] * cw[3:4, :]
    uc = cb_ref[...] + acc
    ucb = uc.astype(BF16)
    r = jax.nn.sigmoid(jnp.dot(ucb, wa_ref[...], preferred_element_type=F32) + ba_ref[...])
    i = jax.nn.sigmoid(jnp.dot(ucb, wx_ref[...], preferred_element_type=F32) + bx_ref[...])
    log_a = (RG_C * r) * _log_sigmoid(lam_ref[...])
    a = jnp.exp(log_a)
    b = jnp.sqrt(-_expm1(2.0 * log_a)) * (i * uc)

    groups = tc // SUBLANES
    a3 = a.reshape(groups, SUBLANES, RG_BW)
    b3 = b.reshape(groups, SUBLANES, RG_BW)
    row = lax.broadcasted_iota(jnp.int32, (groups, SUBLANES, RG_BW), 1)
    for s in (1, 2, 4):
        keep = row >= s
        a_prev = pltpu.roll(a3, s, 1)
        b_prev = pltpu.roll(b3, s, 1)
        b3 = jnp.where(keep, b3 + a3 * b_prev, b3)
        a3 = jnp.where(keep, a3 * a_prev, a3)
    h_prev = jnp.broadcast_to(hout_ref[...], (SUBLANES, RG_BW))
    hs = []
    for g in range(groups):
        h_g = a3[g] * h_prev + b3[g]
        hs.append(h_g)
        h_prev = jnp.broadcast_to(h_g[SUBLANES - 1:SUBLANES, :], (SUBLANES, RG_BW))
    h = jnp.concatenate(hs, axis=0)
    hout_ref[...] = hs[-1][SUBLANES - 1:SUBLANES, :]
    hg_ref[...] = (h * jax.nn.gelu(gate_ref[...])).astype(BF16)

    uext_ref[0:pad, :] = uext_ref[tc:tc + pad, :]

    @pl.when(t == nt - 1)
    def _():
        cout_ref[...] = uext_ref[pad - (CONV_W - 1):pad, :]


def _rglru_core(gu, nseq, tlen, j, h_all, c_all, w):
    m = nseq * tlen
    tc = min(tlen, RGLRU_TC)
    nt = tlen // tc
    bw = RG_BW
    row = lambda b, n, t: b * nt + t
    vec = pl.BlockSpec((None, 1, bw), lambda b, n, t: (j, 0, n))
    gate_spec = pl.BlockSpec((tc, bw), lambda b, n, t: (row(b, n, t), n))
    u_spec = pl.BlockSpec((tc, bw), lambda b, n, t: (row(b, n, t), RG_BLOCKS + n))
    hin_spec = pl.BlockSpec((None, None, 1, bw), lambda b, n, t: (j, b, 0, n))
    cin_spec = pl.BlockSpec((None, None, CONV_W - 1, bw), lambda b, n, t: (j, b, 0, n))
    hout_spec = pl.BlockSpec((None, 1, bw), lambda b, n, t: (b, 0, n))
    cout_spec = pl.BlockSpec((None, CONV_W - 1, bw), lambda b, n, t: (b, 0, n))
    w_spec = pl.BlockSpec((None, None, bw, bw), lambda b, n, t: (j, n, 0, 0))
    return pl.pallas_call(
        functools.partial(_rglru_kernel, tc=tc, nt=nt),
        grid=(nseq, RG_BLOCKS, nt),
        in_specs=[gate_spec, u_spec, hin_spec, cin_spec,
                  pl.BlockSpec((None, CONV_W, bw), lambda b, n, t: (j, 0, n)), vec,
                  w_spec, vec, w_spec, vec, vec],
        out_specs=[pl.BlockSpec((tc, bw), lambda b, n, t: (row(b, n, t), n)), hout_spec, cout_spec],
        out_shape=[jax.ShapeDtypeStruct((m, D_RNN), BF16),
                   jax.ShapeDtypeStruct((nseq, 1, D_RNN), F32),
                   jax.ShapeDtypeStruct((nseq, CONV_W - 1, D_RNN), F32)],
        scratch_shapes=[pltpu.VMEM((tc + SUBLANES, bw), F32)],
        compiler_params=_params(("parallel", "parallel", "arbitrary"), 32),
        name="rglru_core")(gu, gu, h_all, c_all, w["rg_conv_w"], w["rg_conv_b"], w["rg_w_a"], w["rg_b_a"],
                           w["rg_w_x"], w["rg_b_x"], w["rg_lambda"])


def _gla_kernel(q_ref, k_ref, v_ref, g_ref, glr_ref, s0_ref, w2_ref, bgk_ref, gnw_ref,
                o_ref, sout_ref, st_ref, *, nt):
    t = pl.program_id(1)

    @pl.when(t == 0)
    def _():
        for h in range(GLA_HEADS):
            st_ref[h] = s0_ref[h].T

    c = CHUNK
    gk = jnp.dot(glr_ref[...].astype(BF16), w2_ref[...], preferred_element_type=F32) + bgk_ref[...]
    gk = _log_sigmoid(gk) * (1.0 / GATE_NORM)
    rows = lax.broadcasted_iota(jnp.int32, (c, c), 0)
    cols = lax.broadcasted_iota(jnp.int32, (c, c), 1)
    tri = rows >= cols
    bcum = jnp.dot(tri.astype(F32), gk, precision=lax.Precision.HIGHEST, preferred_element_type=F32)
    blast = bcum[c - 1:c, :]
    k = k_ref[...]
    qe = ((q_ref[...] * (HEAD_K ** -0.5)) * jnp.exp(bcum)).astype(BF16)
    ke = (k * jnp.exp(-bcum)).astype(BF16)
    kd = (k * jnp.exp(blast - bcum)).astype(BF16)
    decay = jnp.exp(blast)
    gnw = gnw_ref[...]
    nt_dims = (((1,), (1,)), ((), ()))
    tn_dims = (((0,), (0,)), ((), ()))
    for h in range(GLA_HEADS):
        ks = slice(h * HEAD_K, (h + 1) * HEAD_K)
        vs = slice(h * HEAD_V, (h + 1) * HEAD_V)
        v_h = v_ref[:, vs].astype(BF16)
        att = lax.dot_general(qe[:, ks], ke[:, ks], nt_dims, preferred_element_type=F32)
        att = jnp.where(tri, att, 0.0).astype(BF16)
        st = st_ref[h]
        o = jnp.dot(att, v_h, preferred_element_type=F32)
        o = o + lax.dot_general(qe[:, ks], st.astype(BF16), nt_dims, preferred_element_type=F32)
        st_ref[h] = st * decay[:, ks] + lax.dot_general(v_h, kd[:, ks], tn_dims, preferred_element_type=F32)
        on = o * lax.rsqrt(jnp.mean(o * o, axis=-1, keepdims=True) + EPS) * gnw
        g_h = g_ref[:, vs]
        o_ref[:, vs] = (on * (g_h * jax.nn.sigmoid(g_h))).astype(BF16)

    @pl.when(t == nt - 1)
    def _():
        for h in range(GLA_HEADS):
            sout_ref[h] = st_ref[h].T


def _gla_core(qkvg, glr, nseq, tlen, j, s_all, w):
    m = nseq * tlen
    c = CHUNK
    nt = tlen // c
    row = lambda b, t: b * nt + t
    return pl.pallas_call(
        functools.partial(_gla_kernel, nt=nt),
        grid=(nseq, nt),
        in_specs=[pl.BlockSpec((c, GLA_DQ), lambda b, t: (row(b, t), 0)),
                  pl.BlockSpec((c, GLA_DQ), lambda b, t: (row(b, t), 1)),
                  pl.BlockSpec((c, GLA_DV), lambda b, t: (row(b, t), 1)),
                  pl.BlockSpec((c, GLA_DV), lambda b, t: (row(b, t), 2)),
                  pl.BlockSpec((c, LANES), lambda b, t: (row(b, t), 0)),
                  pl.BlockSpec((None, None, GLA_HEADS, HEAD_K, HEAD_V), lambda b, t: (j, b, 0, 0, 0)),
                  pl.BlockSpec((None, LANES, GLA_DQ), lambda b, t: (j, 0, 0)),
                  pl.BlockSpec((None, 1, GLA_DQ), lambda b, t: (j, 0, 0)),
                  pl.BlockSpec((None, 1, HEAD_V), lambda b, t: (j, 0, 0))],
        out_specs=[pl.BlockSpec((c, GLA_DV), lambda b, t: (row(b, t), 0)),
                   pl.BlockSpec((None, GLA_HEADS, HEAD_K, HEAD_V), lambda b, t: (b, 0, 0, 0))],
        out_shape=[jax.ShapeDtypeStruct((m, GLA_DV), BF16),
                   jax.ShapeDtypeStruct((nseq, GLA_HEADS, HEAD_K, HEAD_V), F32)],
        scratch_shapes=[pltpu.VMEM((GLA_HEADS, HEAD_V, HEAD_K), F32)],
        compiler_params=_params(("parallel", "arbitrary"), 32),
        name="gla_core")(qkvg, qkvg, qkvg, qkvg, glr, s_all, w["gla_w_gk2"], w["gla_b_gk"], w["gla_norm_w"])


def _trunk(x, h_all, conv_all, s_all, w):
    nseq, tlen, d = x.shape
    x = x.reshape(nseq * tlen, d)
    h_all = h_all[:, :, None, :]
    hs, cs, ss = [], [], []
    for layer in range(DEPTH):
        j = layer // 2
        if layer % 2 == 0:
            gu = _norm_matmul(x, w["norm_mix"], layer, w["rg_w_in"], j)
            hg, h_new, c_new = _rglru_core(gu, nseq, tlen, j, h_all, conv_all, w)
            hs.append(h_new[:, 0, :])
            cs.append(c_new)
            x = _matmul_res(hg, w["rg_w_out"], j, x)
        else:
            qkvg, glr = _norm_matmul(x, w["norm_mix"], layer, w["gla_w_main"], j, w["gla_w_glr"])
            og, s_new = _gla_core(qkvg, glr, nseq, tlen, j, s_all, w)
            ss.append(s_new)
            x = _matmul_res(og, w["gla_w_out"], j, x)
        x = _ffn(x, w["norm_ffn"], w["ffn_w_up"], w["ffn_w_down"], layer, w["norm_final"],
                 final=(layer == DEPTH - 1))
    return x.reshape(nseq, tlen, d), jnp.stack(hs), jnp.stack(cs), jnp.stack(ss)


def _prepare_weights(norm_mix, norm_ffn, norm_final, rg_w_in, rg_conv_w, rg_conv_b, rg_w_a, rg_b_a, rg_w_x, rg_b_x,
                     rg_lambda, rg_w_out, gla_w_in, gla_w_gk2, gla_b_gk, gla_norm_w, gla_w_out, ffn_w_up, ffn_w_down):
    glr_pad = LANES - GATE_RANK
    return {
        "norm_mix": norm_mix[:, None, :],
        "norm_ffn": norm_ffn[:, None, :],
        "norm_final": norm_final[None, :],
        "rg_w_in": rg_w_in.astype(BF16),
        "rg_conv_w": rg_conv_w,
        "rg_conv_b": rg_conv_b[:, None, :],
        "rg_w_a": rg_w_a.astype(BF16),
        "rg_b_a": rg_b_a[:, None, :],
        "rg_w_x": rg_w_x.astype(BF16),
        "rg_b_x": rg_b_x[:, None, :],
        "rg_lambda": rg_lambda[:, None, :],
        "rg_w_out": rg_w_out.astype(BF16),
        "gla_w_main": gla_w_in[:, :, :GLA_MAIN].astype(BF16),
        "gla_w_glr": jnp.pad(gla_w_in[:, :, GLA_MAIN:], ((0, 0), (0, 0), (0, glr_pad))).astype(BF16),
        "gla_w_gk2": jnp.pad(gla_w_gk2, ((0, 0), (0, glr_pad), (0, 0))).astype(BF16),
        "gla_b_gk": gla_b_gk[:, None, :],
        "gla_norm_w": gla_norm_w[:, None, :],
        "gla_w_out": gla_w_out.astype(BF16),
        "ffn_w_up": ffn_w_up.astype(BF16),
        "ffn_w_down": ffn_w_down.astype(BF16),
    }


def kernel(x_prompt, x_sample, state_rglru_h, state_rglru_conv, state_gla, norm_mix, norm_ffn, norm_final, rg_w_in, rg_conv_w, rg_conv_b, rg_w_a, rg_b_a, rg_w_x, rg_b_x, rg_lambda, rg_w_out, gla_w_in, gla_w_gk2, gla_b_gk, gla_norm_w, gla_w_out, ffn_w_up, ffn_w_down):
    w = _prepare_weights(norm_mix, norm_ffn, norm_final, rg_w_in, rg_conv_w, rg_conv_b, rg_w_a, rg_b_a, rg_w_x,
                         rg_b_x, rg_lambda, rg_w_out, gla_w_in, gla_w_gk2, gla_b_gk, gla_norm_w, gla_w_out,
                         ffn_w_up, ffn_w_down)
    bp = x_prompt.shape[0]
    n_rg = state_rglru_h.shape[0]
    n_gla = state_gla.shape[0]
    h0 = jnp.zeros((n_rg, bp, D_RNN), F32)
    c0 = jnp.zeros((n_rg, bp, CONV_W - 1, D_RNN), F32)
    s0 = jnp.zeros((n_gla, bp, GLA_HEADS, HEAD_K, HEAD_V), F32)
    y_p, h_p, c_p, s_p = _trunk(x_prompt, h0, c0, s0, w)
    y_s, h_s, c_s, s_s = _trunk(x_sample, state_rglru_h, state_rglru_conv, state_gla, w)
    return (y_p, y_s, h_p, c_p, s_p, h_s, c_s, s_s)
```

```python
import functools

import jax
import jax.numpy as jnp
from jax import lax
from jax.experimental import pallas as pl
from jax.experimental.pallas import tpu as pltpu

F32 = jnp.float32
BF16 = jnp.bfloat16

D_MODEL = 2048
DEPTH = 4
CHUNK = 64
EPS = 1e-6
D_RNN = D_MODEL
RG_BLOCKS = 8
RG_BW = D_RNN // RG_BLOCKS
CONV_W = 4
RG_C = 8.0
GLA_HEADS = 4
HEAD_K = 256
HEAD_V = 512
GATE_RANK = 16
GATE_NORM = 16.0
GLA_DQ = GLA_HEADS * HEAD_K
GLA_DV = GLA_HEADS * HEAD_V
GLA_MAIN = 2 * GLA_DQ + 2 * GLA_DV
D_FF = 5632

LANES = 128
SUBLANES = 8
NORM_ROWS = 16
NORM_UNROLL = 8
MIB = 1024 * 1024
TINY = 1e-30
GELU_C1 = 0.7978845608028654
GELU_C2 = 0.044715 * GELU_C1

MATMUL_TM = 1024
MATMUL_TN = 1024
FFN_TM = 1024
FFN_TF = 512
RGLRU_TC = 512


def _params(dims, vmem_mib):
    return pltpu.CompilerParams(dimension_semantics=dims, vmem_limit_bytes=vmem_mib * MIB)


def _rms(x, w):
    ms = jnp.mean(x * x, axis=-1, keepdims=True)
    return x * lax.rsqrt(ms + EPS) * w


def _log_sigmoid(x):
    return jnp.minimum(x, 0.0) - jnp.log1p(jnp.exp(-jnp.abs(x)))


def _sqrt_nonneg(y):
    return y * lax.rsqrt(jnp.maximum(y, TINY))


def _gelu_tanh(x):
    inner = x * (GELU_C1 + GELU_C2 * (x * x))
    return (0.5 * x) * (1.0 + jnp.tanh(inner))


def _norm_rows(src_ref, nw_ref, dst_ref, rows, dtype):
    nw = nw_ref[...]

    def body(r, carry):
        sl = pl.ds(pl.multiple_of(r * NORM_ROWS, NORM_ROWS), NORM_ROWS)
        dst_ref[sl, :] = _rms(src_ref[sl, :], nw).astype(dtype)
        return carry

    lax.fori_loop(0, rows // NORM_ROWS, body, 0, unroll=NORM_UNROLL)


def _norm_matmul_kernel(x_ref, nw_ref, w_ref, o_ref, hn_ref, *, tm):
    @pl.when(pl.program_id(1) == 0)
    def _():
        _norm_rows(x_ref, nw_ref, hn_ref, tm, BF16)

    o_ref[...] = jnp.dot(hn_ref[...], w_ref[...], preferred_element_type=F32)


def _norm_matmul2_kernel(x_ref, nw_ref, w_ref, w2_ref, o_ref, o2_ref, hn_ref, *, tm):
    @pl.when(pl.program_id(1) == 0)
    def _():
        _norm_rows(x_ref, nw_ref, hn_ref, tm, BF16)
        o2_ref[...] = jnp.dot(hn_ref[...], w2_ref[...], preferred_element_type=F32)

    o_ref[...] = jnp.dot(hn_ref[...], w_ref[...], preferred_element_type=F32)


def _norm_matmul(x, nws, layer, ws, j, w2s=None):
    m, d = x.shape
    n = ws.shape[2]
    tm = min(m, MATMUL_TM)
    tn = MATMUL_TN
    grid = (m // tm, n // tn)
    x_spec = pl.BlockSpec((tm, d), lambda i, c: (i, 0))
    nw_spec = pl.BlockSpec((None, 1, d), lambda i, c: (layer, 0, 0))
    w_spec = pl.BlockSpec((None, d, tn), lambda i, c: (j, 0, c))
    o_spec = pl.BlockSpec((tm, tn), lambda i, c: (i, c))
    scratch = [pltpu.VMEM((tm, d), BF16)]
    cp = _params(("parallel", "arbitrary"), 48)
    if w2s is None:
        return pl.pallas_call(
            functools.partial(_norm_matmul_kernel, tm=tm),
            grid=grid, in_specs=[x_spec, nw_spec, w_spec], out_specs=o_spec,
            out_shape=jax.ShapeDtypeStruct((m, n), F32), scratch_shapes=scratch,
            compiler_params=cp, name="norm_matmul")(x, nws, ws)
    n2 = w2s.shape[2]
    return pl.pallas_call(
        functools.partial(_norm_matmul2_kernel, tm=tm),
        grid=grid,
        in_specs=[x_spec, nw_spec, w_spec, pl.BlockSpec((None, d, n2), lambda i, c: (j, 0, 0))],
        out_specs=[o_spec, pl.BlockSpec((tm, n2), lambda i, c: (i, 0))],
        out_shape=[jax.ShapeDtypeStruct((m, n), F32), jax.ShapeDtypeStruct((m, n2), F32)],
        scratch_shapes=scratch, compiler_params=cp, name="norm_matmul2")(x, nws, ws, w2s)


def _matmul_res_kernel(a_ref, w_ref, x_ref, o_ref):
    o_ref[...] = x_ref[...] + jnp.dot(a_ref[...], w_ref[...], preferred_element_type=F32)


def _matmul_res(a, ws, j, x):
    m, k = a.shape
    n = ws.shape[2]
    tm = min(m, MATMUL_TM)
    tn = MATMUL_TN
    return pl.pallas_call(
        _matmul_res_kernel,
        grid=(m // tm, n // tn),
        in_specs=[pl.BlockSpec((tm, k), lambda i, c: (i, 0)),
                  pl.BlockSpec((None, k, tn), lambda i, c: (j, 0, c)),
                  pl.BlockSpec((tm, tn), lambda i, c: (i, c))],
        out_specs=pl.BlockSpec((tm, tn), lambda i, c: (i, c)),
        out_shape=jax.ShapeDtypeStruct((m, n), F32),
        compiler_params=_params(("parallel", "arbitrary"), 48), name="matmul_res")(a, ws, x)


def _ffn_kernel(x_ref, nw_ref, wg_ref, wu_ref, wd_ref, fw_ref, o_ref, hn_ref, *, tm, nf, final):
    f = pl.program_id(1)

    @pl.when(f == 0)
    def _():
        _norm_rows(x_ref, nw_ref, hn_ref, tm, BF16)
        o_ref[...] = x_ref[...]

    hn = hn_ref[...]
    gt = jnp.dot(hn, wg_ref[...], preferred_element_type=F32)
    up = jnp.dot(hn, wu_ref[...], preferred_element_type=F32)
    act = ((gt * jax.nn.sigmoid(gt)) * up).astype(BF16)
    o_ref[...] += jnp.dot(act, wd_ref[...], preferred_element_type=F32)

    if final:
        @pl.when(f == nf - 1)
        def _():
            _norm_rows(o_ref, fw_ref, o_ref, tm, F32)


def _ffn(x, nws, w_ups, w_downs, layer, fw, final):
    m, d = x.shape
    tm = min(m, FFN_TM)
    tf = FFN_TF
    nf = D_FF // tf
    return pl.pallas_call(
        functools.partial(_ffn_kernel, tm=tm, nf=nf, final=final),
        grid=(m // tm, nf),
        in_specs=[pl.BlockSpec((tm, d), lambda i, f: (i, 0), pipeline_mode=pl.Buffered(1)),
                  pl.BlockSpec((None, 1, d), lambda i, f: (layer, 0, 0)),
                  pl.BlockSpec((None, d, tf), lambda i, f: (layer, 0, f)),
                  pl.BlockSpec((None, d, tf), lambda i, f: (layer, 0, f + nf)),
                  pl.BlockSpec((None, tf, d), lambda i, f: (layer, f, 0)),
                  pl.BlockSpec((1, d), lambda i, f: (0, 0))],
        out_specs=pl.BlockSpec((tm, d), lambda i, f: (i, 0)),
        out_shape=jax.ShapeDtypeStruct((m, d), F32),
        scratch_shapes=[pltpu.VMEM((tm, d), BF16)],
        compiler_params=_params(("parallel", "arbitrary"), 56), name="ffn")(x, nws, w_ups, w_ups, w_downs, fw)


def _rglru_kernel(gate_ref, u_ref, h0_ref, c0_ref, cw_ref, cb_ref, wa_ref, ba_ref, wx_ref, bx_ref, lam_ref,
                  hg_ref, hout_ref, cout_ref, uext_ref, *, tc, nt):
    t = pl.program_id(2)
    pad = SUBLANES

    @pl.when(t == 0)
    def _():
        uext_ref[...] = jnp.zeros((pad, RG_BW), F32)
        uext_ref[pad - (CONV_W - 1):pad, :] = c0_ref[...]
        hout_ref[...] = h0_ref[...]

    u = u_ref[...]
    ext = jnp.concatenate([uext_ref[...], u], axis=0)
    cw = cw_ref[...]
    acc = pltpu.roll(ext, 3, 0)[pad:, :] * cw[0:1, :]
    acc = acc + pltpu.roll(ext, 2, 0)[pad:, :] * cw[1:2, :]
    acc = acc + pltpu.roll(ext, 1, 0)[pad:, :] * cw[2:3, :]
    acc = acc + u * cw[3:4, :]
    uc = cb_ref[...] + acc
    ucb = uc.astype(BF16)
    r = jax.nn.sigmoid(jnp.dot(ucb, wa_ref[...], preferred_element_type=F32) + ba_ref[...])
    i = jax.nn.sigmoid(jnp.dot(ucb, wx_ref[...], preferred_element_type=F32) + bx_ref[...])
    log_a = (RG_C * r) * _log_sigmoid(lam_ref[...])
    a = jnp.exp(log_a)
    b = _sqrt_nonneg(1.0 - a * a) * (i * uc)

    groups = tc // SUBLANES
    a3 = a.reshape(groups, SUBLANES, RG_BW)
    b3 = b.reshape(groups, SUBLANES, RG_BW)
    row = lax.broadcasted_iota(jnp.int32, (groups, SUBLANES, RG_BW), 1)
    for s in (1, 2, 4):
        keep = row >= s
        a_prev = pltpu.roll(a3, s, 1)
        b_prev = pltpu.roll(b3, s, 1)
        b3 = jnp.where(keep, b3 + a3 * b_prev, b3)
        a3 = jnp.where(keep, a3 * a_prev, a3)
    h_prev = jnp.broadcast_to(hout_ref[...], (SUBLANES, RG_BW))
    hs = []
    for g in range(groups):
        h_g = a3[g] * h_prev + b3[g]
        hs.append(h_g)
        h_prev = jnp.broadcast_to(h_g[SUBLANES - 1:SUBLANES, :], (SUBLANES, RG_BW))
    h = jnp.concatenate(hs, axis=0)
    hout_ref[...] = hs[-1][SUBLANES - 1:SUBLANES, :]
    hg_ref[...] = (h * _gelu_tanh(gate_ref[...])).astype(BF16)

    uext_ref[...] = u[tc - pad:, :]

    @pl.when(t == nt - 1)
    def _():
        cout_ref[...] = uext_ref[pad - (CONV_W - 1):pad, :]


def _rglru_core(gu, nseq, tlen, j, h_all, c_all, w):
    m = nseq * tlen
    tc = min(tlen, RGLRU_TC)
    nt = tlen // tc
    bw = RG_BW
    row = lambda b, n, t: b * nt + t
    vec = pl.BlockSpec((None, 1, bw), lambda b, n, t: (j, 0, n))
    gate_spec = pl.BlockSpec((tc, bw), lambda b, n, t: (row(b, n, t), n))
    u_spec = pl.BlockSpec((tc, bw), lambda b, n, t: (row(b, n, t), RG_BLOCKS + n))
    hin_spec = pl.BlockSpec((None, None, 1, bw), lambda b, n, t: (j, b, 0, n))
    cin_spec = pl.BlockSpec((None, None, CONV_W - 1, bw), lambda b, n, t: (j, b, 0, n))
    hout_spec = pl.BlockSpec((None, 1, bw), lambda b, n, t: (b, 0, n))
    cout_spec = pl.BlockSpec((None, CONV_W - 1, bw), lambda b, n, t: (b, 0, n))
    w_spec = pl.BlockSpec((None, None, bw, bw), lambda b, n, t: (j, n, 0, 0))
    return pl.pallas_call(
        functools.partial(_rglru_kernel, tc=tc, nt=nt),
        grid=(nseq, RG_BLOCKS, nt),
        in_specs=[gate_spec, u_spec, hin_spec, cin_spec,
                  pl.BlockSpec((None, CONV_W, bw), lambda b, n, t: (j, 0, n)), vec,
                  w_spec, vec, w_spec, vec, vec],
        out_specs=[pl.BlockSpec((tc, bw), lambda b, n, t: (row(b, n, t), n)), hout_spec, cout_spec],
        out_shape=[jax.ShapeDtypeStruct((m, D_RNN), BF16),
                   jax.ShapeDtypeStruct((nseq, 1, D_RNN), F32),
                   jax.ShapeDtypeStruct((nseq, CONV_W - 1, D_RNN), F32)],
        scratch_shapes=[pltpu.VMEM((SUBLANES, bw), F32)],
        compiler_params=_params(("parallel", "parallel", "arbitrary"), 32),
        name="rglru_core")(gu, gu, h_all, c_all, w["rg_conv_w"], w["rg_conv_b"], w["rg_w_a"], w["rg_b_a"],
                           w["rg_w_x"], w["rg_b_x"], w["rg_lambda"])


def _gla_kernel(q_ref, k_ref, v_ref, g_ref, glr_ref, s0_ref, w2_ref, bgk_ref, gnw_ref,
                o_ref, sout_ref, st_ref, *, nt):
    t = pl.program_id(1)

    @pl.when(t == 0)
    def _():
        for h in range(GLA_HEADS):
            st_ref[h] = s0_ref[h].T

    c = CHUNK
    gk = jnp.dot(glr_ref[...].astype(BF16), w2_ref[...], preferred_element_type=F32) + bgk_ref[...]
    gk = _log_sigmoid(gk) * (1.0 / GATE_NORM)
    rows = lax.broadcasted_iota(jnp.int32, (c, c), 0)
    cols = lax.broadcasted_iota(jnp.int32, (c, c), 1)
    tri = rows >= cols
    bcum = jnp.dot(tri.astype(F32), gk, precision=lax.Precision.HIGHEST, preferred_element_type=F32)
    blast = bcum[c - 1:c, :]
    k = k_ref[...]
    qe = ((q_ref[...] * (HEAD_K ** -0.5)) * jnp.exp(bcum)).astype(BF16)
    ke = (k * jnp.exp(-bcum)).astype(BF16)
    kd = (k * jnp.exp(blast - bcum)).astype(BF16)
    decay = jnp.exp(blast)
    gnw = gnw_ref[...]
    nt_dims = (((1,), (1,)), ((), ()))
    tn_dims = (((0,), (0,)), ((), ()))
    for h in range(GLA_HEADS):
        ks = slice(h * HEAD_K, (h + 1) * HEAD_K)
        vs = slice(h * HEAD_V, (h + 1) * HEAD_V)
        v_h = v_ref[:, vs].astype(BF16)
        att = lax.dot_general(qe[:, ks], ke[:, ks], nt_dims, preferred_element_type=F32)
        att = jnp.where(tri, att, 0.0).astype(BF16)
        st = st_ref[h]
        o = jnp.dot(att, v_h, preferred_element_type=F32)
        o = o + lax.dot_general(qe[:, ks], st.astype(BF16), nt_dims, preferred_element_type=F32)
        st_ref[h] = st * decay[:, ks] + lax.dot_general(v_h, kd[:, ks], tn_dims, preferred_element_type=F32)
        on = o * lax.rsqrt(jnp.mean(o * o, axis=-1, keepdims=True) + EPS) * gnw
        g_h = g_ref[:, vs]
        o_ref[:, vs] = (on * (g_h * jax.nn.sigmoid(g_h))).astype(BF16)

    @pl.when(t == nt - 1)
    def _():
        for h in range(GLA_HEADS):
            sout_ref[h] = st_ref[h].T


def _gla_core(qkvg, glr, nseq, tlen, j, s_all, w):
    m = nseq * tlen
    c = CHUNK
    nt = tlen // c
    row = lambda b, t: b * nt + t
    return pl.pallas_call(
        functools.partial(_gla_kernel, nt=nt),
        grid=(nseq, nt),
        in_specs=[pl.BlockSpec((c, GLA_DQ), lambda b, t: (row(b, t), 0)),
                  pl.BlockSpec((c, GLA_DQ), lambda b, t: (row(b, t), 1)),
                  pl.BlockSpec((c, GLA_DV), lambda b, t: (row(b, t), 1)),
                  pl.BlockSpec((c, GLA_DV), lambda b, t: (row(b, t), 2)),
                  pl.BlockSpec((c, LANES), lambda b, t: (row(b, t), 0)),
                  pl.BlockSpec((None, None, GLA_HEADS, HEAD_K, HEAD_V), lambda b, t: (j, b, 0, 0, 0)),
                  pl.BlockSpec((None, LANES, GLA_DQ), lambda b, t: (j, 0, 0)),
                  pl.BlockSpec((None, 1, GLA_DQ), lambda b, t: (j, 0, 0)),
                  pl.BlockSpec((None, 1, HEAD_V), lambda b, t: (j, 0, 0))],
        out_specs=[pl.BlockSpec((c, GLA_DV), lambda b, t: (row(b, t), 0)),
                   pl.BlockSpec((None, GLA_HEADS, HEAD_K, HEAD_V), lambda b, t: (b, 0, 0, 0))],
        out_shape=[jax.ShapeDtypeStruct((m, GLA_DV), BF16),
                   jax.ShapeDtypeStruct((nseq, GLA_HEADS, HEAD_K, HEAD_V), F32)],
        scratch_shapes=[pltpu.VMEM((GLA_HEADS, HEAD_V, HEAD_K), F32)],
        compiler_params=_params(("parallel", "arbitrary"), 32),
        name="gla_core")(qkvg, qkvg, qkvg, qkvg, glr, s_all, w["gla_w_gk2"], w["gla_b_gk"], w["gla_norm_w"])


def _trunk(x, h_all, conv_all, s_all, w):
    nseq, tlen, d = x.shape
    x = x.reshape(nseq * tlen, d)
    h_all = h_all[:, :, None, :]
    hs, cs, ss = [], [], []
    for layer in range(DEPTH):
        j = layer // 2
        if layer % 2 == 0:
            gu = _norm_matmul(x, w["norm_mix"], layer, w["rg_w_in"], j)
            hg, h_new, c_new = _rglru_core(gu, nseq, tlen, j, h_all, conv_all, w)
            hs.append(h_new[:, 0, :])
            cs.append(c_new)
            x = _matmul_res(hg, w["rg_w_out"], j, x)
        else:
            qkvg, glr = _norm_matmul(x, w["norm_mix"], layer, w["gla_w_main"], j, w["gla_w_glr"])
            og, s_new = _gla_core(qkvg, glr, nseq, tlen, j, s_all, w)
            ss.append(s_new)
            x = _matmul_res(og, w["gla_w_out"], j, x)
        x = _ffn(x, w["norm_ffn"], w["ffn_w_up"], w["ffn_w_down"], layer, w["norm_final"],
                 final=(layer == DEPTH - 1))
    return x.reshape(nseq, tlen, d), jnp.stack(hs), jnp.stack(cs), jnp.stack(ss)


def _prepare_weights(norm_mix, norm_ffn, norm_final, rg_w_in, rg_conv_w, rg_conv_b, rg_w_a, rg_b_a, rg_w_x, rg_b_x,
                     rg_lambda, rg_w_out, gla_w_in, gla_w_gk2, gla_b_gk, gla_norm_w, gla_w_out, ffn_w_up, ffn_w_down):
    glr_pad = LANES - GATE_RANK
    return {
        "norm_mix": norm_mix[:, None, :],
        "norm_ffn": norm_ffn[:, None, :],
        "norm_final": norm_final[None, :],
        "rg_w_in": rg_w_in.astype(BF16),
        "rg_conv_w": rg_conv_w,
        "rg_conv_b": rg_conv_b[:, None, :],
        "rg_w_a": rg_w_a.astype(BF16),
        "rg_b_a": rg_b_a[:, None, :],
        "rg_w_x": rg_w_x.astype(BF16),
        "rg_b_x": rg_b_x[:, None, :],
        "rg_lambda": rg_lambda[:, None, :],
        "rg_w_out": rg_w_out.astype(BF16),
        "gla_w_main": gla_w_in[:, :, :GLA_MAIN].astype(BF16),
        "gla_w_glr": jnp.pad(gla_w_in[:, :, GLA_MAIN:], ((0, 0), (0, 0), (0, glr_pad))).astype(BF16),
        "gla_w_gk2": jnp.pad(gla_w_gk2, ((0, 0), (0, glr_pad), (0, 0))).astype(BF16),
        "gla_b_gk": gla_b_gk[:, None, :],
        "gla_norm_w": gla_norm_w[:, None, :],
        "gla_w_out": gla_w_out.astype(BF16),
        "ffn_w_up": ffn_w_up.astype(BF16),
        "ffn_w_down": ffn_w_down.astype(BF16),
    }


def kernel(x_prompt, x_sample, state_rglru_h, state_rglru_conv, state_gla, norm_mix, norm_ffn, norm_final, rg_w_in, rg_conv_w, rg_conv_b, rg_w_a, rg_b_a, rg_w_x, rg_b_x, rg_lambda, rg_w_out, gla_w_in, gla_w_gk2, gla_b_gk, gla_norm_w, gla_w_out, ffn_w_up, ffn_w_down):
    w = _prepare_weights(norm_mix, norm_ffn, norm_final, rg_w_in, rg_conv_w, rg_conv_b, rg_w_a, rg_b_a, rg_w_x,
                         rg_b_x, rg_lambda, rg_w_out, gla_w_in, gla_w_gk2, gla_b_gk, gla_norm_w, gla_w_out,
                         ffn_w_up, ffn_w_down)
    bp = x_prompt.shape[0]
    n_rg = state_rglru_h.shape[0]
    n_gla = state_gla.shape[0]
    h0 = jnp.zeros((n_rg, bp, D_RNN), F32)
    c0 = jnp.zeros((n_rg, bp, CONV_W - 1, D_RNN), F32)
    s0 = jnp.zeros((n_gla, bp, GLA_HEADS, HEAD_K, HEAD_V), F32)
    y_p, h_p, c_p, s_p = _trunk(x_prompt, h0, c0, s0, w)
    y_s, h_s, c_s, s_s = _trunk(x_sample, state_rglru_h, state_rglru_conv, state_gla, w)
    return (y_p, y_s, h_p, c_p, s_p, h_s, c_s, s_s)
```

```python
import functools

import jax
import jax.numpy as jnp
from jax import lax
from jax.experimental import pallas as pl
from jax.experimental.pallas import tpu as pltpu

F32 = jnp.float32
BF16 = jnp.bfloat16

D_MODEL = 2048
DEPTH = 4
CHUNK = 64
EPS = 1e-6
D_RNN = D_MODEL
RG_BLOCKS = 8
RG_BW = D_RNN // RG_BLOCKS
CONV_W = 4
RG_C = 8.0
GLA_HEADS = 4
HEAD_K = 256
HEAD_V = 512
GATE_RANK = 16
GATE_NORM = 16.0
GLA_DQ = GLA_HEADS * HEAD_K
GLA_DV = GLA_HEADS * HEAD_V
GLA_MAIN = 2 * GLA_DQ + 2 * GLA_DV
D_FF = 5632

LANES = 128
SUBLANES = 8
NORM_ROWS = 16
NORM_UNROLL = 8
MIB = 1024 * 1024
TINY = 1e-30
GELU_C1 = 0.7978845608028654
GELU_C2 = 0.044715 * GELU_C1

MATMUL_TM = 1024
MATMUL_TN = 1024
FFN_TM = 1024
FFN_TF = 512
RGLRU_TC = 512
GLA_SEQS = 4


def _params(dims, vmem_mib):
    return pltpu.CompilerParams(dimension_semantics=dims, vmem_limit_bytes=vmem_mib * MIB)


def _rms(x, w):
    ms = jnp.mean(x * x, axis=-1, keepdims=True)
    return x * lax.rsqrt(ms + EPS) * w


def _log_sigmoid(x):
    return jnp.minimum(x, 0.0) - jnp.log(1.0 + jnp.exp(-jnp.abs(x)))


def _split3_bf16(x):
    hi = x.astype(BF16)
    r1 = x - hi.astype(F32)
    mid = r1.astype(BF16)
    lo = (r1 - mid.astype(F32)).astype(BF16)
    return hi, mid, lo


def _sqrt_nonneg(y):
    return y * lax.rsqrt(jnp.maximum(y, TINY))


def _gelu_tanh(x):
    inner = x * (GELU_C1 + GELU_C2 * (x * x))
    return (0.5 * x) * (1.0 + jnp.tanh(inner))


def _norm_rows(src_ref, nw_ref, dst_ref, rows, dtype):
    nw = nw_ref[...]

    def body(r, carry):
        sl = pl.ds(pl.multiple_of(r * NORM_ROWS, NORM_ROWS), NORM_ROWS)
        dst_ref[sl, :] = _rms(src_ref[sl, :], nw).astype(dtype)
        return carry

    lax.fori_loop(0, rows // NORM_ROWS, body, 0, unroll=NORM_UNROLL)


def _norm_matmul_kernel(x_ref, nw_ref, w_ref, o_ref, hn_ref, *, tm):
    @pl.when(pl.program_id(1) == 0)
    def _():
        _norm_rows(x_ref, nw_ref, hn_ref, tm, BF16)

    o_ref[...] = jnp.dot(hn_ref[...], w_ref[...], preferred_element_type=F32)


def _norm_matmul2_kernel(x_ref, nw_ref, w_ref, w2_ref, o_ref, o2_ref, hn_ref, *, tm):
    @pl.when(pl.program_id(1) == 0)
    def _():
        _norm_rows(x_ref, nw_ref, hn_ref, tm, BF16)
        o2_ref[...] = jnp.dot(hn_ref[...], w2_ref[...], preferred_element_type=F32)

    o_ref[...] = jnp.dot(hn_ref[...], w_ref[...], preferred_element_type=F32)


def _norm_matmul(x, nws, layer, ws, j, w2s=None):
    m, d = x.shape
    n = ws.shape[2]
    tm = min(m, MATMUL_TM)
    tn = MATMUL_TN
    grid = (m // tm, n // tn)
    x_spec = pl.BlockSpec((tm, d), lambda i, c: (i, 0))
    nw_spec = pl.BlockSpec((None, 1, d), lambda i, c: (layer, 0, 0))
    w_spec = pl.BlockSpec((None, d, tn), lambda i, c: (j, 0, c))
    o_spec = pl.BlockSpec((tm, tn), lambda i, c: (i, c))
    scratch = [pltpu.VMEM((tm, d), BF16)]
    cp = _params(("parallel", "arbitrary"), 48)
    if w2s is None:
        return pl.pallas_call(
            functools.partial(_norm_matmul_kernel, tm=tm),
            grid=grid, in_specs=[x_spec, nw_spec, w_spec], out_specs=o_spec,
            out_shape=jax.ShapeDtypeStruct((m, n), F32), scratch_shapes=scratch,
            compiler_params=cp, name="norm_matmul")(x, nws, ws)
    n2 = w2s.shape[2]
    return pl.pallas_call(
        functools.partial(_norm_matmul2_kernel, tm=tm),
        grid=grid,
        in_specs=[x_spec, nw_spec, w_spec, pl.BlockSpec((None, d, n2), lambda i, c: (j, 0, 0))],
        out_specs=[o_spec, pl.BlockSpec((tm, n2), lambda i, c: (i, 0))],
        out_shape=[jax.ShapeDtypeStruct((m, n), F32), jax.ShapeDtypeStruct((m, n2), F32)],
        scratch_shapes=scratch, compiler_params=cp, name="norm_matmul2")(x, nws, ws, w2s)


def _matmul_res_kernel(a_ref, w_ref, x_ref, o_ref):
    o_ref[...] = x_ref[...] + jnp.dot(a_ref[...], w_ref[...], preferred_element_type=F32)


def _matmul_res(a, ws, j, x):
    m, k = a.shape
    n = ws.shape[2]
    tm = min(m, MATMUL_TM)
    tn = MATMUL_TN
    return pl.pallas_call(
        _matmul_res_kernel,
        grid=(m // tm, n // tn),
        in_specs=[pl.BlockSpec((tm, k), lambda i, c: (i, 0)),
                  pl.BlockSpec((None, k, tn), lambda i, c: (j, 0, c)),
                  pl.BlockSpec((tm, tn), lambda i, c: (i, c))],
        out_specs=pl.BlockSpec((tm, tn), lambda i, c: (i, c)),
        out_shape=jax.ShapeDtypeStruct((m, n), F32),
        compiler_params=_params(("parallel", "arbitrary"), 48), name="matmul_res")(a, ws, x)


def _ffn_kernel(x_ref, nw_ref, wg_ref, wu_ref, wd_ref, fw_ref, o_ref, hn_ref, *, tm, nf, final):
    f = pl.program_id(1)

    @pl.when(f == 0)
    def _():
        _norm_rows(x_ref, nw_ref, hn_ref, tm, BF16)
        o_ref[...] = x_ref[...]

    hn = hn_ref[...]
    gt = jnp.dot(hn, wg_ref[...], preferred_element_type=F32)
    up = jnp.dot(hn, wu_ref[...], preferred_element_type=F32)
    act = ((gt * jax.nn.sigmoid(gt)) * up).astype(BF16)
    o_ref[...] += jnp.dot(act, wd_ref[...], preferred_element_type=F32)

    if final:
        @pl.when(f == nf - 1)
        def _():
            _norm_rows(o_ref, fw_ref, o_ref, tm, F32)


def _ffn(x, nws, w_ups, w_downs, layer, fw, final):
    m, d = x.shape
    tm = min(m, FFN_TM)
    tf = FFN_TF
    nf = D_FF // tf
    return pl.pallas_call(
        functools.partial(_ffn_kernel, tm=tm, nf=nf, final=final),
        grid=(m // tm, nf),
        in_specs=[pl.BlockSpec((tm, d), lambda i, f: (i, 0), pipeline_mode=pl.Buffered(1)),
                  pl.BlockSpec((None, 1, d), lambda i, f: (layer, 0, 0)),
                  pl.BlockSpec((None, d, tf), lambda i, f: (layer, 0, f)),
                  pl.BlockSpec((None, d, tf), lambda i, f: (layer, 0, f + nf)),
                  pl.BlockSpec((None, tf, d), lambda i, f: (layer, f, 0)),
                  pl.BlockSpec((1, d), lambda i, f: (0, 0))],
        out_specs=pl.BlockSpec((tm, d), lambda i, f: (i, 0)),
        out_shape=jax.ShapeDtypeStruct((m, d), F32),
        scratch_shapes=[pltpu.VMEM((tm, d), BF16)],
        compiler_params=_params(("parallel", "arbitrary"), 56), name="ffn")(x, nws, w_ups, w_ups, w_downs, fw)


def _rglru_kernel(gate_ref, u_ref, h0_ref, c0_ref, cw_ref, cb_ref, wa_ref, ba_ref, wx_ref, bx_ref, lam_ref,
                  hg_ref, hout_ref, cout_ref, uext_ref, *, tc, nt):
    t = pl.program_id(2)
    pad = SUBLANES

    @pl.when(t == 0)
    def _():
        uext_ref[...] = jnp.zeros((pad, RG_BW), F32)
        uext_ref[pad - (CONV_W - 1):pad, :] = c0_ref[...]
        hout_ref[...] = h0_ref[...]

    u = u_ref[...]
    ext = jnp.concatenate([uext_ref[...], u], axis=0)
    cw = cw_ref[...]
    acc = pltpu.roll(ext, 3, 0)[pad:, :] * cw[0:1, :]
    acc = acc + pltpu.roll(ext, 2, 0)[pad:, :] * cw[1:2, :]
    acc = acc + pltpu.roll(ext, 1, 0)[pad:, :] * cw[2:3, :]
    acc = acc + u * cw[3:4, :]
    uc = cb_ref[...] + acc
    ucb = uc.astype(BF16)
    r = jax.nn.sigmoid(jnp.dot(ucb, wa_ref[...], preferred_element_type=F32) + ba_ref[...])
    i = jax.nn.sigmoid(jnp.dot(ucb, wx_ref[...], preferred_element_type=F32) + bx_ref[...])
    log_a = (RG_C * r) * _log_sigmoid(lam_ref[...])
    a = jnp.exp(log_a)
    b = _sqrt_nonneg(1.0 - a * a) * (i * uc)

    groups = tc // SUBLANES
    a3 = a.reshape(groups, SUBLANES, RG_BW)
    b3 = b.reshape(groups, SUBLANES, RG_BW)
    row = lax.broadcasted_iota(jnp.int32, (groups, SUBLANES, RG_BW), 1)
    for s in (1, 2, 4):
        keep = row >= s
        a_prev = pltpu.roll(a3, s, 1)
        b_prev = pltpu.roll(b3, s, 1)
        b3 = jnp.where(keep, b3 + a3 * b_prev, b3)
        a3 = jnp.where(keep, a3 * a_prev, a3)
    h_prev = jnp.broadcast_to(hout_ref[...], (SUBLANES, RG_BW))
    hs = []
    for g in range(groups):
        h_g = a3[g] * h_prev + b3[g]
        hs.append(h_g)
        h_prev = jnp.broadcast_to(h_g[SUBLANES - 1:SUBLANES, :], (SUBLANES, RG_BW))
    h = jnp.concatenate(hs, axis=0)
    hout_ref[...] = hs[-1][SUBLANES - 1:SUBLANES, :]
    hg_ref[...] = (h * _gelu_tanh(gate_ref[...])).astype(BF16)

    uext_ref[...] = u[tc - pad:, :]

    @pl.when(t == nt - 1)
    def _():
        cout_ref[...] = uext_ref[pad - (CONV_W - 1):pad, :]


def _rglru_core(gu, nseq, tlen, j, h_all, c_all, w):
    m = nseq * tlen
    tc = min(tlen, RGLRU_TC)
    nt = tlen // tc
    bw = RG_BW
    row = lambda b, n, t: b * nt + t
    vec = pl.BlockSpec((None, 1, bw), lambda b, n, t: (j, 0, n))
    gate_spec = pl.BlockSpec((tc, bw), lambda b, n, t: (row(b, n, t), n))
    u_spec = pl.BlockSpec((tc, bw), lambda b, n, t: (row(b, n, t), RG_BLOCKS + n))
    hin_spec = pl.BlockSpec((None, None, 1, bw), lambda b, n, t: (j, b, 0, n))
    cin_spec = pl.BlockSpec((None, None, CONV_W - 1, bw), lambda b, n, t: (j, b, 0, n))
    hout_spec = pl.BlockSpec((None, 1, bw), lambda b, n, t: (b, 0, n))
    cout_spec = pl.BlockSpec((None, CONV_W - 1, bw), lambda b, n, t: (b, 0, n))
    w_spec = pl.BlockSpec((None, None, bw, bw), lambda b, n, t: (j, n, 0, 0))
    return pl.pallas_call(
        functools.partial(_rglru_kernel, tc=tc, nt=nt),
        grid=(nseq, RG_BLOCKS, nt),
        in_specs=[gate_spec, u_spec, hin_spec, cin_spec,
                  pl.BlockSpec((None, CONV_W, bw), lambda b, n, t: (j, 0, n)), vec,
                  w_spec, vec, w_spec, vec, vec],
        out_specs=[pl.BlockSpec((tc, bw), lambda b, n, t: (row(b, n, t), n)), hout_spec, cout_spec],
        out_shape=[jax.ShapeDtypeStruct((m, D_RNN), BF16),
                   jax.ShapeDtypeStruct((nseq, 1, D_RNN), F32),
                   jax.ShapeDtypeStruct((nseq, CONV_W - 1, D_RNN), F32)],
        scratch_shapes=[pltpu.VMEM((SUBLANES, bw), F32)],
        compiler_params=_params(("parallel", "parallel", "arbitrary"), 32),
        name="rglru_core")(gu, gu, h_all, c_all, w["rg_conv_w"], w["rg_conv_b"], w["rg_w_a"], w["rg_b_a"],
                           w["rg_w_x"], w["rg_b_x"], w["rg_lambda"])


def _gla_kernel(q_ref, k_ref, v_ref, g_ref, glr_ref, s0_ref, w2_ref, bgk_ref, gnw_ref,
                o_ref, sout_ref, st_ref, *, nt, nb):
    t = pl.program_id(1)

    @pl.when(t == 0)
    def _():
        for s in range(nb):
            for h in range(GLA_HEADS):
                st_ref[s, h] = s0_ref[s, h].T

    c = CHUNK
    glr = glr_ref[...].reshape(nb * c, LANES).astype(BF16)
    gk_all = jnp.dot(glr, w2_ref[...], preferred_element_type=F32) + bgk_ref[...]
    gk_all = _log_sigmoid(gk_all) * (1.0 / GATE_NORM)
    rows = lax.broadcasted_iota(jnp.int32, (c, c), 0)
    cols = lax.broadcasted_iota(jnp.int32, (c, c), 1)
    tri = rows >= cols
    tri_b = tri.astype(BF16)
    gnw = gnw_ref[...]
    nt_dims = (((1,), (1,)), ((), ()))
    tn_dims = (((0,), (0,)), ((), ()))
    for s in range(nb):
        gk = gk_all[s * c:(s + 1) * c, :]
        bcum = sum(jnp.dot(tri_b, piece, preferred_element_type=F32) for piece in _split3_bf16(gk))
        blast = bcum[c - 1:c, :]
        k = k_ref[s]
        qe = ((q_ref[s] * (HEAD_K ** -0.5)) * jnp.exp(bcum)).astype(BF16)
        ke = (k * jnp.exp(-bcum)).astype(BF16)
        kd = (k * jnp.exp(blast - bcum)).astype(BF16)
        decay = jnp.exp(blast)
        for h in range(GLA_HEADS):
            ks = slice(h * HEAD_K, (h + 1) * HEAD_K)
            vs = slice(h * HEAD_V, (h + 1) * HEAD_V)
            v_h = v_ref[s, :, vs].astype(BF16)
            att = lax.dot_general(qe[:, ks], ke[:, ks], nt_dims, preferred_element_type=F32)
            att = jnp.where(tri, att, 0.0).astype(BF16)
            st = st_ref[s, h]
            o = jnp.dot(att, v_h, preferred_element_type=F32)
            o = o + lax.dot_general(qe[:, ks], st.astype(BF16), nt_dims, preferred_element_type=F32)
            st_ref[s, h] = st * decay[:, ks] + lax.dot_general(v_h, kd[:, ks], tn_dims,
                                                               preferred_element_type=F32)
            on = o * lax.rsqrt(jnp.mean(o * o, axis=-1, keepdims=True) + EPS) * gnw
            g_h = g_ref[s, :, vs]
            o_ref[s, :, vs] = (on * (g_h * jax.nn.sigmoid(g_h))).astype(BF16)

    @pl.when(t == nt - 1)
    def _():
        for s in range(nb):
            for h in range(GLA_HEADS):
                sout_ref[s, h] = st_ref[s, h].T


def _gla_core(qkvg, glr, nseq, tlen, j, s_all, w):
    c = CHUNK
    nt = tlen // c
    nb = GLA_SEQS
    qkvg = qkvg.reshape(nseq, tlen, GLA_MAIN)
    glr = glr.reshape(nseq, tlen, LANES)
    og, s_new = pl.pallas_call(
        functools.partial(_gla_kernel, nt=nt, nb=nb),
        grid=(nseq // nb, nt),
        in_specs=[pl.BlockSpec((nb, c, GLA_DQ), lambda b, t: (b, t, 0)),
                  pl.BlockSpec((nb, c, GLA_DQ), lambda b, t: (b, t, 1)),
                  pl.BlockSpec((nb, c, GLA_DV), lambda b, t: (b, t, 1)),
                  pl.BlockSpec((nb, c, GLA_DV), lambda b, t: (b, t, 2)),
                  pl.BlockSpec((nb, c, LANES), lambda b, t: (b, t, 0)),
                  pl.BlockSpec((None, nb, GLA_HEADS, HEAD_K, HEAD_V), lambda b, t: (j, b, 0, 0, 0),
                               pipeline_mode=pl.Buffered(1)),
                  pl.BlockSpec((None, LANES, GLA_DQ), lambda b, t: (j, 0, 0)),
                  pl.BlockSpec((None, 1, GLA_DQ), lambda b, t: (j, 0, 0)),
                  pl.BlockSpec((None, 1, HEAD_V), lambda b, t: (j, 0, 0))],
        out_specs=[pl.BlockSpec((nb, c, GLA_DV), lambda b, t: (b, t, 0)),
                   pl.BlockSpec((nb, GLA_HEADS, HEAD_K, HEAD_V), lambda b, t: (b, 0, 0, 0))],
        out_shape=[jax.ShapeDtypeStruct((nseq, tlen, GLA_DV), BF16),
                   jax.ShapeDtypeStruct((nseq, GLA_HEADS, HEAD_K, HEAD_V), F32)],
        scratch_shapes=[pltpu.VMEM((nb, GLA_HEADS, HEAD_V, HEAD_K), F32)],
        compiler_params=_params(("parallel", "arbitrary"), 56),
        name="gla_core")(qkvg, qkvg, qkvg, qkvg, glr, s_all, w["gla_w_gk2"], w["gla_b_gk"], w["gla_norm_w"])
    return og.reshape(nseq * tlen, GLA_DV), s_new


def _trunk(x, h_all, conv_all, s_all, w):
    nseq, tlen, d = x.shape
    x = x.reshape(nseq * tlen, d)
    h_all = h_all[:, :, None, :]
    hs, cs, ss = [], [], []
    for layer in range(DEPTH):
        j = layer // 2
        if layer % 2 == 0:
            gu = _norm_matmul(x, w["norm_mix"], layer, w["rg_w_in"], j)
            hg, h_new, c_new = _rglru_core(gu, nseq, tlen, j, h_all, conv_all, w)
            hs.append(h_new[:, 0, :])
            cs.append(c_new)
            x = _matmul_res(hg, w["rg_w_out"], j, x)
        else:
            qkvg, glr = _norm_matmul(x, w["norm_mix"], layer, w["gla_w_main"], j, w["gla_w_glr"])
            og, s_new = _gla_core(qkvg, glr, nseq, tlen, j, s_all, w)
            ss.append(s_new)
            x = _matmul_res(og, w["gla_w_out"], j, x)
        x = _ffn(x, w["norm_ffn"], w["ffn_w_up"], w["ffn_w_down"], layer, w["norm_final"],
                 final=(layer == DEPTH - 1))
    return x.reshape(nseq, tlen, d), jnp.stack(hs), jnp.stack(cs), jnp.stack(ss)


def _prepare_weights(norm_mix, norm_ffn, norm_final, rg_w_in, rg_conv_w, rg_conv_b, rg_w_a, rg_b_a, rg_w_x, rg_b_x,
                     rg_lambda, rg_w_out, gla_w_in, gla_w_gk2, gla_b_gk, gla_norm_w, gla_w_out, ffn_w_up, ffn_w_down):
    glr_pad = LANES - GATE_RANK
    return {
        "norm_mix": norm_mix[:, None, :],
        "norm_ffn": norm_ffn[:, None, :],
        "norm_final": norm_final[None, :],
        "rg_w_in": rg_w_in.astype(BF16),
        "rg_conv_w": rg_conv_w,
        "rg_conv_b": rg_conv_b[:, None, :],
        "rg_w_a": rg_w_a.astype(BF16),
        "rg_b_a": rg_b_a[:, None, :],
        "rg_w_x": rg_w_x.astype(BF16),
        "rg_b_x": rg_b_x[:, None, :],
        "rg_lambda": rg_lambda[:, None, :],
        "rg_w_out": rg_w_out.astype(BF16),
        "gla_w_main": gla_w_in[:, :, :GLA_MAIN].astype(BF16),
        "gla_w_glr": jnp.pad(gla_w_in[:, :, GLA_MAIN:], ((0, 0), (0, 0), (0, glr_pad))).astype(BF16),
        "gla_w_gk2": jnp.pad(gla_w_gk2, ((0, 0), (0, glr_pad), (0, 0))).astype(BF16),
        "gla_b_gk": gla_b_gk[:, None, :],
        "gla_norm_w": gla_norm_w[:, None, :],
        "gla_w_out": gla_w_out.astype(BF16),
        "ffn_w_up": ffn_w_up.astype(BF16),
        "ffn_w_down": ffn_w_down.astype(BF16),
    }


def kernel(x_prompt, x_sample, state_rglru_h, state_rglru_conv, state_gla, norm_mix, norm_ffn, norm_final, rg_w_in, rg_conv_w, rg_conv_b, rg_w_a, rg_b_a, rg_w_x, rg_b_x, rg_lambda, rg_w_out, gla_w_in, gla_w_gk2, gla_b_gk, gla_norm_w, gla_w_out, ffn_w_up, ffn_w_down):
    w = _prepare_weights(norm_mix, norm_ffn, norm_final, rg_w_in, rg_conv_w, rg_conv_b, rg_w_a, rg_b_a, rg_w_x,
                         rg_b_x, rg_lambda, rg_w_out, gla_w_in, gla_w_gk2, gla_b_gk, gla_norm_w, gla_w_out,
                         ffn_w_up, ffn_w_down)
    bp = x_prompt.shape[0]
    n_rg = state_rglru_h.shape[0]
    n_gla = state_gla.shape[0]
    h0 = jnp.zeros((n_rg, bp, D_RNN), F32)
    c0 = jnp.zeros((n_rg, bp, CONV_W - 1, D_RNN), F32)
    s0 = jnp.zeros((n_gla, bp, GLA_HEADS, HEAD_K, HEAD_V), F32)
    y_p, h_p, c_p, s_p = _trunk(x_prompt, h0, c0, s0, w)
    y_s, h_s, c_s, s_s = _trunk(x_sample, state_rglru_h, state_rglru_conv, state_gla, w)
    return (y_p, y_s, h_p, c_p, s_p, h_s, c_s, s_s)
```

```python
import functools

import jax
import jax.numpy as jnp
from jax import lax
from jax.experimental import pallas as pl
from jax.experimental.pallas import tpu as pltpu

F32 = jnp.float32
BF16 = jnp.bfloat16

D_MODEL = 2048
DEPTH = 4
CHUNK = 64
EPS = 1e-6
D_RNN = D_MODEL
RG_BLOCKS = 8
RG_BW = D_RNN // RG_BLOCKS
CONV_W = 4
RG_C = 8.0
GLA_HEADS = 4
HEAD_K = 256
HEAD_V = 512
GATE_RANK = 16
GATE_NORM = 16.0
GLA_DQ = GLA_HEADS * HEAD_K
GLA_DV = GLA_HEADS * HEAD_V
GLA_MAIN = 2 * GLA_DQ + 2 * GLA_DV
D_FF = 5632

LANES = 128
SUBLANES = 8
NORM_ROWS = 16
NORM_UNROLL = 8
MIB = 1024 * 1024
TINY = 1e-30
GELU_C1 = 0.7978845608028654
GELU_C2 = 0.044715 * GELU_C1

MATMUL_TM = 1024
MATMUL_TN = 1024
FFN_TM = 1024
FFN_TF = 512
FFN_CAST_TF = 256
RGLRU_TC = 512
GLA_SEQS = 4


def _params(dims, vmem_mib):
    return pltpu.CompilerParams(dimension_semantics=dims, vmem_limit_bytes=vmem_mib * MIB)


def _rms(x, w):
    ms = jnp.mean(x * x, axis=-1, keepdims=True)
    return x * lax.rsqrt(ms + EPS) * w


def _log_sigmoid(x):
    return jnp.minimum(x, 0.0) - jnp.log(1.0 + jnp.exp(-jnp.abs(x)))


def _split3_bf16(x):
    hi = x.astype(BF16)
    r1 = x - hi.astype(F32)
    mid = r1.astype(BF16)
    lo = (r1 - mid.astype(F32)).astype(BF16)
    return hi, mid, lo


def _sqrt_nonneg(y):
    return y * lax.rsqrt(jnp.maximum(y, TINY))


def _gelu_tanh(x):
    inner = x * (GELU_C1 + GELU_C2 * (x * x))
    return (0.5 * x) * (1.0 + jnp.tanh(inner))


def _norm_rows(src_ref, nw_ref, dst_ref, rows, dtype):
    nw = nw_ref[...]

    def body(r, carry):
        sl = pl.ds(pl.multiple_of(r * NORM_ROWS, NORM_ROWS), NORM_ROWS)
        dst_ref[sl, :] = _rms(src_ref[sl, :], nw).astype(dtype)
        return carry

    lax.fori_loop(0, rows // NORM_ROWS, body, 0, unroll=NORM_UNROLL)


def _norm_matmul_kernel(x_ref, nw_ref, w_ref, o_ref, hn_ref, *, tm):
    @pl.when(pl.program_id(1) == 0)
    def _():
        _norm_rows(x_ref, nw_ref, hn_ref, tm, BF16)

    o_ref[...] = jnp.dot(hn_ref[...], w_ref[...], preferred_element_type=F32)


def _norm_matmul2_kernel(x_ref, nw_ref, w_ref, w2_ref, o_ref, o2_ref, hn_ref, *, tm, side):
    @pl.when(pl.program_id(1) == 0)
    def _():
        _norm_rows(x_ref, nw_ref, hn_ref, tm, BF16)
        lane = lax.broadcasted_iota(jnp.int32, w2_ref.shape, 1)
        w2 = jnp.where(lane < side, w2_ref[...], jnp.zeros_like(w2_ref))
        o2_ref[...] = jnp.dot(hn_ref[...], w2, preferred_element_type=F32)

    o_ref[...] = jnp.dot(hn_ref[...], w_ref[...], preferred_element_type=F32)


def _norm_matmul(x, nws, layer, ws, j, n, side=0):
    m, d = x.shape
    tm = min(m, MATMUL_TM)
    tn = MATMUL_TN
    grid = (m // tm, n // tn)
    x_spec = pl.BlockSpec((tm, d), lambda i, c: (i, 0))
    nw_spec = pl.BlockSpec((None, 1, d), lambda i, c: (layer, 0, 0))
    w_spec = pl.BlockSpec((None, d, tn), lambda i, c: (j, 0, c))
    o_spec = pl.BlockSpec((tm, tn), lambda i, c: (i, c))
    scratch = [pltpu.VMEM((tm, d), BF16)]
    cp = _params(("parallel", "arbitrary"), 48)
    if not side:
        return pl.pallas_call(
            functools.partial(_norm_matmul_kernel, tm=tm),
            grid=grid, in_specs=[x_spec, nw_spec, w_spec], out_specs=o_spec,
            out_shape=jax.ShapeDtypeStruct((m, n), F32), scratch_shapes=scratch,
            compiler_params=cp, name="norm_matmul")(x, nws, ws)
    return pl.pallas_call(
        functools.partial(_norm_matmul2_kernel, tm=tm, side=side),
        grid=grid,
        in_specs=[x_spec, nw_spec, w_spec, pl.BlockSpec((None, d, LANES), lambda i, c: (j, 0, n // LANES))],
        out_specs=[o_spec, pl.BlockSpec((tm, LANES), lambda i, c: (i, 0))],
        out_shape=[jax.ShapeDtypeStruct((m, n), F32), jax.ShapeDtypeStruct((m, LANES), F32)],
        scratch_shapes=scratch, compiler_params=cp, name="norm_matmul2")(x, nws, ws, ws)


def _matmul_res_kernel(a_ref, w_ref, x_ref, o_ref):
    o_ref[...] = x_ref[...] + jnp.dot(a_ref[...], w_ref[...], preferred_element_type=F32)


def _matmul_res(a, ws, j, x):
    m, k = a.shape
    n = ws.shape[2]
    tm = min(m, MATMUL_TM)
    tn = MATMUL_TN
    return pl.pallas_call(
        _matmul_res_kernel,
        grid=(m // tm, n // tn),
        in_specs=[pl.BlockSpec((tm, k), lambda i, c: (i, 0)),
                  pl.BlockSpec((None, k, tn), lambda i, c: (j, 0, c)),
                  pl.BlockSpec((tm, tn), lambda i, c: (i, c))],
        out_specs=pl.BlockSpec((tm, tn), lambda i, c: (i, c)),
        out_shape=jax.ShapeDtypeStruct((m, n), F32),
        compiler_params=_params(("parallel", "arbitrary"), 48), name="matmul_res")(a, ws, x)


def _ffn_step(x_ref, nw_ref, wg, wu, wd, fw_ref, o_ref, hn_ref, *, tm, nf, final):
    f = pl.program_id(1)

    @pl.when(f == 0)
    def _():
        _norm_rows(x_ref, nw_ref, hn_ref, tm, BF16)
        o_ref[...] = x_ref[...]

    hn = hn_ref[...]
    gt = jnp.dot(hn, wg, preferred_element_type=F32)
    up = jnp.dot(hn, wu, preferred_element_type=F32)
    act = ((gt * jax.nn.sigmoid(gt)) * up).astype(BF16)
    o_ref[...] += jnp.dot(act, wd, preferred_element_type=F32)

    if final:
        @pl.when(f == nf - 1)
        def _():
            _norm_rows(o_ref, fw_ref, o_ref, tm, F32)


def _ffn_kernel(x_ref, nw_ref, wg_ref, wu_ref, wd_ref, fw_ref, o_ref, hn_ref, **kw):
    _ffn_step(x_ref, nw_ref, wg_ref[...], wu_ref[...], wd_ref[...], fw_ref, o_ref, hn_ref, **kw)


def _ffn_cast_kernel(x_ref, nw_ref, wg_ref, wu_ref, wd_ref, fw_ref, o_ref, wgo_ref, wuo_ref, wdo_ref, hn_ref, **kw):
    wg = wg_ref[...].astype(BF16)
    wu = wu_ref[...].astype(BF16)
    wd = wd_ref[...].astype(BF16)
    wgo_ref[...] = wg
    wuo_ref[...] = wu
    wdo_ref[...] = wd
    _ffn_step(x_ref, nw_ref, wg, wu, wd, fw_ref, o_ref, hn_ref, **kw)


def _ffn(x, nws, layer, fw, final, w_bf16=None, w_f32=None):
    m, d = x.shape
    tm = min(m, FFN_TM)
    tf = FFN_TF if w_bf16 is not None else FFN_CAST_TF
    nf = D_FF // tf
    kw = dict(tm=tm, nf=nf, final=final)
    x_spec = pl.BlockSpec((tm, d), lambda i, f: (i, 0), pipeline_mode=pl.Buffered(1))
    nw_spec = pl.BlockSpec((None, 1, d), lambda i, f: (layer, 0, 0))
    fw_spec = pl.BlockSpec((1, d), lambda i, f: (0, 0))
    o_spec = pl.BlockSpec((tm, d), lambda i, f: (i, 0))
    o_shape = jax.ShapeDtypeStruct((m, d), F32)
    scratch = [pltpu.VMEM((tm, d), BF16)]
    cp = _params(("parallel", "arbitrary"), 56)
    up_spec = pl.BlockSpec((d, tf), lambda i, f: (0, f))
    down_spec = pl.BlockSpec((tf, d), lambda i, f: (f, 0))
    if w_bf16 is not None:
        return pl.pallas_call(
            functools.partial(_ffn_kernel, **kw), grid=(m // tm, nf),
            in_specs=[x_spec, nw_spec, up_spec, up_spec, down_spec, fw_spec],
            out_specs=o_spec, out_shape=o_shape, scratch_shapes=scratch,
            compiler_params=cp, name="ffn")(x, nws, *w_bf16, fw)
    assert m == tm, "the casting variant rewrites the bf16 weights once per row block"
    w_ups, w_downs = w_f32
    out, wg, wu, wd = pl.pallas_call(
        functools.partial(_ffn_cast_kernel, **kw), grid=(1, nf),
        in_specs=[x_spec, nw_spec,
                  pl.BlockSpec((None, d, tf), lambda i, f: (layer, 0, f)),
                  pl.BlockSpec((None, d, tf), lambda i, f: (layer, 0, f + nf)),
                  pl.BlockSpec((None, tf, d), lambda i, f: (layer, f, 0)),
                  fw_spec],
        out_specs=[o_spec, up_spec, up_spec, down_spec],
        out_shape=[o_shape, jax.ShapeDtypeStruct((d, D_FF), BF16), jax.ShapeDtypeStruct((d, D_FF), BF16),
                   jax.ShapeDtypeStruct((D_FF, d), BF16)],
        scratch_shapes=scratch, compiler_params=cp, name="ffn_cast")(x, nws, w_ups, w_ups, w_downs, fw)
    return out, (wg, wu, wd)


def _rglru_kernel(gate_ref, u_ref, h0_ref, c0_ref, cw_ref, cb_ref, wa_ref, ba_ref, wx_ref, bx_ref, lam_ref,
                  hg_ref, hout_ref, cout_ref, uext_ref, *, tc, nt):
    t = pl.program_id(2)
    pad = SUBLANES

    @pl.when(t == 0)
    def _():
        uext_ref[...] = jnp.zeros((pad, RG_BW), F32)
        uext_ref[pad - (CONV_W - 1):pad, :] = c0_ref[...]
        hout_ref[...] = h0_ref[...]

    u = u_ref[...]
    ext = jnp.concatenate([uext_ref[...], u], axis=0)
    cw = cw_ref[...]
    acc = pltpu.roll(ext, 3, 0)[pad:, :] * cw[0:1, :]
    acc = acc + pltpu.roll(ext, 2, 0)[pad:, :] * cw[1:2, :]
    acc = acc + pltpu.roll(ext, 1, 0)[pad:, :] * cw[2:3, :]
    acc = acc + u * cw[3:4, :]
    uc = cb_ref[...] + acc
    ucb = uc.astype(BF16)
    r = jax.nn.sigmoid(jnp.dot(ucb, wa_ref[...], preferred_element_type=F32) + ba_ref[...])
    i = jax.nn.sigmoid(jnp.dot(ucb, wx_ref[...], preferred_element_type=F32) + bx_ref[...])
    log_a = (RG_C * r) * _log_sigmoid(lam_ref[...])
    a = jnp.exp(log_a)
    b = _sqrt_nonneg(1.0 - a * a) * (i * uc)

    groups = tc // SUBLANES
    a3 = a.reshape(groups, SUBLANES, RG_BW)
    b3 = b.reshape(groups, SUBLANES, RG_BW)
    row = lax.broadcasted_iota(jnp.int32, (groups, SUBLANES, RG_BW), 1)
    for s in (1, 2, 4):
        keep = row >= s
        a_prev = pltpu.roll(a3, s, 1)
        b_prev = pltpu.roll(b3, s, 1)
        b3 = jnp.where(keep, b3 + a3 * b_prev, b3)
        a3 = jnp.where(keep, a3 * a_prev, a3)
    h_prev = jnp.broadcast_to(hout_ref[...], (SUBLANES, RG_BW))
    hs = []
    for g in range(groups):
        h_g = a3[g] * h_prev + b3[g]
        hs.append(h_g)
        h_prev = jnp.broadcast_to(h_g[SUBLANES - 1:SUBLANES, :], (SUBLANES, RG_BW))
    h = jnp.concatenate(hs, axis=0)
    hout_ref[...] = hs[-1][SUBLANES - 1:SUBLANES, :]
    hg_ref[...] = (h * _gelu_tanh(gate_ref[...])).astype(BF16)

    uext_ref[...] = u[tc - pad:, :]

    @pl.when(t == nt - 1)
    def _():
        cout_ref[...] = uext_ref[pad - (CONV_W - 1):pad, :]


def _rglru_core(gu, nseq, tlen, j, h_all, c_all, w):
    m = nseq * tlen
    tc = min(tlen, RGLRU_TC)
    nt = tlen // tc
    bw = RG_BW
    row = lambda b, n, t: b * nt + t
    vec = pl.BlockSpec((None, 1, bw), lambda b, n, t: (j, 0, n))
    gate_spec = pl.BlockSpec((tc, bw), lambda b, n, t: (row(b, n, t), n))
    u_spec = pl.BlockSpec((tc, bw), lambda b, n, t: (row(b, n, t), RG_BLOCKS + n))
    hin_spec = pl.BlockSpec((None, None, 1, bw), lambda b, n, t: (j, b, 0, n))
    cin_spec = pl.BlockSpec((None, None, CONV_W - 1, bw), lambda b, n, t: (j, b, 0, n))
    hout_spec = pl.BlockSpec((None, 1, bw), lambda b, n, t: (b, 0, n))
    cout_spec = pl.BlockSpec((None, CONV_W - 1, bw), lambda b, n, t: (b, 0, n))
    w_spec = pl.BlockSpec((None, None, bw, bw), lambda b, n, t: (j, n, 0, 0))
    return pl.pallas_call(
        functools.partial(_rglru_kernel, tc=tc, nt=nt),
        grid=(nseq, RG_BLOCKS, nt),
        in_specs=[gate_spec, u_spec, hin_spec, cin_spec,
                  pl.BlockSpec((None, CONV_W, bw), lambda b, n, t: (j, 0, n)), vec,
                  w_spec, vec, w_spec, vec, vec],
        out_specs=[pl.BlockSpec((tc, bw), lambda b, n, t: (row(b, n, t), n)), hout_spec, cout_spec],
        out_shape=[jax.ShapeDtypeStruct((m, D_RNN), BF16),
                   jax.ShapeDtypeStruct((nseq, 1, D_RNN), F32),
                   jax.ShapeDtypeStruct((nseq, CONV_W - 1, D_RNN), F32)],
        scratch_shapes=[pltpu.VMEM((SUBLANES, bw), F32)],
        compiler_params=_params(("parallel", "parallel", "arbitrary"), 32),
        name="rglru_core")(gu, gu, h_all, c_all, w["rg_conv_w"], w["rg_conv_b"], w["rg_w_a"], w["rg_b_a"],
                           w["rg_w_x"], w["rg_b_x"], w["rg_lambda"])


def _gla_kernel(q_ref, k_ref, v_ref, g_ref, glr_ref, s0_ref, w2_ref, bgk_ref, gnw_ref,
                o_ref, sout_ref, st_ref, *, nt, nb):
    t = pl.program_id(1)

    @pl.when(t == 0)
    def _():
        for s in range(nb):
            for h in range(GLA_HEADS):
                st_ref[s, h] = s0_ref[s, h].T

    c = CHUNK
    glr = glr_ref[...].reshape(nb * c, LANES).astype(BF16)
    gk_all = jnp.dot(glr, w2_ref[...], preferred_element_type=F32) + bgk_ref[...]
    gk_all = _log_sigmoid(gk_all) * (1.0 / GATE_NORM)
    rows = lax.broadcasted_iota(jnp.int32, (c, c), 0)
    cols = lax.broadcasted_iota(jnp.int32, (c, c), 1)
    tri = rows >= cols
    tri_b = tri.astype(BF16)
    gnw = gnw_ref[...]
    nt_dims = (((1,), (1,)), ((), ()))
    tn_dims = (((0,), (0,)), ((), ()))
    for s in range(nb):
        gk = gk_all[s * c:(s + 1) * c, :]
        bcum = sum(jnp.dot(tri_b, piece, preferred_element_type=F32) for piece in _split3_bf16(gk))
        blast = bcum[c - 1:c, :]
        k = k_ref[s]
        qe = ((q_ref[s] * (HEAD_K ** -0.5)) * jnp.exp(bcum)).astype(BF16)
        ke = (k * jnp.exp(-bcum)).astype(BF16)
        kd = (k * jnp.exp(blast - bcum)).astype(BF16)
        decay = jnp.exp(blast)
        for h in range(GLA_HEADS):
            ks = slice(h * HEAD_K, (h + 1) * HEAD_K)
            vs = slice(h * HEAD_V, (h + 1) * HEAD_V)
            v_h = v_ref[s, :, vs].astype(BF16)
            att = lax.dot_general(qe[:, ks], ke[:, ks], nt_dims, preferred_element_type=F32)
            att = jnp.where(tri, att, 0.0).astype(BF16)
            st = st_ref[s, h]
            o = jnp.dot(att, v_h, preferred_element_type=F32)
            o = o + lax.dot_general(qe[:, ks], st.astype(BF16), nt_dims, preferred_element_type=F32)
            st_ref[s, h] = st * decay[:, ks] + lax.dot_general(v_h, kd[:, ks], tn_dims,
                                                               preferred_element_type=F32)
            on = o * lax.rsqrt(jnp.mean(o * o, axis=-1, keepdims=True) + EPS) * gnw
            g_h = g_ref[s, :, vs]
            o_ref[s, :, vs] = (on * (g_h * jax.nn.sigmoid(g_h))).astype(BF16)

    @pl.when(t == nt - 1)
    def _():
        for s in range(nb):
            for h in range(GLA_HEADS):
                sout_ref[s, h] = st_ref[s, h].T


def _gla_core(qkvg, glr, nseq, tlen, j, s_all, w):
    c = CHUNK
    nt = tlen // c
    nb = GLA_SEQS
    qkvg = qkvg.reshape(nseq, tlen, GLA_MAIN)
    glr = glr.reshape(nseq, tlen, LANES)
    og, s_new = pl.pallas_call(
        functools.partial(_gla_kernel, nt=nt, nb=nb),
        grid=(nseq // nb, nt),
        in_specs=[pl.BlockSpec((nb, c, GLA_DQ), lambda b, t: (b, t, 0)),
                  pl.BlockSpec((nb, c, GLA_DQ), lambda b, t: (b, t, 1)),
                  pl.BlockSpec((nb, c, GLA_DV), lambda b, t: (b, t, 1)),
                  pl.BlockSpec((nb, c, GLA_DV), lambda b, t: (b, t, 2)),
                  pl.BlockSpec((nb, c, LANES), lambda b, t: (b, t, 0)),
                  pl.BlockSpec((None, nb, GLA_HEADS, HEAD_K, HEAD_V), lambda b, t: (j, b, 0, 0, 0),
                               pipeline_mode=pl.Buffered(1)),
                  pl.BlockSpec((None, LANES, GLA_DQ), lambda b, t: (j, 0, 0)),
                  pl.BlockSpec((None, 1, GLA_DQ), lambda b, t: (j, 0, 0)),
                  pl.BlockSpec((None, 1, HEAD_V), lambda b, t: (j, 0, 0))],
        out_specs=[pl.BlockSpec((nb, c, GLA_DV), lambda b, t: (b, t, 0)),
                   pl.BlockSpec((nb, GLA_HEADS, HEAD_K, HEAD_V), lambda b, t: (b, 0, 0, 0))],
        out_shape=[jax.ShapeDtypeStruct((nseq, tlen, GLA_DV), BF16),
                   jax.ShapeDtypeStruct((nseq, GLA_HEADS, HEAD_K, HEAD_V), F32)],
        scratch_shapes=[pltpu.VMEM((nb, GLA_HEADS, HEAD_V, HEAD_K), F32)],
        compiler_params=_params(("parallel", "arbitrary"), 56),
        name="gla_core")(qkvg, qkvg, qkvg, qkvg, glr, s_all, w["gla_w_gk2"], w["gla_b_gk"], w["gla_norm_w"])
    return og.reshape(nseq * tlen, GLA_DV), s_new


def _trunk(x, h_all, conv_all, s_all, w, ffn_bf16=None):
    nseq, tlen, d = x.shape
    x = x.reshape(nseq * tlen, d)
    h_all = h_all[:, :, None, :]
    hs, cs, ss, ffn_out = [], [], [], []
    for layer in range(DEPTH):
        j = layer // 2
        if layer % 2 == 0:
            gu = _norm_matmul(x, w["norm_mix"], layer, w["rg_w_in"], j, 2 * D_RNN)
            hg, h_new, c_new = _rglru_core(gu, nseq, tlen, j, h_all, conv_all, w)
            hs.append(h_new[:, 0, :])
            cs.append(c_new)
            x = _matmul_res(hg, w["rg_w_out"], j, x)
        else:
            qkvg, glr = _norm_matmul(x, w["norm_mix"], layer, w["gla_w_in"], j, GLA_MAIN, side=GATE_RANK)
            og, s_new = _gla_core(qkvg, glr, nseq, tlen, j, s_all, w)
            ss.append(s_new)
            x = _matmul_res(og, w["gla_w_out"], j, x)
        final = layer == DEPTH - 1
        if ffn_bf16 is None:
            x, w_layer = _ffn(x, w["norm_ffn"], layer, w["norm_final"], final,
                              w_f32=(w["ffn_w_up"], w["ffn_w_down"]))
            ffn_out.append(w_layer)
        else:
            x = _ffn(x, w["norm_ffn"], layer, w["norm_final"], final, w_bf16=ffn_bf16[layer])
    return (x.reshape(nseq, tlen, d), jnp.stack(hs), jnp.stack(cs), jnp.stack(ss)), ffn_out


def _prepare_weights(norm_mix, norm_ffn, norm_final, rg_w_in, rg_conv_w, rg_conv_b, rg_w_a, rg_b_a, rg_w_x, rg_b_x,
                     rg_lambda, rg_w_out, gla_w_in, gla_w_gk2, gla_b_gk, gla_norm_w, gla_w_out, ffn_w_up, ffn_w_down):
    glr_pad = LANES - GATE_RANK
    return {
        "norm_mix": norm_mix[:, None, :],
        "norm_ffn": norm_ffn[:, None, :],
        "norm_final": norm_final[None, :],
        "rg_w_in": rg_w_in.astype(BF16),
        "rg_conv_w": rg_conv_w,
        "rg_conv_b": rg_conv_b[:, None, :],
        "rg_w_a": rg_w_a.astype(BF16),
        "rg_b_a": rg_b_a[:, None, :],
        "rg_w_x": rg_w_x.astype(BF16),
        "rg_b_x": rg_b_x[:, None, :],
        "rg_lambda": rg_lambda[:, None, :],
        "rg_w_out": rg_w_out.astype(BF16),
        "gla_w_in": gla_w_in.astype(BF16),
        "gla_w_gk2": jnp.pad(gla_w_gk2, ((0, 0), (0, glr_pad), (0, 0))).astype(BF16),
        "gla_b_gk": gla_b_gk[:, None, :],
        "gla_norm_w": gla_norm_w[:, None, :],
        "gla_w_out": gla_w_out.astype(BF16),
        "ffn_w_up": ffn_w_up,
        "ffn_w_down": ffn_w_down,
    }


def kernel(x_prompt, x_sample, state_rglru_h, state_rglru_conv, state_gla, norm_mix, norm_ffn, norm_final, rg_w_in, rg_conv_w, rg_conv_b, rg_w_a, rg_b_a, rg_w_x, rg_b_x, rg_lambda, rg_w_out, gla_w_in, gla_w_gk2, gla_b_gk, gla_norm_w, gla_w_out, ffn_w_up, ffn_w_down):
    w = _prepare_weights(norm_mix, norm_ffn, norm_final, rg_w_in, rg_conv_w, rg_conv_b, rg_w_a, rg_b_a, rg_w_x,
                         rg_b_x, rg_lambda, rg_w_out, gla_w_in, gla_w_gk2, gla_b_gk, gla_norm_w, gla_w_out,
                         ffn_w_up, ffn_w_down)
    bp = x_prompt.shape[0]
    n_rg = state_rglru_h.shape[0]
    n_gla = state_gla.shape[0]
    h0 = jnp.zeros((n_rg, bp, D_RNN), F32)
    c0 = jnp.zeros((n_rg, bp, CONV_W - 1, D_RNN), F32)
    s0 = jnp.zeros((n_gla, bp, GLA_HEADS, HEAD_K, HEAD_V), F32)
    (y_s, h_s, c_s, s_s), ffn_bf16 = _trunk(x_sample, state_rglru_h, state_rglru_conv, state_gla, w)
    (y_p, h_p, c_p, s_p), _ = _trunk(x_prompt, h0, c0, s0, w, ffn_bf16)
    return (y_p, y_s, h_p, c_p, s_p, h_s, c_s, s_s)
```

```python
import functools

import jax
import jax.numpy as jnp
from jax import lax
from jax.experimental import pallas as pl
from jax.experimental.pallas import tpu as pltpu

F32 = jnp.float32
BF16 = jnp.bfloat16

D_MODEL = 2048
DEPTH = 4
CHUNK = 64
EPS = 1e-6
D_RNN = D_MODEL
RG_BLOCKS = 8
RG_BW = D_RNN // RG_BLOCKS
CONV_W = 4
RG_C = 8.0
GLA_HEADS = 4
HEAD_K = 256
HEAD_V = 512
GATE_RANK = 16
GATE_NORM = 16.0
GLA_DQ = GLA_HEADS * HEAD_K
GLA_DV = GLA_HEADS * HEAD_V
GLA_MAIN = 2 * GLA_DQ + 2 * GLA_DV
D_FF = 5632

LANES = 128
SUBLANES = 8
NORM_ROWS = 16
NORM_UNROLL = 8
MIB = 1024 * 1024
TINY = 1e-30
GELU_C1 = 0.7978845608028654
GELU_C2 = 0.044715 * GELU_C1

MATMUL_TM = 1024
MATMUL_TN = 1024
FFN_TM = 1024
FFN_TF = 512
FFN_CAST_TF = 256
RGLRU_TC = 512
GLA_SEQS = 4


def _params(dims, vmem_mib):
    return pltpu.CompilerParams(dimension_semantics=dims, vmem_limit_bytes=vmem_mib * MIB)


def _rms(x, w):
    ms = jnp.mean(x * x, axis=-1, keepdims=True)
    return x * lax.rsqrt(ms + EPS) * w


def _log_sigmoid(x):
    return jnp.minimum(x, 0.0) - jnp.log(1.0 + jnp.exp(-jnp.abs(x)))


def _split3_bf16(x):
    hi = x.astype(BF16)
    r1 = x - hi.astype(F32)
    mid = r1.astype(BF16)
    lo = (r1 - mid.astype(F32)).astype(BF16)
    return hi, mid, lo


def _sqrt_nonneg(y):
    return y * lax.rsqrt(jnp.maximum(y, TINY))


def _gelu_tanh(x):
    inner = x * (GELU_C1 + GELU_C2 * (x * x))
    return (0.5 * x) * (1.0 + jnp.tanh(inner))


def _norm_rows(src_ref, nw_ref, dst_ref, rows, dtype):
    nw = nw_ref[...]

    def body(r, carry):
        sl = pl.ds(pl.multiple_of(r * NORM_ROWS, NORM_ROWS), NORM_ROWS)
        dst_ref[sl, :] = _rms(src_ref[sl, :], nw).astype(dtype)
        return carry

    lax.fori_loop(0, rows // NORM_ROWS, body, 0, unroll=NORM_UNROLL)


def _norm_matmul_kernel(x_ref, nw_ref, w_ref, o_ref, hn_ref, *, tm):
    @pl.when(pl.program_id(1) == 0)
    def _():
        _norm_rows(x_ref, nw_ref, hn_ref, tm, BF16)

    o_ref[...] = jnp.dot(hn_ref[...], w_ref[...], preferred_element_type=F32)


def _norm_matmul2_kernel(x_ref, nw_ref, w_ref, w2_ref, o_ref, o2_ref, hn_ref, *, tm, side):
    @pl.when(pl.program_id(1) == 0)
    def _():
        _norm_rows(x_ref, nw_ref, hn_ref, tm, BF16)
        lane = lax.broadcasted_iota(jnp.int32, w2_ref.shape, 1)
        w2 = jnp.where(lane < side, w2_ref[...], jnp.zeros_like(w2_ref))
        o2_ref[...] = jnp.dot(hn_ref[...], w2, preferred_element_type=F32)

    o_ref[...] = jnp.dot(hn_ref[...], w_ref[...], preferred_element_type=F32)


def _norm_matmul(x, nws, layer, ws, j, n, side=0):
    m, d = x.shape
    tm = min(m, MATMUL_TM)
    tn = MATMUL_TN
    grid = (m // tm, n // tn)
    x_spec = pl.BlockSpec((tm, d), lambda i, c: (i, 0))
    nw_spec = pl.BlockSpec((None, 1, d), lambda i, c: (layer, 0, 0))
    w_spec = pl.BlockSpec((None, d, tn), lambda i, c: (j, 0, c))
    o_spec = pl.BlockSpec((tm, tn), lambda i, c: (i, c))
    scratch = [pltpu.VMEM((tm, d), BF16)]
    cp = _params(("parallel", "arbitrary"), 48)
    if not side:
        return pl.pallas_call(
            functools.partial(_norm_matmul_kernel, tm=tm),
            grid=grid, in_specs=[x_spec, nw_spec, w_spec], out_specs=o_spec,
            out_shape=jax.ShapeDtypeStruct((m, n), F32), scratch_shapes=scratch,
            compiler_params=cp, name="norm_matmul")(x, nws, ws)
    return pl.pallas_call(
        functools.partial(_norm_matmul2_kernel, tm=tm, side=side),
        grid=grid,
        in_specs=[x_spec, nw_spec, w_spec, pl.BlockSpec((None, d, LANES), lambda i, c: (j, 0, n // LANES))],
        out_specs=[o_spec, pl.BlockSpec((tm, LANES), lambda i, c: (i, 0))],
        out_shape=[jax.ShapeDtypeStruct((m, n), F32), jax.ShapeDtypeStruct((m, LANES), F32)],
        scratch_shapes=scratch, compiler_params=cp, name="norm_matmul2")(x, nws, ws, ws)


def _matmul_res_kernel(a_ref, w_ref, x_ref, o_ref):
    o_ref[...] = x_ref[...] + jnp.dot(a_ref[...], w_ref[...], preferred_element_type=F32)


def _matmul_res(a, ws, j, x):
    m, k = a.shape
    n = ws.shape[2]
    tm = min(m, MATMUL_TM)
    tn = MATMUL_TN
    return pl.pallas_call(
        _matmul_res_kernel,
        grid=(m // tm, n // tn),
        in_specs=[pl.BlockSpec((tm, k), lambda i, c: (i, 0)),
                  pl.BlockSpec((None, k, tn), lambda i, c: (j, 0, c)),
                  pl.BlockSpec((tm, tn), lambda i, c: (i, c))],
        out_specs=pl.BlockSpec((tm, tn), lambda i, c: (i, c)),
        out_shape=jax.ShapeDtypeStruct((m, n), F32),
        compiler_params=_params(("parallel", "arbitrary"), 48), name="matmul_res")(a, ws, x)


def _ffn_kernel(x_ref, nw_ref, wg_ref, wu_ref, wd_ref, fw_ref, o_ref, hn_ref, *, tm, nf, final):
    f = pl.program_id(1)

    @pl.when(f == 0)
    def _():
        _norm_rows(x_ref, nw_ref, hn_ref, tm, BF16)
        o_ref[...] = x_ref[...]

    hn = hn_ref[...]
    gt = jnp.dot(hn, wg_ref[...], preferred_element_type=F32)
    up = jnp.dot(hn, wu_ref[...], preferred_element_type=F32)
    act = ((gt * jax.nn.sigmoid(gt)) * up).astype(BF16)
    o_ref[...] += jnp.dot(act, wd_ref[...], preferred_element_type=F32)

    if final:
        @pl.when(f == nf - 1)
        def _():
            _norm_rows(o_ref, fw_ref, o_ref, tm, F32)


def _ffn_cast_kernel(x_ref, nw_ref, wg_ref, wu_ref, wd_ref, fw_ref, o_ref, wgo_ref, wuo_ref, wdo_ref, hn_ref, **kw):
    wgo_ref[...] = wg_ref[...].astype(BF16)
    wuo_ref[...] = wu_ref[...].astype(BF16)
    wdo_ref[...] = wd_ref[...].astype(BF16)
    _ffn_kernel(x_ref, nw_ref, wgo_ref, wuo_ref, wdo_ref, fw_ref, o_ref, hn_ref, **kw)


def _ffn(x, nws, layer, fw, final, w_bf16=None, w_f32=None):
    m, d = x.shape
    tm = min(m, FFN_TM)
    tf = FFN_TF if w_bf16 is not None else FFN_CAST_TF
    nf = D_FF // tf
    kw = dict(tm=tm, nf=nf, final=final)
    x_spec = pl.BlockSpec((tm, d), lambda i, f: (i, 0), pipeline_mode=pl.Buffered(1))
    nw_spec = pl.BlockSpec((None, 1, d), lambda i, f: (layer, 0, 0))
    fw_spec = pl.BlockSpec((1, d), lambda i, f: (0, 0))
    o_spec = pl.BlockSpec((tm, d), lambda i, f: (i, 0))
    o_shape = jax.ShapeDtypeStruct((m, d), F32)
    scratch = [pltpu.VMEM((tm, d), BF16)]
    cp = _params(("parallel", "arbitrary"), 56)
    up_spec = pl.BlockSpec((d, tf), lambda i, f: (0, f))
    down_spec = pl.BlockSpec((tf, d), lambda i, f: (f, 0))
    if w_bf16 is not None:
        return pl.pallas_call(
            functools.partial(_ffn_kernel, **kw), grid=(m // tm, nf),
            in_specs=[x_spec, nw_spec, up_spec, up_spec, down_spec, fw_spec],
            out_specs=o_spec, out_shape=o_shape, scratch_shapes=scratch,
            compiler_params=cp, name="ffn")(x, nws, *w_bf16, fw)
    assert m == tm, "the casting variant rewrites the bf16 weights once per row block"
    w_ups, w_downs = w_f32
    out, wg, wu, wd = pl.pallas_call(
        functools.partial(_ffn_cast_kernel, **kw), grid=(1, nf),
        in_specs=[x_spec, nw_spec,
                  pl.BlockSpec((None, d, tf), lambda i, f: (layer, 0, f)),
                  pl.BlockSpec((None, d, tf), lambda i, f: (layer, 0, f + nf)),
                  pl.BlockSpec((None, tf, d), lambda i, f: (layer, f, 0)),
                  fw_spec],
        out_specs=[o_spec, up_spec, up_spec, down_spec],
        out_shape=[o_shape, jax.ShapeDtypeStruct((d, D_FF), BF16), jax.ShapeDtypeStruct((d, D_FF), BF16),
                   jax.ShapeDtypeStruct((D_FF, d), BF16)],
        scratch_shapes=scratch, compiler_params=cp, name="ffn_cast")(x, nws, w_ups, w_ups, w_downs, fw)
    return out, (wg, wu, wd)


def _rglru_kernel(gate_ref, u_ref, h0_ref, c0_ref, cw_ref, cb_ref, wa_ref, ba_ref, wx_ref, bx_ref, lam_ref,
                  hg_ref, hout_ref, cout_ref, uext_ref, *, tc, nt):
    t = pl.program_id(2)
    pad = SUBLANES

    @pl.when(t == 0)
    def _():
        uext_ref[...] = jnp.zeros((pad, RG_BW), F32)
        uext_ref[pad - (CONV_W - 1):pad, :] = c0_ref[...]
        hout_ref[...] = h0_ref[...]

    u = u_ref[...]
    ext = jnp.concatenate([uext_ref[...], u], axis=0)
    cw = cw_ref[...]
    acc = pltpu.roll(ext, 3, 0)[pad:, :] * cw[0:1, :]
    acc = acc + pltpu.roll(ext, 2, 0)[pad:, :] * cw[1:2, :]
    acc = acc + pltpu.roll(ext, 1, 0)[pad:, :] * cw[2:3, :]
    acc = acc + u * cw[3:4, :]
    uc = cb_ref[...] + acc
    ucb = uc.astype(BF16)
    r = jax.nn.sigmoid(jnp.dot(ucb, wa_ref[...], preferred_element_type=F32) + ba_ref[...])
    i = jax.nn.sigmoid(jnp.dot(ucb, wx_ref[...], preferred_element_type=F32) + bx_ref[...])
    log_a = (RG_C * r) * _log_sigmoid(lam_ref[...])
    a = jnp.exp(log_a)
    b = _sqrt_nonneg(1.0 - a * a) * (i * uc)

    groups = tc // SUBLANES
    a3 = a.reshape(groups, SUBLANES, RG_BW)
    b3 = b.reshape(groups, SUBLANES, RG_BW)
    row = lax.broadcasted_iota(jnp.int32, (groups, SUBLANES, RG_BW), 1)
    for s in (1, 2, 4):
        keep = row >= s
        a_prev = pltpu.roll(a3, s, 1)
        b_prev = pltpu.roll(b3, s, 1)
        b3 = jnp.where(keep, b3 + a3 * b_prev, b3)
        a3 = jnp.where(keep, a3 * a_prev, a3)
    h_prev = jnp.broadcast_to(hout_ref[...], (SUBLANES, RG_BW))
    hs = []
    for g in range(groups):
        h_g = a3[g] * h_prev + b3[g]
        hs.append(h_g)
        h_prev = jnp.broadcast_to(h_g[SUBLANES - 1:SUBLANES, :], (SUBLANES, RG_BW))
    h = jnp.concatenate(hs, axis=0)
    hout_ref[...] = hs[-1][SUBLANES - 1:SUBLANES, :]
    hg_ref[...] = (h * _gelu_tanh(gate_ref[...])).astype(BF16)

    uext_ref[...] = u[tc - pad:, :]

    @pl.when(t == nt - 1)
    def _():
        cout_ref[...] = uext_ref[pad - (CONV_W - 1):pad, :]


def _rglru_core(gu, nseq, tlen, j, h_all, c_all, w):
    m = nseq * tlen
    tc = min(tlen, RGLRU_TC)
    nt = tlen // tc
    bw = RG_BW
    row = lambda b, n, t: b * nt + t
    vec = pl.BlockSpec((None, 1, bw), lambda b, n, t: (j, 0, n))
    gate_spec = pl.BlockSpec((tc, bw), lambda b, n, t: (row(b, n, t), n))
    u_spec = pl.BlockSpec((tc, bw), lambda b, n, t: (row(b, n, t), RG_BLOCKS + n))
    hin_spec = pl.BlockSpec((None, None, 1, bw), lambda b, n, t: (j, b, 0, n))
    cin_spec = pl.BlockSpec((None, None, CONV_W - 1, bw), lambda b, n, t: (j, b, 0, n))
    hout_spec = pl.BlockSpec((None, 1, bw), lambda b, n, t: (b, 0, n))
    cout_spec = pl.BlockSpec((None, CONV_W - 1, bw), lambda b, n, t: (b, 0, n))
    w_spec = pl.BlockSpec((None, None, bw, bw), lambda b, n, t: (j, n, 0, 0))
    return pl.pallas_call(
        functools.partial(_rglru_kernel, tc=tc, nt=nt),
        grid=(nseq, RG_BLOCKS, nt),
        in_specs=[gate_spec, u_spec, hin_spec, cin_spec,
                  pl.BlockSpec((None, CONV_W, bw), lambda b, n, t: (j, 0, n)), vec,
                  w_spec, vec, w_spec, vec, vec],
        out_specs=[pl.BlockSpec((tc, bw), lambda b, n, t: (row(b, n, t), n)), hout_spec, cout_spec],
        out_shape=[jax.ShapeDtypeStruct((m, D_RNN), BF16),
                   jax.ShapeDtypeStruct((nseq, 1, D_RNN), F32),
                   jax.ShapeDtypeStruct((nseq, CONV_W - 1, D_RNN), F32)],
        scratch_shapes=[pltpu.VMEM((SUBLANES, bw), F32)],
        compiler_params=_params(("parallel", "parallel", "arbitrary"), 32),
        name="rglru_core")(gu, gu, h_all, c_all, w["rg_conv_w"], w["rg_conv_b"], w["rg_w_a"], w["rg_b_a"],
                           w["rg_w_x"], w["rg_b_x"], w["rg_lambda"])


def _gla_kernel(q_ref, k_ref, v_ref, g_ref, glr_ref, s0_ref, w2_ref, bgk_ref, gnw_ref,
                o_ref, sout_ref, st_ref, *, nt, nb):
    t = pl.program_id(1)

    @pl.when(t == 0)
    def _():
        for s in range(nb):
            for h in range(GLA_HEADS):
                st_ref[s, h] = s0_ref[s, h].T

    c = CHUNK
    glr = glr_ref[...].reshape(nb * c, LANES).astype(BF16)
    gk_all = jnp.dot(glr, w2_ref[...], preferred_element_type=F32) + bgk_ref[...]
    gk_all = _log_sigmoid(gk_all) * (1.0 / GATE_NORM)
    rows = lax.broadcasted_iota(jnp.int32, (c, c), 0)
    cols = lax.broadcasted_iota(jnp.int32, (c, c), 1)
    tri = rows >= cols
    tri_b = tri.astype(BF16)
    gnw = gnw_ref[...]
    nt_dims = (((1,), (1,)), ((), ()))
    tn_dims = (((0,), (0,)), ((), ()))
    for s in range(nb):
        gk = gk_all[s * c:(s + 1) * c, :]
        bcum = sum(jnp.dot(tri_b, piece, preferred_element_type=F32) for piece in _split3_bf16(gk))
        blast = bcum[c - 1:c, :]
        k = k_ref[s]
        qe = ((q_ref[s] * (HEAD_K ** -0.5)) * jnp.exp(bcum)).astype(BF16)
        ke = (k * jnp.exp(-bcum)).astype(BF16)
        kd = (k * jnp.exp(blast - bcum)).astype(BF16)
        decay = jnp.exp(blast)
        for h in range(GLA_HEADS):
            ks = slice(h * HEAD_K, (h + 1) * HEAD_K)
            vs = slice(h * HEAD_V, (h + 1) * HEAD_V)
            v_h = v_ref[s, :, vs].astype(BF16)
            att = lax.dot_general(qe[:, ks], ke[:, ks], nt_dims, preferred_element_type=F32)
            att = jnp.where(tri, att, 0.0).astype(BF16)
            st = st_ref[s, h]
            o = jnp.dot(att, v_h, preferred_element_type=F32)
            o = o + lax.dot_general(qe[:, ks], st.astype(BF16), nt_dims, preferred_element_type=F32)
            st_ref[s, h] = st * decay[:, ks] + lax.dot_general(v_h, kd[:, ks], tn_dims,
                                                               preferred_element_type=F32)
            on = o * lax.rsqrt(jnp.mean(o * o, axis=-1, keepdims=True) + EPS) * gnw
            g_h = g_ref[s, :, vs]
            o_ref[s, :, vs] = (on * (g_h * jax.nn.sigmoid(g_h))).astype(BF16)

    @pl.when(t == nt - 1)
    def _():
        for s in range(nb):
            for h in range(GLA_HEADS):
                sout_ref[s, h] = st_ref[s, h].T


def _gla_core(qkvg, glr, nseq, tlen, j, s_all, w):
    c = CHUNK
    nt = tlen // c
    nb = GLA_SEQS
    qkvg = qkvg.reshape(nseq, tlen, GLA_MAIN)
    glr = glr.reshape(nseq, tlen, LANES)
    og, s_new = pl.pallas_call(
        functools.partial(_gla_kernel, nt=nt, nb=nb),
        grid=(nseq // nb, nt),
        in_specs=[pl.BlockSpec((nb, c, GLA_DQ), lambda b, t: (b, t, 0)),
                  pl.BlockSpec((nb, c, GLA_DQ), lambda b, t: (b, t, 1)),
                  pl.BlockSpec((nb, c, GLA_DV), lambda b, t: (b, t, 1)),
                  pl.BlockSpec((nb, c, GLA_DV), lambda b, t: (b, t, 2)),
                  pl.BlockSpec((nb, c, LANES), lambda b, t: (b, t, 0)),
                  pl.BlockSpec((None, nb, GLA_HEADS, HEAD_K, HEAD_V), lambda b, t: (j, b, 0, 0, 0),
                               pipeline_mode=pl.Buffered(1)),
                  pl.BlockSpec((None, LANES, GLA_DQ), lambda b, t: (j, 0, 0)),
                  pl.BlockSpec((None, 1, GLA_DQ), lambda b, t: (j, 0, 0)),
                  pl.BlockSpec((None, 1, HEAD_V), lambda b, t: (j, 0, 0))],
        out_specs=[pl.BlockSpec((nb, c, GLA_DV), lambda b, t: (b, t, 0)),
                   pl.BlockSpec((nb, GLA_HEADS, HEAD_K, HEAD_V), lambda b, t: (b, 0, 0, 0))],
        out_shape=[jax.ShapeDtypeStruct((nseq, tlen, GLA_DV), BF16),
                   jax.ShapeDtypeStruct((nseq, GLA_HEADS, HEAD_K, HEAD_V), F32)],
        scratch_shapes=[pltpu.VMEM((nb, GLA_HEADS, HEAD_V, HEAD_K), F32)],
        compiler_params=_params(("parallel", "arbitrary"), 56),
        name="gla_core")(qkvg, qkvg, qkvg, qkvg, glr, s_all, w["gla_w_gk2"], w["gla_b_gk"], w["gla_norm_w"])
    return og.reshape(nseq * tlen, GLA_DV), s_new


def _trunk(x, h_all, conv_all, s_all, w, ffn_bf16=None):
    nseq, tlen, d = x.shape
    x = x.reshape(nseq * tlen, d)
    h_all = h_all[:, :, None, :]
    hs, cs, ss, ffn_out = [], [], [], []
    for layer in range(DEPTH):
        j = layer // 2
        if layer % 2 == 0:
            gu = _norm_matmul(x, w["norm_mix"], layer, w["rg_w_in"], j, 2 * D_RNN)
            hg, h_new, c_new = _rglru_core(gu, nseq, tlen, j, h_all, conv_all, w)
            hs.append(h_new[:, 0, :])
            cs.append(c_new)
            x = _matmul_res(hg, w["rg_w_out"], j, x)
        else:
            qkvg, glr = _norm_matmul(x, w["norm_mix"], layer, w["gla_w_in"], j, GLA_MAIN, side=GATE_RANK)
            og, s_new = _gla_core(qkvg, glr, nseq, tlen, j, s_all, w)
            ss.append(s_new)
            x = _matmul_res(og, w["gla_w_out"], j, x)
        final = layer == DEPTH - 1
        if ffn_bf16 is None:
            x, w_layer = _ffn(x, w["norm_ffn"], layer, w["norm_final"], final,
                              w_f32=(w["ffn_w_up"], w["ffn_w_down"]))
            ffn_out.append(w_layer)
        else:
            x = _ffn(x, w["norm_ffn"], layer, w["norm_final"], final, w_bf16=ffn_bf16[layer])
    return (x.reshape(nseq, tlen, d), jnp.stack(hs), jnp.stack(cs), jnp.stack(ss)), ffn_out


def _prepare_weights(norm_mix, norm_ffn, norm_final, rg_w_in, rg_conv_w, rg_conv_b, rg_w_a, rg_b_a, rg_w_x, rg_b_x,
                     rg_lambda, rg_w_out, gla_w_in, gla_w_gk2, gla_b_gk, gla_norm_w, gla_w_out, ffn_w_up, ffn_w_down):
    glr_pad = LANES - GATE_RANK
    return {
        "norm_mix": norm_mix[:, None, :],
        "norm_ffn": norm_ffn[:, None, :],
        "norm_final": norm_final[None, :],
        "rg_w_in": rg_w_in.astype(BF16),
        "rg_conv_w": rg_conv_w,
        "rg_conv_b": rg_conv_b[:, None, :],
        "rg_w_a": rg_w_a.astype(BF16),
        "rg_b_a": rg_b_a[:, None, :],
        "rg_w_x": rg_w_x.astype(BF16),
        "rg_b_x": rg_b_x[:, None, :],
        "rg_lambda": rg_lambda[:, None, :],
        "rg_w_out": rg_w_out.astype(BF16),
        "gla_w_in": gla_w_in.astype(BF16),
        "gla_w_gk2": jnp.pad(gla_w_gk2, ((0, 0), (0, glr_pad), (0, 0))).astype(BF16),
        "gla_b_gk": gla_b_gk[:, None, :],
        "gla_norm_w": gla_norm_w[:, None, :],
        "gla_w_out": gla_w_out.astype(BF16),
        "ffn_w_up": ffn_w_up,
        "ffn_w_down": ffn_w_down,
    }


def kernel(x_prompt, x_sample, state_rglru_h, state_rglru_conv, state_gla, norm_mix, norm_ffn, norm_final, rg_w_in, rg_conv_w, rg_conv_b, rg_w_a, rg_b_a, rg_w_x, rg_b_x, rg_lambda, rg_w_out, gla_w_in, gla_w_gk2, gla_b_gk, gla_norm_w, gla_w_out, ffn_w_up, ffn_w_down):
    w = _prepare_weights(norm_mix, norm_ffn, norm_final, rg_w_in, rg_conv_w, rg_conv_b, rg_w_a, rg_b_a, rg_w_x,
                         rg_b_x, rg_lambda, rg_w_out, gla_w_in, gla_w_gk2, gla_b_gk, gla_norm_w, gla_w_out,
                         ffn_w_up, ffn_w_down)
    bp = x_prompt.shape[0]
    n_rg = state_rglru_h.shape[0]
    n_gla = state_gla.shape[0]
    h0 = jnp.zeros((n_rg, bp, D_RNN), F32)
    c0 = jnp.zeros((n_rg, bp, CONV_W - 1, D_RNN), F32)
    s0 = jnp.zeros((n_gla, bp, GLA_HEADS, HEAD_K, HEAD_V), F32)
    (y_s, h_s, c_s, s_s), ffn_bf16 = _trunk(x_sample, state_rglru_h, state_rglru_conv, state_gla, w)
    (y_p, h_p, c_p, s_p), _ = _trunk(x_prompt, h0, c0, s0, w, ffn_bf16)
    return (y_p, y_s, h_p, c_p, s_p, h_s, c_s, s_s)
```

```python
import functools

import jax
import jax.numpy as jnp
from jax import lax
from jax.experimental import pallas as pl
from jax.experimental.pallas import tpu as pltpu

F32 = jnp.float32
BF16 = jnp.bfloat16

D_MODEL = 2048
DEPTH = 4
CHUNK = 64
EPS = 1e-6
D_RNN = D_MODEL
RG_BLOCKS = 8
RG_BW = D_RNN // RG_BLOCKS
CONV_W = 4
RG_C = 8.0
GLA_HEADS = 4
HEAD_K = 256
HEAD_V = 512
GATE_RANK = 16
GATE_NORM = 16.0
GLA_DQ = GLA_HEADS * HEAD_K
GLA_DV = GLA_HEADS * HEAD_V
GLA_MAIN = 2 * GLA_DQ + 2 * GLA_DV
D_FF = 5632

LANES = 128
SUBLANES = 8
NORM_ROWS = 16
NORM_UNROLL = 8
MIB = 1024 * 1024
TINY = 1e-30
GELU_C1 = 0.7978845608028654
GELU_C2 = 0.044715 * GELU_C1

MATMUL_TM = 1024
MATMUL_TN = 1024
FFN_TM = 1024
FFN_TF = 512
FFN_CAST_TF = 256
RGLRU_TC = 512
GLA_SEQS = 4


def _params(dims, vmem_mib):
    return pltpu.CompilerParams(dimension_semantics=dims, vmem_limit_bytes=vmem_mib * MIB)


def _rms(x, w):
    ms = jnp.mean(x * x, axis=-1, keepdims=True)
    return x * lax.rsqrt(ms + EPS) * w


def _log_sigmoid(x):
    return jnp.minimum(x, 0.0) - jnp.log(1.0 + jnp.exp(-jnp.abs(x)))


def _split3_bf16(x):
    hi = x.astype(BF16)
    r1 = x - hi.astype(F32)
    mid = r1.astype(BF16)
    lo = (r1 - mid.astype(F32)).astype(BF16)
    return hi, mid, lo


def _sqrt_nonneg(y):
    return y * lax.rsqrt(jnp.maximum(y, TINY))


def _gelu_tanh(x):
    inner = x * (GELU_C1 + GELU_C2 * (x * x))
    return (0.5 * x) * (1.0 + jnp.tanh(inner))


def _norm_rows(src_ref, nw_ref, dst_ref, rows, dtype):
    nw = nw_ref[...]

    def body(r, carry):
        sl = pl.ds(pl.multiple_of(r * NORM_ROWS, NORM_ROWS), NORM_ROWS)
        dst_ref[sl, :] = _rms(src_ref[sl, :], nw).astype(dtype)
        return carry

    lax.fori_loop(0, rows // NORM_ROWS, body, 0, unroll=NORM_UNROLL)


def _norm_matmul_kernel(x_ref, nw_ref, w_ref, w2_ref, o_ref, o2_ref, hn_ref, *, tm):
    @pl.when(pl.program_id(1) == 0)
    def _():
        _norm_rows(x_ref, nw_ref, hn_ref, tm, BF16)
        if w2_ref is not None:
            o2_ref[...] = jnp.dot(hn_ref[...], w2_ref[...], preferred_element_type=F32)

    o_ref[...] = jnp.dot(hn_ref[...], w_ref[...], preferred_element_type=F32)


def _norm_matmul_plain_kernel(x_ref, nw_ref, w_ref, o_ref, hn_ref, *, tm):
    _norm_matmul_kernel(x_ref, nw_ref, w_ref, None, o_ref, None, hn_ref, tm=tm)


def _norm_matmul_cast_kernel(x_ref, nw_ref, w_ref, o_ref, wo_ref, hn_ref, *, tm):
    wo_ref[...] = w_ref[...].astype(BF16)
    _norm_matmul_kernel(x_ref, nw_ref, wo_ref, None, o_ref, None, hn_ref, tm=tm)


def _norm_matmul_cast2_kernel(x_ref, nw_ref, w_ref, w2_ref, o_ref, o2_ref, wo_ref, w2o_ref, hn_ref, *, tm, side):
    wo_ref[...] = w_ref[...].astype(BF16)

    @pl.when(pl.program_id(1) == 0)
    def _():
        lane = lax.broadcasted_iota(jnp.int32, w2_ref.shape, 1)
        w2o_ref[...] = jnp.where(lane < side, w2_ref[...], 0.0).astype(BF16)

    _norm_matmul_kernel(x_ref, nw_ref, wo_ref, w2o_ref, o_ref, o2_ref, hn_ref, tm=tm)


def _norm_matmul(x, nws, layer, n, side=0, w_bf16=None, w_f32=None, j=None):
    m, d = x.shape
    tm = min(m, MATMUL_TM)
    tn = MATMUL_TN
    grid = (m // tm, n // tn)
    x_spec = pl.BlockSpec((tm, d), lambda i, c: (i, 0))
    nw_spec = pl.BlockSpec((None, 1, d), lambda i, c: (layer, 0, 0))
    w_spec = pl.BlockSpec((d, tn), lambda i, c: (0, c))
    w2_spec = pl.BlockSpec((d, LANES), lambda i, c: (0, 0))
    o_spec = pl.BlockSpec((tm, tn), lambda i, c: (i, c))
    o2_spec = pl.BlockSpec((tm, LANES), lambda i, c: (i, 0))
    o_shape = jax.ShapeDtypeStruct((m, n), F32)
    o2_shape = jax.ShapeDtypeStruct((m, LANES), F32)
    scratch = [pltpu.VMEM((tm, d), BF16)]
    cp = _params(("parallel", "arbitrary"), 48)
    if w_bf16 is not None:
        w, w2 = w_bf16
        if not side:
            return pl.pallas_call(
                functools.partial(_norm_matmul_plain_kernel, tm=tm), grid=grid,
                in_specs=[x_spec, nw_spec, w_spec], out_specs=o_spec, out_shape=o_shape,
                scratch_shapes=scratch, compiler_params=cp, name="norm_matmul")(x, nws, w)
        return pl.pallas_call(
            functools.partial(_norm_matmul_kernel, tm=tm), grid=grid,
            in_specs=[x_spec, nw_spec, w_spec, w2_spec], out_specs=[o_spec, o2_spec],
            out_shape=[o_shape, o2_shape], scratch_shapes=scratch, compiler_params=cp,
            name="norm_matmul2")(x, nws, w, w2)
    assert m == tm, "the casting variant rewrites the bf16 weights once per row block"
    wf_spec = pl.BlockSpec((None, d, tn), lambda i, c: (j, 0, c))
    w_shape = jax.ShapeDtypeStruct((d, n), BF16)
    if not side:
        out, w = pl.pallas_call(
            functools.partial(_norm_matmul_cast_kernel, tm=tm), grid=grid,
            in_specs=[x_spec, nw_spec, wf_spec], out_specs=[o_spec, w_spec], out_shape=[o_shape, w_shape],
            scratch_shapes=scratch, compiler_params=cp, name="norm_matmul_cast")(x, nws, w_f32)
        return out, (w, None)
    out, out2, w, w2 = pl.pallas_call(
        functools.partial(_norm_matmul_cast2_kernel, tm=tm, side=side), grid=grid,
        in_specs=[x_spec, nw_spec, wf_spec, pl.BlockSpec((None, d, LANES), lambda i, c: (j, 0, n // LANES))],
        out_specs=[o_spec, o2_spec, w_spec, w2_spec],
        out_shape=[o_shape, o2_shape, w_shape, jax.ShapeDtypeStruct((d, LANES), BF16)],
        scratch_shapes=scratch, compiler_params=cp, name="norm_matmul2_cast")(x, nws, w_f32, w_f32)
    return (out, out2), (w, w2)


def _matmul_res_kernel(a_ref, w_ref, x_ref, o_ref):
    o_ref[...] = x_ref[...] + jnp.dot(a_ref[...], w_ref[...], preferred_element_type=F32)


def _matmul_res_cast_kernel(a_ref, w_ref, x_ref, o_ref, wo_ref):
    wo_ref[...] = w_ref[...].astype(BF16)
    _matmul_res_kernel(a_ref, wo_ref, x_ref, o_ref)


def _matmul_res(a, x, w_bf16=None, w_f32=None, j=None):
    m, k = a.shape
    n = x.shape[1]
    tm = min(m, MATMUL_TM)
    tn = MATMUL_TN
    grid = (m // tm, n // tn)
    a_spec = pl.BlockSpec((tm, k), lambda i, c: (i, 0))
    w_spec = pl.BlockSpec((k, tn), lambda i, c: (0, c))
    xo_spec = pl.BlockSpec((tm, tn), lambda i, c: (i, c))
    o_shape = jax.ShapeDtypeStruct((m, n), F32)
    cp = _params(("parallel", "arbitrary"), 48)
    if w_bf16 is not None:
        return pl.pallas_call(
            _matmul_res_kernel, grid=grid, in_specs=[a_spec, w_spec, xo_spec], out_specs=xo_spec,
            out_shape=o_shape, compiler_params=cp, name="matmul_res")(a, w_bf16, x)
    assert m == tm, "the casting variant rewrites the bf16 weight once per row block"
    return pl.pallas_call(
        _matmul_res_cast_kernel, grid=grid,
        in_specs=[a_spec, pl.BlockSpec((None, k, tn), lambda i, c: (j, 0, c)), xo_spec],
        out_specs=[xo_spec, w_spec], out_shape=[o_shape, jax.ShapeDtypeStruct((k, n), BF16)],
        compiler_params=cp, name="matmul_res_cast")(a, w_f32, x)


def _ffn_kernel(x_ref, nw_ref, wg_ref, wu_ref, wd_ref, fw_ref, o_ref, hn_ref, *, tm, nf, final):
    f = pl.program_id(1)

    @pl.when(f == 0)
    def _():
        _norm_rows(x_ref, nw_ref, hn_ref, tm, BF16)
        o_ref[...] = x_ref[...]

    hn = hn_ref[...]
    gt = jnp.dot(hn, wg_ref[...], preferred_element_type=F32)
    up = jnp.dot(hn, wu_ref[...], preferred_element_type=F32)
    act = ((gt * jax.nn.sigmoid(gt)) * up).astype(BF16)
    o_ref[...] += jnp.dot(act, wd_ref[...], preferred_element_type=F32)

    if final:
        @pl.when(f == nf - 1)
        def _():
            _norm_rows(o_ref, fw_ref, o_ref, tm, F32)


def _ffn_cast_kernel(x_ref, nw_ref, wg_ref, wu_ref, wd_ref, fw_ref, o_ref, wgo_ref, wuo_ref, wdo_ref, hn_ref, **kw):
    wgo_ref[...] = wg_ref[...].astype(BF16)
    wuo_ref[...] = wu_ref[...].astype(BF16)
    wdo_ref[...] = wd_ref[...].astype(BF16)
    _ffn_kernel(x_ref, nw_ref, wgo_ref, wuo_ref, wdo_ref, fw_ref, o_ref, hn_ref, **kw)


def _ffn(x, nws, layer, fw, final, w_bf16=None, w_f32=None):
    m, d = x.shape
    tm = min(m, FFN_TM)
    tf = FFN_TF if w_bf16 is not None else FFN_CAST_TF
    nf = D_FF // tf
    kw = dict(tm=tm, nf=nf, final=final)
    x_spec = pl.BlockSpec((tm, d), lambda i, f: (i, 0), pipeline_mode=pl.Buffered(1))
    nw_spec = pl.BlockSpec((None, 1, d), lambda i, f: (layer, 0, 0))
    fw_spec = pl.BlockSpec((1, d), lambda i, f: (0, 0))
    o_spec = pl.BlockSpec((tm, d), lambda i, f: (i, 0))
    o_shape = jax.ShapeDtypeStruct((m, d), F32)
    scratch = [pltpu.VMEM((tm, d), BF16)]
    cp = _params(("parallel", "arbitrary"), 56)
    up_spec = pl.BlockSpec((d, tf), lambda i, f: (0, f))
    down_spec = pl.BlockSpec((tf, d), lambda i, f: (f, 0))
    if w_bf16 is not None:
        return pl.pallas_call(
            functools.partial(_ffn_kernel, **kw), grid=(m // tm, nf),
            in_specs=[x_spec, nw_spec, up_spec, up_spec, down_spec, fw_spec],
            out_specs=o_spec, out_shape=o_shape, scratch_shapes=scratch,
            compiler_params=cp, name="ffn")(x, nws, *w_bf16, fw)
    assert m == tm, "the casting variant rewrites the bf16 weights once per row block"
    w_ups, w_downs = w_f32
    out, wg, wu, wd = pl.pallas_call(
        functools.partial(_ffn_cast_kernel, **kw), grid=(1, nf),
        in_specs=[x_spec, nw_spec,
                  pl.BlockSpec((None, d, tf), lambda i, f: (layer, 0, f)),
                  pl.BlockSpec((None, d, tf), lambda i, f: (layer, 0, f + nf)),
                  pl.BlockSpec((None, tf, d), lambda i, f: (layer, f, 0)),
                  fw_spec],
        out_specs=[o_spec, up_spec, up_spec, down_spec],
        out_shape=[o_shape, jax.ShapeDtypeStruct((d, D_FF), BF16), jax.ShapeDtypeStruct((d, D_FF), BF16),
                   jax.ShapeDtypeStruct((D_FF, d), BF16)],
        scratch_shapes=scratch, compiler_params=cp, name="ffn_cast")(x, nws, w_ups, w_ups, w_downs, fw)
    return out, (wg, wu, wd)


def _rglru_kernel(gate_ref, u_ref, h0_ref, c0_ref, cw_ref, cb_ref, wa_ref, ba_ref, wx_ref, bx_ref, lam_ref,
                  hg_ref, hout_ref, cout_ref, uext_ref, *, tc, nt):
    t = pl.program_id(2)
    pad = SUBLANES

    @pl.when(t == 0)
    def _():
        uext_ref[...] = jnp.zeros((pad, RG_BW), F32)
        uext_ref[pad - (CONV_W - 1):pad, :] = c0_ref[...]
        hout_ref[...] = h0_ref[...]

    u = u_ref[...]
    ext = jnp.concatenate([uext_ref[...], u], axis=0)
    cw = cw_ref[...]
    acc = pltpu.roll(ext, 3, 0)[pad:, :] * cw[0:1, :]
    acc = acc + pltpu.roll(ext, 2, 0)[pad:, :] * cw[1:2, :]
    acc = acc + pltpu.roll(ext, 1, 0)[pad:, :] * cw[2:3, :]
    acc = acc + u * cw[3:4, :]
    uc = cb_ref[...] + acc
    ucb = uc.astype(BF16)
    r = jax.nn.sigmoid(jnp.dot(ucb, wa_ref[...], preferred_element_type=F32) + ba_ref[...])
    i = jax.nn.sigmoid(jnp.dot(ucb, wx_ref[...], preferred_element_type=F32) + bx_ref[...])
    log_a = (RG_C * r) * _log_sigmoid(lam_ref[...])
    a = jnp.exp(log_a)
    b = _sqrt_nonneg(1.0 - a * a) * (i * uc)

    groups = tc // SUBLANES
    a3 = a.reshape(groups, SUBLANES, RG_BW)
    b3 = b.reshape(groups, SUBLANES, RG_BW)
    row = lax.broadcasted_iota(jnp.int32, (groups, SUBLANES, RG_BW), 1)
    for s in (1, 2, 4):
        keep = row >= s
        a_prev = pltpu.roll(a3, s, 1)
        b_prev = pltpu.roll(b3, s, 1)
        b3 = jnp.where(keep, b3 + a3 * b_prev, b3)
        a3 = jnp.where(keep, a3 * a_prev, a3)
    h_prev = jnp.broadcast_to(hout_ref[...], (SUBLANES, RG_BW))
    hs = []
    for g in range(groups):
        h_g = a3[g] * h_prev + b3[g]
        hs.append(h_g)
        h_prev = jnp.broadcast_to(h_g[SUBLANES - 1:SUBLANES, :], (SUBLANES, RG_BW))
    h = jnp.concatenate(hs, axis=0)
    hout_ref[...] = hs[-1][SUBLANES - 1:SUBLANES, :]
    hg_ref[...] = (h * _gelu_tanh(gate_ref[...])).astype(BF16)

    uext_ref[...] = u[tc - pad:, :]

    @pl.when(t == nt - 1)
    def _():
        cout_ref[...] = uext_ref[pad - (CONV_W - 1):pad, :]


def _rglru_core(gu, nseq, tlen, j, h_all, c_all, w):
    m = nseq * tlen
    tc = min(tlen, RGLRU_TC)
    nt = tlen // tc
    bw = RG_BW
    row = lambda b, n, t: b * nt + t
    vec = pl.BlockSpec((None, 1, bw), lambda b, n, t: (j, 0, n))
    gate_spec = pl.BlockSpec((tc, bw), lambda b, n, t: (row(b, n, t), n))
    u_spec = pl.BlockSpec((tc, bw), lambda b, n, t: (row(b, n, t), RG_BLOCKS + n))
    hin_spec = pl.BlockSpec((None, None, 1, bw), lambda b, n, t: (j, b, 0, n))
    cin_spec = pl.BlockSpec((None, None, CONV_W - 1, bw), lambda b, n, t: (j, b, 0, n))
    hout_spec = pl.BlockSpec((None, 1, bw), lambda b, n, t: (b, 0, n))
    cout_spec = pl.BlockSpec((None, CONV_W - 1, bw), lambda b, n, t: (b, 0, n))
    w_spec = pl.BlockSpec((None, None, bw, bw), lambda b, n, t: (j, n, 0, 0))
    return pl.pallas_call(
        functools.partial(_rglru_kernel, tc=tc, nt=nt),
        grid=(nseq, RG_BLOCKS, nt),
        in_specs=[gate_spec, u_spec, hin_spec, cin_spec,
                  pl.BlockSpec((None, CONV_W, bw), lambda b, n, t: (j, 0, n)), vec,
                  w_spec, vec, w_spec, vec, vec],
        out_specs=[pl.BlockSpec((tc, bw), lambda b, n, t: (row(b, n, t), n)), hout_spec, cout_spec],
        out_shape=[jax.ShapeDtypeStruct((m, D_RNN), BF16),
                   jax.ShapeDtypeStruct((nseq, 1, D_RNN), F32),
                   jax.ShapeDtypeStruct((nseq, CONV_W - 1, D_RNN), F32)],
        scratch_shapes=[pltpu.VMEM((SUBLANES, bw), F32)],
        compiler_params=_params(("parallel", "parallel", "arbitrary"), 32),
        name="rglru_core")(gu, gu, h_all, c_all, w["rg_conv_w"], w["rg_conv_b"], w["rg_w_a"], w["rg_b_a"],
                           w["rg_w_x"], w["rg_b_x"], w["rg_lambda"])


def _gla_kernel(q_ref, k_ref, v_ref, g_ref, glr_ref, s0_ref, w2_ref, bgk_ref, gnw_ref,
                o_ref, sout_ref, st_ref, *, nt, nb):
    t = pl.program_id(1)

    @pl.when(t == 0)
    def _():
        for s in range(nb):
            for h in range(GLA_HEADS):
                st_ref[s, h] = s0_ref[s, h].T

    c = CHUNK
    glr = glr_ref[...].reshape(nb * c, LANES).astype(BF16)
    gk_all = jnp.dot(glr, w2_ref[...], preferred_element_type=F32) + bgk_ref[...]
    gk_all = _log_sigmoid(gk_all) * (1.0 / GATE_NORM)
    rows = lax.broadcasted_iota(jnp.int32, (c, c), 0)
    cols = lax.broadcasted_iota(jnp.int32, (c, c), 1)
    tri = rows >= cols
    tri_b = tri.astype(BF16)
    gnw = gnw_ref[...]
    nt_dims = (((1,), (1,)), ((), ()))
    tn_dims = (((0,), (0,)), ((), ()))
    prep = []
    for s in range(nb):
        gk = gk_all[s * c:(s + 1) * c, :]
        bcum = sum(jnp.dot(tri_b, piece, preferred_element_type=F32) for piece in _split3_bf16(gk))
        blast = bcum[c - 1:c, :]
        k = k_ref[s]
        qe = ((q_ref[s] * (HEAD_K ** -0.5)) * jnp.exp(bcum)).astype(BF16)
        ke = (k * jnp.exp(-bcum)).astype(BF16)
        kd = (k * jnp.exp(blast - bcum)).astype(BF16)
        prep.append((qe, ke, kd, jnp.exp(blast)))
    for h in range(GLA_HEADS):
        ks = slice(h * HEAD_K, (h + 1) * HEAD_K)
        vs = slice(h * HEAD_V, (h + 1) * HEAD_V)
        for s in range(nb):
            qe, ke, kd, decay = prep[s]
            v_h = v_ref[s, :, vs].astype(BF16)
            att = lax.dot_general(qe[:, ks], ke[:, ks], nt_dims, preferred_element_type=F32)
            att = jnp.where(tri, att, 0.0).astype(BF16)
            st = st_ref[s, h]
            o = jnp.dot(att, v_h, preferred_element_type=F32)
            o = o + lax.dot_general(qe[:, ks], st.astype(BF16), nt_dims, preferred_element_type=F32)
            st_ref[s, h] = st * decay[:, ks] + lax.dot_general(v_h, kd[:, ks], tn_dims,
                                                               preferred_element_type=F32)
            on = o * lax.rsqrt(jnp.mean(o * o, axis=-1, keepdims=True) + EPS) * gnw
            g_h = g_ref[s, :, vs]
            o_ref[s, :, vs] = (on * (g_h * jax.nn.sigmoid(g_h))).astype(BF16)

    @pl.when(t == nt - 1)
    def _():
        for s in range(nb):
            for h in range(GLA_HEADS):
                sout_ref[s, h] = st_ref[s, h].T


def _gla_core(qkvg, glr, nseq, tlen, j, s_all, w):
    c = CHUNK
    nt = tlen // c
    nb = GLA_SEQS
    qkvg = qkvg.reshape(nseq, tlen, GLA_MAIN)
    glr = glr.reshape(nseq, tlen, LANES)
    og, s_new = pl.pallas_call(
        functools.partial(_gla_kernel, nt=nt, nb=nb),
        grid=(nseq // nb, nt),
        in_specs=[pl.BlockSpec((nb, c, GLA_DQ), lambda b, t: (b, t, 0)),
                  pl.BlockSpec((nb, c, GLA_DQ), lambda b, t: (b, t, 1)),
                  pl.BlockSpec((nb, c, GLA_DV), lambda b, t: (b, t, 1)),
                  pl.BlockSpec((nb, c, GLA_DV), lambda b, t: (b, t, 2)),
                  pl.BlockSpec((nb, c, LANES), lambda b, t: (b, t, 0)),
                  pl.BlockSpec((None, nb, GLA_HEADS, HEAD_K, HEAD_V), lambda b, t: (j, b, 0, 0, 0),
                               pipeline_mode=pl.Buffered(1)),
                  pl.BlockSpec((None, LANES, GLA_DQ), lambda b, t: (j, 0, 0)),
                  pl.BlockSpec((None, 1, GLA_DQ), lambda b, t: (j, 0, 0)),
                  pl.BlockSpec((None, 1, HEAD_V), lambda b, t: (j, 0, 0))],
        out_specs=[pl.BlockSpec((nb, c, GLA_DV), lambda b, t: (b, t, 0)),
                   pl.BlockSpec((nb, GLA_HEADS, HEAD_K, HEAD_V), lambda b, t: (b, 0, 0, 0))],
        out_shape=[jax.ShapeDtypeStruct((nseq, tlen, GLA_DV), BF16),
                   jax.ShapeDtypeStruct((nseq, GLA_HEADS, HEAD_K, HEAD_V), F32)],
        scratch_shapes=[pltpu.VMEM((nb, GLA_HEADS, HEAD_V, HEAD_K), F32)],
        compiler_params=_params(("parallel", "arbitrary"), 56),
        name="gla_core")(qkvg, qkvg, qkvg, qkvg, glr, s_all, w["gla_w_gk2"], w["gla_b_gk"], w["gla_norm_w"])
    return og.reshape(nseq * tlen, GLA_DV), s_new


def _trunk(x, h_all, conv_all, s_all, w, bf16=None):
    nseq, tlen, d = x.shape
    x = x.reshape(nseq * tlen, d)
    h_all = h_all[:, :, None, :]
    cast = bf16 is None
    made = []
    hs, cs, ss = [], [], []
    for layer in range(DEPTH):
        j = layer // 2
        wl = {} if cast else bf16[layer]
        if layer % 2 == 0:
            if cast:
                gu, wl["in"] = _norm_matmul(x, w["norm_mix"], layer, 2 * D_RNN, w_f32=w["rg_w_in"], j=j)
            else:
                gu = _norm_matmul(x, w["norm_mix"], layer, 2 * D_RNN, w_bf16=wl["in"])
            mixed, h_new, c_new = _rglru_core(gu, nseq, tlen, j, h_all, conv_all, w)
            hs.append(h_new[:, 0, :])
            cs.append(c_new)
            w_out = w["rg_w_out"]
        else:
            if cast:
                (qkvg, glr), wl["in"] = _norm_matmul(x, w["norm_mix"], layer, GLA_MAIN, side=GATE_RANK,
                                                     w_f32=w["gla_w_in"], j=j)
            else:
                qkvg, glr = _norm_matmul(x, w["norm_mix"], layer, GLA_MAIN, side=GATE_RANK, w_bf16=wl["in"])
            mixed, s_new = _gla_core(qkvg, glr, nseq, tlen, j, s_all, w)
            ss.append(s_new)
            w_out = w["gla_w_out"]
        final = layer == DEPTH - 1
        if cast:
            x, wl["out"] = _matmul_res(mixed, x, w_f32=w_out, j=j)
            x, wl["ffn"] = _ffn(x, w["norm_ffn"], layer, w["norm_final"], final,
                                w_f32=(w["ffn_w_up"], w["ffn_w_down"]))
            made.append(wl)
        else:
            x = _matmul_res(mixed, x, w_bf16=wl["out"])
            x = _ffn(x, w["norm_ffn"], layer, w["norm_final"], final, w_bf16=wl["ffn"])
    return (x.reshape(nseq, tlen, d), jnp.stack(hs), jnp.stack(cs), jnp.stack(ss)), made


def _prepare_weights(norm_mix, norm_ffn, norm_final, rg_w_in, rg_conv_w, rg_conv_b, rg_w_a, rg_b_a, rg_w_x, rg_b_x,
                     rg_lambda, rg_w_out, gla_w_in, gla_w_gk2, gla_b_gk, gla_norm_w, gla_w_out, ffn_w_up, ffn_w_down):
    glr_pad = LANES - GATE_RANK
    return {
        "norm_mix": norm_mix[:, None, :],
        "norm_ffn": norm_ffn[:, None, :],
        "norm_final": norm_final[None, :],
        "rg_w_in": rg_w_in,
        "rg_conv_w": rg_conv_w,
        "rg_conv_b": rg_conv_b[:, None, :],
        "rg_w_a": rg_w_a.astype(BF16),
        "rg_b_a": rg_b_a[:, None, :],
        "rg_w_x": rg_w_x.astype(BF16),
        "rg_b_x": rg_b_x[:, None, :],
        "rg_lambda": rg_lambda[:, None, :],
        "rg_w_out": rg_w_out,
        "gla_w_in": gla_w_in,
        "gla_w_gk2": jnp.pad(gla_w_gk2, ((0, 0), (0, glr_pad), (0, 0))).astype(BF16),
        "gla_b_gk": gla_b_gk[:, None, :],
        "gla_norm_w": gla_norm_w[:, None, :],
        "gla_w_out": gla_w_out,
        "ffn_w_up": ffn_w_up,
        "ffn_w_down": ffn_w_down,
    }


def kernel(x_prompt, x_sample, state_rglru_h, state_rglru_conv, state_gla, norm_mix, norm_ffn, norm_final, rg_w_in, rg_conv_w, rg_conv_b, rg_w_a, rg_b_a, rg_w_x, rg_b_x, rg_lambda, rg_w_out, gla_w_in, gla_w_gk2, gla_b_gk, gla_norm_w, gla_w_out, ffn_w_up, ffn_w_down):
    w = _prepare_weights(norm_mix, norm_ffn, norm_final, rg_w_in, rg_conv_w, rg_conv_b, rg_w_a, rg_b_a, rg_w_x,
                         rg_b_x, rg_lambda, rg_w_out, gla_w_in, gla_w_gk2, gla_b_gk, gla_norm_w, gla_w_out,
                         ffn_w_up, ffn_w_down)
    bp = x_prompt.shape[0]
    n_rg = state_rglru_h.shape[0]
    n_gla = state_gla.shape[0]
    h0 = jnp.zeros((n_rg, bp, D_RNN), F32)
    c0 = jnp.zeros((n_rg, bp, CONV_W - 1, D_RNN), F32)
    s0 = jnp.zeros((n_gla, bp, GLA_HEADS, HEAD_K, HEAD_V), F32)
    (y_s, h_s, c_s, s_s), bf16 = _trunk(x_sample, state_rglru_h, state_rglru_conv, state_gla, w)
    (y_p, h_p, c_p, s_p), _ = _trunk(x_prompt, h0, c0, s0, w, bf16)
    return (y_p, y_s, h_p, c_p, s_p, h_s, c_s, s_s)
```

```python
import functools

import jax
import jax.numpy as jnp
from jax import lax
from jax.experimental import pallas as pl
from jax.experimental.pallas import tpu as pltpu

F32 = jnp.float32
BF16 = jnp.bfloat16

D_MODEL = 2048
DEPTH = 4
CHUNK = 64
EPS = 1e-6
D_RNN = D_MODEL
RG_BLOCKS = 8
RG_BW = D_RNN // RG_BLOCKS
CONV_W = 4
RG_C = 8.0
GLA_HEADS = 4
HEAD_K = 256
HEAD_V = 512
GATE_RANK = 16
GATE_NORM = 16.0
GLA_DQ = GLA_HEADS * HEAD_K
GLA_DV = GLA_HEADS * HEAD_V
GLA_MAIN = 2 * GLA_DQ + 2 * GLA_DV
D_FF = 5632

LANES = 128
SUBLANES = 8
NORM_ROWS = 16
NORM_UNROLL = 8
MIB = 1024 * 1024
TINY = 1e-30
GELU_C1 = 0.7978845608028654
GELU_C2 = 0.044715 * GELU_C1

MATMUL_TM = 1024
MATMUL_TN = 1024
FFN_TM = 1024
FFN_TF = 512
FFN_CAST_TF = 256
RGLRU_TC = 512
GLA_SEQS = 4


def _params(dims, vmem_mib):
    return pltpu.CompilerParams(dimension_semantics=dims, vmem_limit_bytes=vmem_mib * MIB)


def _rms(x, w):
    ms = jnp.mean(x * x, axis=-1, keepdims=True)
    return x * lax.rsqrt(ms + EPS) * w


def _log_sigmoid(x):
    return jnp.minimum(x, 0.0) - jnp.log(1.0 + jnp.exp(-jnp.abs(x)))


def _split3_bf16(x):
    hi = x.astype(BF16)
    r1 = x - hi.astype(F32)
    mid = r1.astype(BF16)
    lo = (r1 - mid.astype(F32)).astype(BF16)
    return hi, mid, lo


def _sqrt_nonneg(y):
    return y * lax.rsqrt(jnp.maximum(y, TINY))


def _gelu_tanh(x):
    inner = x * (GELU_C1 + GELU_C2 * (x * x))
    return (0.5 * x) * (1.0 + jnp.tanh(inner))


def _norm_rows(src_ref, nw_ref, dst_ref, rows, dtype):
    nw = nw_ref[...]

    def body(r, carry):
        sl = pl.ds(pl.multiple_of(r * NORM_ROWS, NORM_ROWS), NORM_ROWS)
        dst_ref[sl, :] = _rms(src_ref[sl, :], nw).astype(dtype)
        return carry

    lax.fori_loop(0, rows // NORM_ROWS, body, 0, unroll=NORM_UNROLL)


def _norm_matmul_kernel(x_ref, nw_ref, w_ref, w2_ref, o_ref, o2_ref, hn_ref, *, tm):
    @pl.when(pl.program_id(1) == 0)
    def _():
        _norm_rows(x_ref, nw_ref, hn_ref, tm, BF16)
        if w2_ref is not None:
            o2_ref[...] = jnp.dot(hn_ref[...], w2_ref[...], preferred_element_type=F32)

    o_ref[...] = jnp.dot(hn_ref[...], w_ref[...], preferred_element_type=F32)


def _norm_matmul_plain_kernel(x_ref, nw_ref, w_ref, o_ref, hn_ref, *, tm):
    _norm_matmul_kernel(x_ref, nw_ref, w_ref, None, o_ref, None, hn_ref, tm=tm)


def _norm_matmul_cast_kernel(x_ref, nw_ref, w_ref, o_ref, wo_ref, hn_ref, *, tm):
    wo_ref[...] = w_ref[...].astype(BF16)
    _norm_matmul_kernel(x_ref, nw_ref, wo_ref, None, o_ref, None, hn_ref, tm=tm)


def _norm_matmul_cast2_kernel(x_ref, nw_ref, w_ref, w2_ref, o_ref, o2_ref, wo_ref, w2o_ref, hn_ref, *, tm, side):
    wo_ref[...] = w_ref[...].astype(BF16)

    @pl.when(pl.program_id(1) == 0)
    def _():
        lane = lax.broadcasted_iota(jnp.int32, w2_ref.shape, 1)
        w2o_ref[...] = jnp.where(lane < side, w2_ref[...], 0.0).astype(BF16)

    _norm_matmul_kernel(x_ref, nw_ref, wo_ref, w2o_ref, o_ref, o2_ref, hn_ref, tm=tm)


def _norm_matmul(x, nws, layer, n, side=0, w_bf16=None, w_f32=None, j=None):
    m, d = x.shape
    tm = min(m, MATMUL_TM)
    tn = MATMUL_TN
    grid = (m // tm, n // tn)
    x_spec = pl.BlockSpec((tm, d), lambda i, c: (i, 0))
    nw_spec = pl.BlockSpec((None, 1, d), lambda i, c: (layer, 0, 0))
    w_spec = pl.BlockSpec((d, tn), lambda i, c: (0, c))
    w2_spec = pl.BlockSpec((d, LANES), lambda i, c: (0, 0))
    o_spec = pl.BlockSpec((tm, tn), lambda i, c: (i, c))
    o2_spec = pl.BlockSpec((tm, LANES), lambda i, c: (i, 0))
    o_shape = jax.ShapeDtypeStruct((m, n), F32)
    o2_shape = jax.ShapeDtypeStruct((m, LANES), F32)
    scratch = [pltpu.VMEM((tm, d), BF16)]
    cp = _params(("parallel", "arbitrary"), 48)
    if w_bf16 is not None:
        w, w2 = w_bf16
        if not side:
            return pl.pallas_call(
                functools.partial(_norm_matmul_plain_kernel, tm=tm), grid=grid,
                in_specs=[x_spec, nw_spec, w_spec], out_specs=o_spec, out_shape=o_shape,
                scratch_shapes=scratch, compiler_params=cp, name="norm_matmul")(x, nws, w)
        return pl.pallas_call(
            functools.partial(_norm_matmul_kernel, tm=tm), grid=grid,
            in_specs=[x_spec, nw_spec, w_spec, w2_spec], out_specs=[o_spec, o2_spec],
            out_shape=[o_shape, o2_shape], scratch_shapes=scratch, compiler_params=cp,
            name="norm_matmul2")(x, nws, w, w2)
    assert m == tm, "the casting variant rewrites the bf16 weights once per row block"
    wf_spec = pl.BlockSpec((None, d, tn), lambda i, c: (j, 0, c))
    w_shape = jax.ShapeDtypeStruct((d, n), BF16)
    if not side:
        out, w = pl.pallas_call(
            functools.partial(_norm_matmul_cast_kernel, tm=tm), grid=grid,
            in_specs=[x_spec, nw_spec, wf_spec], out_specs=[o_spec, w_spec], out_shape=[o_shape, w_shape],
            scratch_shapes=scratch, compiler_params=cp, name="norm_matmul_cast")(x, nws, w_f32)
        return out, (w, None)
    out, out2, w, w2 = pl.pallas_call(
        functools.partial(_norm_matmul_cast2_kernel, tm=tm, side=side), grid=grid,
        in_specs=[x_spec, nw_spec, wf_spec, pl.BlockSpec((None, d, LANES), lambda i, c: (j, 0, n // LANES))],
        out_specs=[o_spec, o2_spec, w_spec, w2_spec],
        out_shape=[o_shape, o2_shape, w_shape, jax.ShapeDtypeStruct((d, LANES), BF16)],
        scratch_shapes=scratch, compiler_params=cp, name="norm_matmul2_cast")(x, nws, w_f32, w_f32)
    return (out, out2), (w, w2)


def _matmul_res_kernel(a_ref, w_ref, x_ref, o_ref):
    o_ref[...] = x_ref[...] + jnp.dot(a_ref[...], w_ref[...], preferred_element_type=F32)


def _matmul_res_cast_kernel(a_ref, w_ref, x_ref, o_ref, wo_ref):
    wo_ref[...] = w_ref[...].astype(BF16)
    _matmul_res_kernel(a_ref, wo_ref, x_ref, o_ref)


def _matmul_res(a, x, w_bf16=None, w_f32=None, j=None):
    m, k = a.shape
    n = x.shape[1]
    tm = min(m, MATMUL_TM)
    tn = n if w_bf16 is not None else MATMUL_TN
    grid = (m // tm, n // tn)
    a_spec = pl.BlockSpec((tm, k), lambda i, c: (i, 0))
    w_spec = pl.BlockSpec((k, tn), lambda i, c: (0, c))
    xo_spec = pl.BlockSpec((tm, tn), lambda i, c: (i, c))
    o_shape = jax.ShapeDtypeStruct((m, n), F32)
    cp = _params(("parallel", "arbitrary"), 56)
    if w_bf16 is not None:
        w_res_spec = pl.BlockSpec((k, n), lambda i, c: (0, 0), pipeline_mode=pl.Buffered(1))
        return pl.pallas_call(
            _matmul_res_kernel, grid=grid, in_specs=[a_spec, w_res_spec, xo_spec], out_specs=xo_spec,
            out_shape=o_shape, compiler_params=cp, name="matmul_res")(a, w_bf16, x)
    assert m == tm, "the casting variant rewrites the bf16 weight once per row block"
    return pl.pallas_call(
        _matmul_res_cast_kernel, grid=grid,
        in_specs=[a_spec, pl.BlockSpec((None, k, tn), lambda i, c: (j, 0, c)), xo_spec],
        out_specs=[xo_spec, w_spec], out_shape=[o_shape, jax.ShapeDtypeStruct((k, n), BF16)],
        compiler_params=cp, name="matmul_res_cast")(a, w_f32, x)


def _ffn_kernel(x_ref, nw_ref, wg_ref, wu_ref, wd_ref, fw_ref, o_ref, hn_ref, *, tm, nf, final):
    f = pl.program_id(1)

    @pl.when(f == 0)
    def _():
        _norm_rows(x_ref, nw_ref, hn_ref, tm, BF16)
        o_ref[...] = x_ref[...]

    hn = hn_ref[...]
    gt = jnp.dot(hn, wg_ref[...], preferred_element_type=F32)
    up = jnp.dot(hn, wu_ref[...], preferred_element_type=F32)
    act = ((gt * jax.nn.sigmoid(gt)) * up).astype(BF16)
    o_ref[...] += jnp.dot(act, wd_ref[...], preferred_element_type=F32)

    if final:
        @pl.when(f == nf - 1)
        def _():
            _norm_rows(o_ref, fw_ref, o_ref, tm, F32)


def _ffn_cast_kernel(x_ref, nw_ref, wg_ref, wu_ref, wd_ref, fw_ref, o_ref, wgo_ref, wuo_ref, wdo_ref, hn_ref, **kw):
    wgo_ref[...] = wg_ref[...].astype(BF16)
    wuo_ref[...] = wu_ref[...].astype(BF16)
    wdo_ref[...] = wd_ref[...].astype(BF16)
    _ffn_kernel(x_ref, nw_ref, wgo_ref, wuo_ref, wdo_ref, fw_ref, o_ref, hn_ref, **kw)


def _ffn(x, nws, layer, fw, final, w_bf16=None, w_f32=None):
    m, d = x.shape
    tm = min(m, FFN_TM)
    tf = FFN_TF if w_bf16 is not None else FFN_CAST_TF
    nf = D_FF // tf
    kw = dict(tm=tm, nf=nf, final=final)
    x_spec = pl.BlockSpec((tm, d), lambda i, f: (i, 0), pipeline_mode=pl.Buffered(1))
    nw_spec = pl.BlockSpec((None, 1, d), lambda i, f: (layer, 0, 0))
    fw_spec = pl.BlockSpec((1, d), lambda i, f: (0, 0))
    o_spec = pl.BlockSpec((tm, d), lambda i, f: (i, 0))
    o_shape = jax.ShapeDtypeStruct((m, d), F32)
    scratch = [pltpu.VMEM((tm, d), BF16)]
    cp = _params(("parallel", "arbitrary"), 56)
    up_spec = pl.BlockSpec((d, tf), lambda i, f: (0, f))
    down_spec = pl.BlockSpec((tf, d), lambda i, f: (f, 0))
    if w_bf16 is not None:
        return pl.pallas_call(
            functools.partial(_ffn_kernel, **kw), grid=(m // tm, nf),
            in_specs=[x_spec, nw_spec, up_spec, up_spec, down_spec, fw_spec],
            out_specs=o_spec, out_shape=o_shape, scratch_shapes=scratch,
            compiler_params=cp, name="ffn")(x, nws, *w_bf16, fw)
    assert m == tm, "the casting variant rewrites the bf16 weights once per row block"
    w_ups, w_downs = w_f32
    out, wg, wu, wd = pl.pallas_call(
        functools.partial(_ffn_cast_kernel, **kw), grid=(1, nf),
        in_specs=[x_spec, nw_spec,
                  pl.BlockSpec((None, d, tf), lambda i, f: (layer, 0, f)),
                  pl.BlockSpec((None, d, tf), lambda i, f: (layer, 0, f + nf)),
                  pl.BlockSpec((None, tf, d), lambda i, f: (layer, f, 0)),
                  fw_spec],
        out_specs=[o_spec, up_spec, up_spec, down_spec],
        out_shape=[o_shape, jax.ShapeDtypeStruct((d, D_FF), BF16), jax.ShapeDtypeStruct((d, D_FF), BF16),
                   jax.ShapeDtypeStruct((D_FF, d), BF16)],
        scratch_shapes=scratch, compiler_params=cp, name="ffn_cast")(x, nws, w_ups, w_ups, w_downs, fw)
    return out, (wg, wu, wd)


def _rglru_kernel(gate_ref, u_ref, h0_ref, c0_ref, cw_ref, cb_ref, wa_ref, ba_ref, wx_ref, bx_ref, lam_ref,
                  hg_ref, hout_ref, cout_ref, uext_ref, *, tc, nt):
    t = pl.program_id(2)
    pad = SUBLANES

    @pl.when(t == 0)
    def _():
        uext_ref[...] = jnp.zeros((pad, RG_BW), F32)
        uext_ref[pad - (CONV_W - 1):pad, :] = c0_ref[...]
        hout_ref[...] = h0_ref[...]

    u = u_ref[...]
    ext = jnp.concatenate([uext_ref[...], u], axis=0)
    cw = cw_ref[...]
    acc = pltpu.roll(ext, 3, 0)[pad:, :] * cw[0:1, :]
    acc = acc + pltpu.roll(ext, 2, 0)[pad:, :] * cw[1:2, :]
    acc = acc + pltpu.roll(ext, 1, 0)[pad:, :] * cw[2:3, :]
    acc = acc + u * cw[3:4, :]
    uc = cb_ref[...] + acc
    ucb = uc.astype(BF16)
    r = jax.nn.sigmoid(jnp.dot(ucb, wa_ref[...], preferred_element_type=F32) + ba_ref[...])
    i = jax.nn.sigmoid(jnp.dot(ucb, wx_ref[...], preferred_element_type=F32) + bx_ref[...])
    log_a = (RG_C * r) * _log_sigmoid(lam_ref[...])
    a = jnp.exp(log_a)
    b = _sqrt_nonneg(1.0 - a * a) * (i * uc)

    groups = tc // SUBLANES
    a3 = a.reshape(groups, SUBLANES, RG_BW)
    b3 = b.reshape(groups, SUBLANES, RG_BW)
    row = lax.broadcasted_iota(jnp.int32, (groups, SUBLANES, RG_BW), 1)
    for s in (1, 2, 4):
        keep = row >= s
        a_prev = pltpu.roll(a3, s, 1)
        b_prev = pltpu.roll(b3, s, 1)
        b3 = jnp.where(keep, b3 + a3 * b_prev, b3)
        a3 = jnp.where(keep, a3 * a_prev, a3)
    h_prev = jnp.broadcast_to(hout_ref[...], (SUBLANES, RG_BW))
    hs = []
    for g in range(groups):
        h_g = a3[g] * h_prev + b3[g]
        hs.append(h_g)
        h_prev = jnp.broadcast_to(h_g[SUBLANES - 1:SUBLANES, :], (SUBLANES, RG_BW))
    h = jnp.concatenate(hs, axis=0)
    hout_ref[...] = hs[-1][SUBLANES - 1:SUBLANES, :]
    hg_ref[...] = (h * _gelu_tanh(gate_ref[...])).astype(BF16)

    uext_ref[...] = u[tc - pad:, :]

    @pl.when(t == nt - 1)
    def _():
        cout_ref[...] = uext_ref[pad - (CONV_W - 1):pad, :]


def _rglru_core(gu, nseq, tlen, j, h_all, c_all, w):
    m = nseq * tlen
    tc = min(tlen, RGLRU_TC)
    nt = tlen // tc
    bw = RG_BW
    row = lambda b, n, t: b * nt + t
    vec = pl.BlockSpec((None, 1, bw), lambda b, n, t: (j, 0, n))
    gate_spec = pl.BlockSpec((tc, bw), lambda b, n, t: (row(b, n, t), n))
    u_spec = pl.BlockSpec((tc, bw), lambda b, n, t: (row(b, n, t), RG_BLOCKS + n))
    hin_spec = pl.BlockSpec((None, None, 1, bw), lambda b, n, t: (j, b, 0, n))
    cin_spec = pl.BlockSpec((None, None, CONV_W - 1, bw), lambda b, n, t: (j, b, 0, n))
    hout_spec = pl.BlockSpec((None, 1, bw), lambda b, n, t: (b, 0, n))
    cout_spec = pl.BlockSpec((None, CONV_W - 1, bw), lambda b, n, t: (b, 0, n))
    w_spec = pl.BlockSpec((None, None, bw, bw), lambda b, n, t: (j, n, 0, 0))
    return pl.pallas_call(
        functools.partial(_rglru_kernel, tc=tc, nt=nt),
        grid=(nseq, RG_BLOCKS, nt),
        in_specs=[gate_spec, u_spec, hin_spec, cin_spec,
                  pl.BlockSpec((None, CONV_W, bw), lambda b, n, t: (j, 0, n)), vec,
                  w_spec, vec, w_spec, vec, vec],
        out_specs=[pl.BlockSpec((tc, bw), lambda b, n, t: (row(b, n, t), n)), hout_spec, cout_spec],
        out_shape=[jax.ShapeDtypeStruct((m, D_RNN), BF16),
                   jax.ShapeDtypeStruct((nseq, 1, D_RNN), F32),
                   jax.ShapeDtypeStruct((nseq, CONV_W - 1, D_RNN), F32)],
        scratch_shapes=[pltpu.VMEM((SUBLANES, bw), F32)],
        compiler_params=_params(("parallel", "parallel", "arbitrary"), 32),
        name="rglru_core")(gu, gu, h_all, c_all, w["rg_conv_w"], w["rg_conv_b"], w["rg_w_a"], w["rg_b_a"],
                           w["rg_w_x"], w["rg_b_x"], w["rg_lambda"])


def _gla_kernel(q_ref, k_ref, v_ref, g_ref, glr_ref, s0_ref, w2_ref, bgk_ref, gnw_ref,
                o_ref, sout_ref, st_ref, *, nt, nb):
    t = pl.program_id(1)

    @pl.when(t == 0)
    def _():
        for s in range(nb):
            for h in range(GLA_HEADS):
                st_ref[s, h] = s0_ref[s, h].T

    c = CHUNK
    glr = glr_ref[...].reshape(nb * c, LANES).astype(BF16)
    gk_all = jnp.dot(glr, w2_ref[...], preferred_element_type=F32) + bgk_ref[...]
    gk_all = _log_sigmoid(gk_all) * (1.0 / GATE_NORM)
    rows = lax.broadcasted_iota(jnp.int32, (c, c), 0)
    cols = lax.broadcasted_iota(jnp.int32, (c, c), 1)
    tri = rows >= cols
    tri_b = tri.astype(BF16)
    gnw = gnw_ref[...]
    nt_dims = (((1,), (1,)), ((), ()))
    tn_dims = (((0,), (0,)), ((), ()))
    prep = []
    for s in range(nb):
        gk = gk_all[s * c:(s + 1) * c, :]
        bcum = sum(jnp.dot(tri_b, piece, preferred_element_type=F32) for piece in _split3_bf16(gk))
        blast = bcum[c - 1:c, :]
        k = k_ref[s]
        qe = ((q_ref[s] * (HEAD_K ** -0.5)) * jnp.exp(bcum)).astype(BF16)
        ke = (k * jnp.exp(-bcum)).astype(BF16)
        kd = (k * jnp.exp(blast - bcum)).astype(BF16)
        prep.append((qe, ke, kd, jnp.exp(blast)))
    for h in range(GLA_HEADS):
        ks = slice(h * HEAD_K, (h + 1) * HEAD_K)
        vs = slice(h * HEAD_V, (h + 1) * HEAD_V)
        for s in range(nb):
            qe, ke, kd, decay = prep[s]
            v_h = v_ref[s, :, vs].astype(BF16)
            att = lax.dot_general(qe[:, ks], ke[:, ks], nt_dims, preferred_element_type=F32)
            att = jnp.where(tri, att, 0.0).astype(BF16)
            st = st_ref[s, h]
            o = jnp.dot(att, v_h, preferred_element_type=F32)
            o = o + lax.dot_general(qe[:, ks], st.astype(BF16), nt_dims, preferred_element_type=F32)
            st_ref[s, h] = st * decay[:, ks] + lax.dot_general(v_h, kd[:, ks], tn_dims,
                                                               preferred_element_type=F32)
            on = o * lax.rsqrt(jnp.mean(o * o, axis=-1, keepdims=True) + EPS) * gnw
            g_h = g_ref[s, :, vs]
            o_ref[s, :, vs] = (on * (g_h * jax.nn.sigmoid(g_h))).astype(BF16)

    @pl.when(t == nt - 1)
    def _():
        for s in range(nb):
            for h in range(GLA_HEADS):
                sout_ref[s, h] = st_ref[s, h].T


def _gla_aliased_kernel(*refs, **kw):
    _gla_kernel(*refs[:9], *refs[10:], **kw)


def _gla_core(qkvg, glr, nseq, tlen, j, s_all, w, s_stack=None):
    c = CHUNK
    nt = tlen // c
    nb = GLA_SEQS
    n_gla = s_all.shape[0]
    qkvg = qkvg.reshape(nseq, tlen, GLA_MAIN)
    glr = glr.reshape(nseq, tlen, LANES)
    in_specs = [pl.BlockSpec((nb, c, GLA_DQ), lambda b, t: (b, t, 0)),
                pl.BlockSpec((nb, c, GLA_DQ), lambda b, t: (b, t, 1)),
                pl.BlockSpec((nb, c, GLA_DV), lambda b, t: (b, t, 1)),
                pl.BlockSpec((nb, c, GLA_DV), lambda b, t: (b, t, 2)),
                pl.BlockSpec((nb, c, LANES), lambda b, t: (b, t, 0)),
                pl.BlockSpec((None, nb, GLA_HEADS, HEAD_K, HEAD_V), lambda b, t: (j, b, 0, 0, 0),
                             pipeline_mode=pl.Buffered(1)),
                pl.BlockSpec((None, LANES, GLA_DQ), lambda b, t: (j, 0, 0)),
                pl.BlockSpec((None, 1, GLA_DQ), lambda b, t: (j, 0, 0)),
                pl.BlockSpec((None, 1, HEAD_V), lambda b, t: (j, 0, 0))]
    args = [qkvg, qkvg, qkvg, qkvg, glr, s_all, w["gla_w_gk2"], w["gla_b_gk"], w["gla_norm_w"]]
    body, aliases = _gla_kernel, {}
    if s_stack is not None:
        in_specs.append(pl.BlockSpec(memory_space=pl.ANY))
        args.append(s_stack)
        body, aliases = _gla_aliased_kernel, {len(args) - 1: 1}
    og, s_stack = pl.pallas_call(
        functools.partial(body, nt=nt, nb=nb),
        grid=(nseq // nb, nt),
        in_specs=in_specs,
        out_specs=[pl.BlockSpec((nb, c, GLA_DV), lambda b, t: (b, t, 0)),
                   pl.BlockSpec((None, nb, GLA_HEADS, HEAD_K, HEAD_V), lambda b, t: (j, b, 0, 0, 0))],
        out_shape=[jax.ShapeDtypeStruct((nseq, tlen, GLA_DV), BF16),
                   jax.ShapeDtypeStruct((n_gla, nseq, GLA_HEADS, HEAD_K, HEAD_V), F32)],
        scratch_shapes=[pltpu.VMEM((nb, GLA_HEADS, HEAD_V, HEAD_K), F32)],
        input_output_aliases=aliases,
        compiler_params=_params(("parallel", "arbitrary"), 56),
        name="gla_core")(*args)
    return og.reshape(nseq * tlen, GLA_DV), s_stack


def _trunk(x, h_all, conv_all, s_all, w, bf16=None):
    nseq, tlen, d = x.shape
    x = x.reshape(nseq * tlen, d)
    h_all = h_all[:, :, None, :]
    cast = bf16 is None
    made = []
    hs, cs, s_stack = [], [], None
    for layer in range(DEPTH):
        j = layer // 2
        wl = {} if cast else bf16[layer]
        if layer % 2 == 0:
            if cast:
                gu, wl["in"] = _norm_matmul(x, w["norm_mix"], layer, 2 * D_RNN, w_f32=w["rg_w_in"], j=j)
            else:
                gu = _norm_matmul(x, w["norm_mix"], layer, 2 * D_RNN, w_bf16=wl["in"])
            mixed, h_new, c_new = _rglru_core(gu, nseq, tlen, j, h_all, conv_all, w)
            hs.append(h_new[:, 0, :])
            cs.append(c_new)
            w_out = w["rg_w_out"]
        else:
            if cast:
                (qkvg, glr), wl["in"] = _norm_matmul(x, w["norm_mix"], layer, GLA_MAIN, side=GATE_RANK,
                                                     w_f32=w["gla_w_in"], j=j)
            else:
                qkvg, glr = _norm_matmul(x, w["norm_mix"], layer, GLA_MAIN, side=GATE_RANK, w_bf16=wl["in"])
            mixed, s_stack = _gla_core(qkvg, glr, nseq, tlen, j, s_all, w, s_stack)
            w_out = w["gla_w_out"]
        final = layer == DEPTH - 1
        if cast:
            x, wl["out"] = _matmul_res(mixed, x, w_f32=w_out, j=j)
            x, wl["ffn"] = _ffn(x, w["norm_ffn"], layer, w["norm_final"], final,
                                w_f32=(w["ffn_w_up"], w["ffn_w_down"]))
            made.append(wl)
        else:
            x = _matmul_res(mixed, x, w_bf16=wl["out"])
            x = _ffn(x, w["norm_ffn"], layer, w["norm_final"], final, w_bf16=wl["ffn"])
    return (x.reshape(nseq, tlen, d), jnp.stack(hs), jnp.stack(cs), s_stack), made


def _prepare_weights(norm_mix, norm_ffn, norm_final, rg_w_in, rg_conv_w, rg_conv_b, rg_w_a, rg_b_a, rg_w_x, rg_b_x,
                     rg_lambda, rg_w_out, gla_w_in, gla_w_gk2, gla_b_gk, gla_norm_w, gla_w_out, ffn_w_up, ffn_w_down):
    glr_pad = LANES - GATE_RANK
    return {
        "norm_mix": norm_mix[:, None, :],
        "norm_ffn": norm_ffn[:, None, :],
        "norm_final": norm_final[None, :],
        "rg_w_in": rg_w_in,
        "rg_conv_w": rg_conv_w,
        "rg_conv_b": rg_conv_b[:, None, :],
        "rg_w_a": rg_w_a.astype(BF16),
        "rg_b_a": rg_b_a[:, None, :],
        "rg_w_x": rg_w_x.astype(BF16),
        "rg_b_x": rg_b_x[:, None, :],
        "rg_lambda": rg_lambda[:, None, :],
        "rg_w_out": rg_w_out,
        "gla_w_in": gla_w_in,
        "gla_w_gk2": jnp.pad(gla_w_gk2, ((0, 0), (0, glr_pad), (0, 0))).astype(BF16),
        "gla_b_gk": gla_b_gk[:, None, :],
        "gla_norm_w": gla_norm_w[:, None, :],
        "gla_w_out": gla_w_out,
        "ffn_w_up": ffn_w_up,
        "ffn_w_down": ffn_w_down,
    }


def kernel(x_prompt, x_sample, state_rglru_h, state_rglru_conv, state_gla, norm_mix, norm_ffn, norm_final, rg_w_in, rg_conv_w, rg_conv_b, rg_w_a, rg_b_a, rg_w_x, rg_b_x, rg_lambda, rg_w_out, gla_w_in, gla_w_gk2, gla_b_gk, gla_norm_w, gla_w_out, ffn_w_up, ffn_w_down):
    w = _prepare_weights(norm_mix, norm_ffn, norm_final, rg_w_in, rg_conv_w, rg_conv_b, rg_w_a, rg_b_a, rg_w_x,
                         rg_b_x, rg_lambda, rg_w_out, gla_w_in, gla_w_gk2, gla_b_gk, gla_norm_w, gla_w_out,
                         ffn_w_up, ffn_w_down)
    bp = x_prompt.shape[0]
    n_rg = state_rglru_h.shape[0]
    n_gla = state_gla.shape[0]
    h0 = jnp.zeros((n_rg, bp, D_RNN), F32)
    c0 = jnp.zeros((n_rg, bp, CONV_W - 1, D_RNN), F32)
    s0 = jnp.zeros((n_gla, bp, GLA_HEADS, HEAD_K, HEAD_V), F32)
    (y_s, h_s, c_s, s_s), bf16 = _trunk(x_sample, state_rglru_h, state_rglru_conv, state_gla, w)
    (y_p, h_p, c_p, s_p), _ = _trunk(x_prompt, h0, c0, s0, w, bf16)
    return (y_p, y_s, h_p, c_p, s_p, h_s, c_s, s_s)
```

```python
import functools

import jax
import jax.numpy as jnp
from jax import lax
from jax.experimental import pallas as pl
from jax.experimental.pallas import tpu as pltpu

F32 = jnp.float32
BF16 = jnp.bfloat16

D_MODEL = 2048
DEPTH = 4
CHUNK = 64
EPS = 1e-6
D_RNN = D_MODEL
RG_BLOCKS = 8
RG_BW = D_RNN // RG_BLOCKS
CONV_W = 4
RG_C = 8.0
GLA_HEADS = 4
HEAD_K = 256
HEAD_V = 512
GATE_RANK = 16
GATE_NORM = 16.0
GLA_DQ = GLA_HEADS * HEAD_K
GLA_DV = GLA_HEADS * HEAD_V
GLA_MAIN = 2 * GLA_DQ + 2 * GLA_DV
D_FF = 5632

LANES = 128
SUBLANES = 8
NORM_ROWS = 16
NORM_UNROLL = 8
MIB = 1024 * 1024
TINY = 1e-30
GELU_C1 = 0.7978845608028654
GELU_C2 = 0.044715 * GELU_C1

MATMUL_TM = 1024
MATMUL_TN = 1024
VMEM_WINDOW_MIB = 46
FFN_TM = 1024
FFN_TF = 512
FFN_CAST_TF = 256
RGLRU_TC = 512
GLA_SEQS = 4


def _params(dims, vmem_mib):
    return pltpu.CompilerParams(dimension_semantics=dims, vmem_limit_bytes=vmem_mib * MIB)


def _rms(x, w):
    ms = jnp.mean(x * x, axis=-1, keepdims=True)
    return x * lax.rsqrt(ms + EPS) * w


def _log_sigmoid(x):
    return jnp.minimum(x, 0.0) - jnp.log(1.0 + jnp.exp(-jnp.abs(x)))


def _split3_bf16(x):
    hi = x.astype(BF16)
    r1 = x - hi.astype(F32)
    mid = r1.astype(BF16)
    lo = (r1 - mid.astype(F32)).astype(BF16)
    return hi, mid, lo


def _sqrt_nonneg(y):
    return y * lax.rsqrt(jnp.maximum(y, TINY))


def _gelu_tanh(x):
    inner = x * (GELU_C1 + GELU_C2 * (x * x))
    return (0.5 * x) * (1.0 + jnp.tanh(inner))


def _norm_rows(src_ref, nw_ref, dst_ref, rows, dtype):
    nw = nw_ref[...]

    def body(r, carry):
        sl = pl.ds(pl.multiple_of(r * NORM_ROWS, NORM_ROWS), NORM_ROWS)
        dst_ref[sl, :] = _rms(src_ref[sl, :], nw).astype(dtype)
        return carry

    lax.fori_loop(0, rows // NORM_ROWS, body, 0, unroll=NORM_UNROLL)


def _norm_matmul_kernel(x_ref, nw_ref, w_ref, w2_ref, o_ref, o2_ref, hn_ref, *, tm):
    @pl.when(pl.program_id(1) == 0)
    def _():
        _norm_rows(x_ref, nw_ref, hn_ref, tm, BF16)
        if w2_ref is not None:
            o2_ref[...] = jnp.dot(hn_ref[...], w2_ref[...], preferred_element_type=F32)

    o_ref[...] = jnp.dot(hn_ref[...], w_ref[...], preferred_element_type=F32)


def _norm_matmul_plain_kernel(x_ref, nw_ref, w_ref, o_ref, hn_ref, *, tm):
    _norm_matmul_kernel(x_ref, nw_ref, w_ref, None, o_ref, None, hn_ref, tm=tm)


def _norm_matmul_cast_kernel(x_ref, nw_ref, w_ref, o_ref, wo_ref, hn_ref, *, tm):
    wo_ref[...] = w_ref[...].astype(BF16)
    _norm_matmul_kernel(x_ref, nw_ref, wo_ref, None, o_ref, None, hn_ref, tm=tm)


def _norm_matmul_cast2_kernel(x_ref, nw_ref, w_ref, w2_ref, o_ref, o2_ref, wo_ref, w2o_ref, hn_ref, *, tm, side):
    wo_ref[...] = w_ref[...].astype(BF16)

    @pl.when(pl.program_id(1) == 0)
    def _():
        lane = lax.broadcasted_iota(jnp.int32, w2_ref.shape, 1)
        w2o_ref[...] = jnp.where(lane < side, w2_ref[...], 0.0).astype(BF16)

    _norm_matmul_kernel(x_ref, nw_ref, wo_ref, w2o_ref, o_ref, o2_ref, hn_ref, tm=tm)


def _norm_matmul(x, nws, layer, n, side=0, w_bf16=None, w_f32=None, j=None):
    m, d = x.shape
    tm = min(m, MATMUL_TM)
    tn = MATMUL_TN
    if w_bf16 is not None:
        def windows_mib(c):
            return (2 * tm * d * 4 + tm * d * 2 + 2 * d * c * 2 + 2 * tm * c * 4) / MIB
        tn = max(c for c in range(MATMUL_TN, n + 1, MATMUL_TN // 2) if n % c == 0 and windows_mib(c) <= VMEM_WINDOW_MIB)
    grid = (m // tm, n // tn)
    x_spec = pl.BlockSpec((tm, d), lambda i, c: (i, 0))
    nw_spec = pl.BlockSpec((None, 1, d), lambda i, c: (layer, 0, 0))
    w_spec = pl.BlockSpec((d, tn), lambda i, c: (0, c))
    w2_spec = pl.BlockSpec((d, LANES), lambda i, c: (0, 0))
    o_spec = pl.BlockSpec((tm, tn), lambda i, c: (i, c))
    o2_spec = pl.BlockSpec((tm, LANES), lambda i, c: (i, 0))
    o_shape = jax.ShapeDtypeStruct((m, n), F32)
    o2_shape = jax.ShapeDtypeStruct((m, LANES), F32)
    scratch = [pltpu.VMEM((tm, d), BF16)]
    cp = _params(("parallel", "arbitrary"), 56)
    if w_bf16 is not None:
        w, w2 = w_bf16
        if not side:
            return pl.pallas_call(
                functools.partial(_norm_matmul_plain_kernel, tm=tm), grid=grid,
                in_specs=[x_spec, nw_spec, w_spec], out_specs=o_spec, out_shape=o_shape,
                scratch_shapes=scratch, compiler_params=cp, name="norm_matmul")(x, nws, w)
        return pl.pallas_call(
            functools.partial(_norm_matmul_kernel, tm=tm), grid=grid,
            in_specs=[x_spec, nw_spec, w_spec, w2_spec], out_specs=[o_spec, o2_spec],
            out_shape=[o_shape, o2_shape], scratch_shapes=scratch, compiler_params=cp,
            name="norm_matmul2")(x, nws, w, w2)
    assert m == tm, "the casting variant rewrites the bf16 weights once per row block"
    wf_spec = pl.BlockSpec((None, d, tn), lambda i, c: (j, 0, c))
    w_shape = jax.ShapeDtypeStruct((d, n), BF16)
    if not side:
        out, w = pl.pallas_call(
            functools.partial(_norm_matmul_cast_kernel, tm=tm), grid=grid,
            in_specs=[x_spec, nw_spec, wf_spec], out_specs=[o_spec, w_spec], out_shape=[o_shape, w_shape],
            scratch_shapes=scratch, compiler_params=cp, name="norm_matmul_cast")(x, nws, w_f32)
        return out, (w, None)
    out, out2, w, w2 = pl.pallas_call(
        functools.partial(_norm_matmul_cast2_kernel, tm=tm, side=side), grid=grid,
        in_specs=[x_spec, nw_spec, wf_spec, pl.BlockSpec((None, d, LANES), lambda i, c: (j, 0, n // LANES))],
        out_specs=[o_spec, o2_spec, w_spec, w2_spec],
        out_shape=[o_shape, o2_shape, w_shape, jax.ShapeDtypeStruct((d, LANES), BF16)],
        scratch_shapes=scratch, compiler_params=cp, name="norm_matmul2_cast")(x, nws, w_f32, w_f32)
    return (out, out2), (w, w2)


def _matmul_res_kernel(a_ref, w_ref, x_ref, o_ref):
    o_ref[...] = x_ref[...] + jnp.dot(a_ref[...], w_ref[...], preferred_element_type=F32)


def _matmul_res_cast_kernel(a_ref, w_ref, x_ref, o_ref, wo_ref):
    wo_ref[...] = w_ref[...].astype(BF16)
    _matmul_res_kernel(a_ref, wo_ref, x_ref, o_ref)


def _matmul_res(a, x, w_bf16=None, w_f32=None, j=None):
    m, k = a.shape
    n = x.shape[1]
    tm = min(m, MATMUL_TM)
    tn = n if w_bf16 is not None else MATMUL_TN
    grid = (m // tm, n // tn)
    a_spec = pl.BlockSpec((tm, k), lambda i, c: (i, 0))
    w_spec = pl.BlockSpec((k, tn), lambda i, c: (0, c))
    xo_spec = pl.BlockSpec((tm, tn), lambda i, c: (i, c))
    o_shape = jax.ShapeDtypeStruct((m, n), F32)
    cp = _params(("parallel", "arbitrary"), 56)
    if w_bf16 is not None:
        w_res_spec = pl.BlockSpec((k, n), lambda i, c: (0, 0), pipeline_mode=pl.Buffered(1))
        return pl.pallas_call(
            _matmul_res_kernel, grid=grid, in_specs=[a_spec, w_res_spec, xo_spec], out_specs=xo_spec,
            out_shape=o_shape, compiler_params=cp, name="matmul_res")(a, w_bf16, x)
    assert m == tm, "the casting variant rewrites the bf16 weight once per row block"
    return pl.pallas_call(
        _matmul_res_cast_kernel, grid=grid,
        in_specs=[a_spec, pl.BlockSpec((None, k, tn), lambda i, c: (j, 0, c)), xo_spec],
        out_specs=[xo_spec, w_spec], out_shape=[o_shape, jax.ShapeDtypeStruct((k, n), BF16)],
        compiler_params=cp, name="matmul_res_cast")(a, w_f32, x)


def _ffn_kernel(x_ref, nw_ref, wg_ref, wu_ref, wd_ref, fw_ref, o_ref, hn_ref, *, tm, nf, final):
    f = pl.program_id(1)

    @pl.when(f == 0)
    def _():
        _norm_rows(x_ref, nw_ref, hn_ref, tm, BF16)
        o_ref[...] = x_ref[...]

    hn = hn_ref[...]
    gt = jnp.dot(hn, wg_ref[...], preferred_element_type=F32)
    up = jnp.dot(hn, wu_ref[...], preferred_element_type=F32)
    act = ((gt * jax.nn.sigmoid(gt)) * up).astype(BF16)
    o_ref[...] += jnp.dot(act, wd_ref[...], preferred_element_type=F32)

    if final:
        @pl.when(f == nf - 1)
        def _():
            _norm_rows(o_ref, fw_ref, o_ref, tm, F32)


def _ffn_cast_kernel(x_ref, nw_ref, wg_ref, wu_ref, wd_ref, fw_ref, o_ref, wgo_ref, wuo_ref, wdo_ref, hn_ref, **kw):
    wgo_ref[...] = wg_ref[...].astype(BF16)
    wuo_ref[...] = wu_ref[...].astype(BF16)
    wdo_ref[...] = wd_ref[...].astype(BF16)
    _ffn_kernel(x_ref, nw_ref, wgo_ref, wuo_ref, wdo_ref, fw_ref, o_ref, hn_ref, **kw)


def _ffn(x, nws, layer, fw, final, w_bf16=None, w_f32=None):
    m, d = x.shape
    tm = min(m, FFN_TM)
    tf = FFN_TF if w_bf16 is not None else FFN_CAST_TF
    nf = D_FF // tf
    kw = dict(tm=tm, nf=nf, final=final)
    x_spec = pl.BlockSpec((tm, d), lambda i, f: (i, 0), pipeline_mode=pl.Buffered(1))
    nw_spec = pl.BlockSpec((None, 1, d), lambda i, f: (layer, 0, 0))
    fw_spec = pl.BlockSpec((1, d), lambda i, f: (0, 0))
    o_spec = pl.BlockSpec((tm, d), lambda i, f: (i, 0))
    o_shape = jax.ShapeDtypeStruct((m, d), F32)
    scratch = [pltpu.VMEM((tm, d), BF16)]
    cp = _params(("parallel", "arbitrary"), 56)
    up_spec = pl.BlockSpec((d, tf), lambda i, f: (0, f))
    down_spec = pl.BlockSpec((tf, d), lambda i, f: (f, 0))
    if w_bf16 is not None:
        return pl.pallas_call(
            functools.partial(_ffn_kernel, **kw), grid=(m // tm, nf),
            in_specs=[x_spec, nw_spec, up_spec, up_spec, down_spec, fw_spec],
            out_specs=o_spec, out_shape=o_shape, scratch_shapes=scratch,
            compiler_params=cp, name="ffn")(x, nws, *w_bf16, fw)
    assert m == tm, "the casting variant rewrites the bf16 weights once per row block"
    w_ups, w_downs = w_f32
    out, wg, wu, wd = pl.pallas_call(
        functools.partial(_ffn_cast_kernel, **kw), grid=(1, nf),
        in_specs=[x_spec, nw_spec,
                  pl.BlockSpec((None, d, tf), lambda i, f: (layer, 0, f)),
                  pl.BlockSpec((None, d, tf), lambda i, f: (layer, 0, f + nf)),
                  pl.BlockSpec((None, tf, d), lambda i, f: (layer, f, 0)),
                  fw_spec],
        out_specs=[o_spec, up_spec, up_spec, down_spec],
        out_shape=[o_shape, jax.ShapeDtypeStruct((d, D_FF), BF16), jax.ShapeDtypeStruct((d, D_FF), BF16),
                   jax.ShapeDtypeStruct((D_FF, d), BF16)],
        scratch_shapes=scratch, compiler_params=cp, name="ffn_cast")(x, nws, w_ups, w_ups, w_downs, fw)
    return out, (wg, wu, wd)


def _rglru_kernel(gate_ref, u_ref, h0_ref, c0_ref, cw_ref, cb_ref, wa_ref, ba_ref, wx_ref, bx_ref, lam_ref,
                  hg_ref, hout_ref, cout_ref, uext_ref, *, tc, nt):
    t = pl.program_id(2)
    pad = SUBLANES

    @pl.when(t == 0)
    def _():
        uext_ref[...] = jnp.zeros((pad, RG_BW), F32)
        uext_ref[pad - (CONV_W - 1):pad, :] = c0_ref[...]
        hout_ref[...] = h0_ref[...]

    u = u_ref[...]
    ext = jnp.concatenate([uext_ref[...], u], axis=0)
    cw = cw_ref[...]
    acc = pltpu.roll(ext, 3, 0)[pad:, :] * cw[0:1, :]
    acc = acc + pltpu.roll(ext, 2, 0)[pad:, :] * cw[1:2, :]
    acc = acc + pltpu.roll(ext, 1, 0)[pad:, :] * cw[2:3, :]
    acc = acc + u * cw[3:4, :]
    uc = cb_ref[...] + acc
    ucb = uc.astype(BF16)
    r = jax.nn.sigmoid(jnp.dot(ucb, wa_ref[...], preferred_element_type=F32) + ba_ref[...])
    i = jax.nn.sigmoid(jnp.dot(ucb, wx_ref[...], preferred_element_type=F32) + bx_ref[...])
    log_a = (RG_C * r) * _log_sigmoid(lam_ref[...])
    a = jnp.exp(log_a)
    b = _sqrt_nonneg(1.0 - a * a) * (i * uc)

    groups = tc // SUBLANES
    a3 = a.reshape(groups, SUBLANES, RG_BW)
    b3 = b.reshape(groups, SUBLANES, RG_BW)
    row = lax.broadcasted_iota(jnp.int32, (groups, SUBLANES, RG_BW), 1)
    for s in (1, 2, 4):
        keep = row >= s
        a_prev = pltpu.roll(a3, s, 1)
        b_prev = pltpu.roll(b3, s, 1)
        b3 = jnp.where(keep, b3 + a3 * b_prev, b3)
        a3 = jnp.where(keep, a3 * a_prev, a3)
    h_prev = jnp.broadcast_to(hout_ref[...], (SUBLANES, RG_BW))
    hs = []
    for g in range(groups):
        h_g = a3[g] * h_prev + b3[g]
        hs.append(h_g)
        h_prev = jnp.broadcast_to(h_g[SUBLANES - 1:SUBLANES, :], (SUBLANES, RG_BW))
    h = jnp.concatenate(hs, axis=0)
    hout_ref[...] = hs[-1][SUBLANES - 1:SUBLANES, :]
    hg_ref[...] = (h * _gelu_tanh(gate_ref[...])).astype(BF16)

    uext_ref[...] = u[tc - pad:, :]

    @pl.when(t == nt - 1)
    def _():
        cout_ref[...] = uext_ref[pad - (CONV_W - 1):pad, :]


def _rglru_core(gu, nseq, tlen, j, h_all, c_all, w):
    m = nseq * tlen
    tc = min(tlen, RGLRU_TC)
    nt = tlen // tc
    bw = RG_BW
    row = lambda b, n, t: b * nt + t
    vec = pl.BlockSpec((None, 1, bw), lambda b, n, t: (j, 0, n))
    gate_spec = pl.BlockSpec((tc, bw), lambda b, n, t: (row(b, n, t), n))
    u_spec = pl.BlockSpec((tc, bw), lambda b, n, t: (row(b, n, t), RG_BLOCKS + n))
    hin_spec = pl.BlockSpec((None, None, 1, bw), lambda b, n, t: (j, b, 0, n))
    cin_spec = pl.BlockSpec((None, None, CONV_W - 1, bw), lambda b, n, t: (j, b, 0, n))
    hout_spec = pl.BlockSpec((None, 1, bw), lambda b, n, t: (b, 0, n))
    cout_spec = pl.BlockSpec((None, CONV_W - 1, bw), lambda b, n, t: (b, 0, n))
    w_spec = pl.BlockSpec((None, None, bw, bw), lambda b, n, t: (j, n, 0, 0))
    return pl.pallas_call(
        functools.partial(_rglru_kernel, tc=tc, nt=nt),
        grid=(nseq, RG_BLOCKS, nt),
        in_specs=[gate_spec, u_spec, hin_spec, cin_spec,
                  pl.BlockSpec((None, CONV_W, bw), lambda b, n, t: (j, 0, n)), vec,
                  w_spec, vec, w_spec, vec, vec],
        out_specs=[pl.BlockSpec((tc, bw), lambda b, n, t: (row(b, n, t), n)), hout_spec, cout_spec],
        out_shape=[jax.ShapeDtypeStruct((m, D_RNN), BF16),
                   jax.ShapeDtypeStruct((nseq, 1, D_RNN), F32),
                   jax.ShapeDtypeStruct((nseq, CONV_W - 1, D_RNN), F32)],
        scratch_shapes=[pltpu.VMEM((SUBLANES, bw), F32)],
        compiler_params=_params(("parallel", "parallel", "arbitrary"), 32),
        name="rglru_core")(gu, gu, h_all, c_all, w["rg_conv_w"], w["rg_conv_b"], w["rg_w_a"], w["rg_b_a"],
                           w["rg_w_x"], w["rg_b_x"], w["rg_lambda"])


def _gla_kernel(q_ref, k_ref, v_ref, g_ref, glr_ref, s0_ref, w2_ref, bgk_ref, gnw_ref,
                o_ref, sout_ref, st_ref, *, nt, nb):
    t = pl.program_id(1)

    @pl.when(t == 0)
    def _():
        for s in range(nb):
            for h in range(GLA_HEADS):
                st_ref[s, h] = s0_ref[s, h].T

    c = CHUNK
    glr = glr_ref[...].reshape(nb * c, LANES).astype(BF16)
    gk_all = jnp.dot(glr, w2_ref[...], preferred_element_type=F32) + bgk_ref[...]
    gk_all = _log_sigmoid(gk_all) * (1.0 / GATE_NORM)
    rows = lax.broadcasted_iota(jnp.int32, (c, c), 0)
    cols = lax.broadcasted_iota(jnp.int32, (c, c), 1)
    tri = rows >= cols
    tri_b = tri.astype(BF16)
    gnw = gnw_ref[...]
    nt_dims = (((1,), (1,)), ((), ()))
    tn_dims = (((0,), (0,)), ((), ()))
    prep = []
    for s in range(nb):
        gk = gk_all[s * c:(s + 1) * c, :]
        bcum = sum(jnp.dot(tri_b, piece, preferred_element_type=F32) for piece in _split3_bf16(gk))
        blast = bcum[c - 1:c, :]
        k = k_ref[s]
        qe = ((q_ref[s] * (HEAD_K ** -0.5)) * jnp.exp(bcum)).astype(BF16)
        ke = (k * jnp.exp(-bcum)).astype(BF16)
        kd = (k * jnp.exp(blast - bcum)).astype(BF16)
        prep.append((qe, ke, kd, jnp.exp(blast)))
    for h in range(GLA_HEADS):
        ks = slice(h * HEAD_K, (h + 1) * HEAD_K)
        vs = slice(h * HEAD_V, (h + 1) * HEAD_V)
        for s in range(nb):
            qe, ke, kd, decay = prep[s]
            v_h = v_ref[s, :, vs].astype(BF16)
            att = lax.dot_general(qe[:, ks], ke[:, ks], nt_dims, preferred_element_type=F32)
            att = jnp.where(tri, att, 0.0).astype(BF16)
            st = st_ref[s, h]
            o = jnp.dot(att, v_h, preferred_element_type=F32)
            o = o + lax.dot_general(qe[:, ks], st.astype(BF16), nt_dims, preferred_element_type=F32)
            st_ref[s, h] = st * decay[:, ks] + lax.dot_general(v_h, kd[:, ks], tn_dims,
                                                               preferred_element_type=F32)
            on = o * lax.rsqrt(jnp.mean(o * o, axis=-1, keepdims=True) + EPS) * gnw
            g_h = g_ref[s, :, vs]
            o_ref[s, :, vs] = (on * (g_h * jax.nn.sigmoid(g_h))).astype(BF16)

    @pl.when(t == nt - 1)
    def _():
        for s in range(nb):
            for h in range(GLA_HEADS):
                sout_ref[s, h] = st_ref[s, h].T


def _gla_aliased_kernel(*refs, **kw):
    _gla_kernel(*refs[:9], *refs[10:], **kw)


def _gla_core(qkvg, glr, nseq, tlen, j, s_all, w, s_stack=None):
    c = CHUNK
    nt = tlen // c
    nb = GLA_SEQS
    n_gla = s_all.shape[0]
    qkvg = qkvg.reshape(nseq, tlen, GLA_MAIN)
    glr = glr.reshape(nseq, tlen, LANES)
    in_specs = [pl.BlockSpec((nb, c, GLA_DQ), lambda b, t: (b, t, 0)),
                pl.BlockSpec((nb, c, GLA_DQ), lambda b, t: (b, t, 1)),
                pl.BlockSpec((nb, c, GLA_DV), lambda b, t: (b, t, 1)),
                pl.BlockSpec((nb, c, GLA_DV), lambda b, t: (b, t, 2)),
                pl.BlockSpec((nb, c, LANES), lambda b, t: (b, t, 0)),
                pl.BlockSpec((None, nb, GLA_HEADS, HEAD_K, HEAD_V), lambda b, t: (j, b, 0, 0, 0),
                             pipeline_mode=pl.Buffered(1)),
                pl.BlockSpec((None, LANES, GLA_DQ), lambda b, t: (j, 0, 0)),
                pl.BlockSpec((None, 1, GLA_DQ), lambda b, t: (j, 0, 0)),
                pl.BlockSpec((None, 1, HEAD_V), lambda b, t: (j, 0, 0))]
    args = [qkvg, qkvg, qkvg, qkvg, glr, s_all, w["gla_w_gk2"], w["gla_b_gk"], w["gla_norm_w"]]
    body, aliases = _gla_kernel, {}
    if s_stack is not None:
        in_specs.append(pl.BlockSpec(memory_space=pl.ANY))
        args.append(s_stack)
        body, aliases = _gla_aliased_kernel, {len(args) - 1: 1}
    og, s_stack = pl.pallas_call(
        functools.partial(body, nt=nt, nb=nb),
        grid=(nseq // nb, nt),
        in_specs=in_specs,
        out_specs=[pl.BlockSpec((nb, c, GLA_DV), lambda b, t: (b, t, 0)),
                   pl.BlockSpec((None, nb, GLA_HEADS, HEAD_K, HEAD_V), lambda b, t: (j, b, 0, 0, 0))],
        out_shape=[jax.ShapeDtypeStruct((nseq, tlen, GLA_DV), BF16),
                   jax.ShapeDtypeStruct((n_gla, nseq, GLA_HEADS, HEAD_K, HEAD_V), F32)],
        scratch_shapes=[pltpu.VMEM((nb, GLA_HEADS, HEAD_V, HEAD_K), F32)],
        input_output_aliases=aliases,
        compiler_params=_params(("parallel", "arbitrary"), 56),
        name="gla_core")(*args)
    return og.reshape(nseq * tlen, GLA_DV), s_stack


def _trunk(x, h_all, conv_all, s_all, w, bf16=None):
    nseq, tlen, d = x.shape
    x = x.reshape(nseq * tlen, d)
    h_all = h_all[:, :, None, :]
    cast = bf16 is None
    made = []
    hs, cs, s_stack = [], [], None
    for layer in range(DEPTH):
        j = layer // 2
        wl = {} if cast else bf16[layer]
        if layer % 2 == 0:
            if cast:
                gu, wl["in"] = _norm_matmul(x, w["norm_mix"], layer, 2 * D_RNN, w_f32=w["rg_w_in"], j=j)
            else:
                gu = _norm_matmul(x, w["norm_mix"], layer, 2 * D_RNN, w_bf16=wl["in"])
            mixed, h_new, c_new = _rglru_core(gu, nseq, tlen, j, h_all, conv_all, w)
            hs.append(h_new[:, 0, :])
            cs.append(c_new)
            w_out = w["rg_w_out"]
        else:
            if cast:
                (qkvg, glr), wl["in"] = _norm_matmul(x, w["norm_mix"], layer, GLA_MAIN, side=GATE_RANK,
                                                     w_f32=w["gla_w_in"], j=j)
            else:
                qkvg, glr = _norm_matmul(x, w["norm_mix"], layer, GLA_MAIN, side=GATE_RANK, w_bf16=wl["in"])
            mixed, s_stack = _gla_core(qkvg, glr, nseq, tlen, j, s_all, w, s_stack)
            w_out = w["gla_w_out"]
        final = layer == DEPTH - 1
        if cast:
            x, wl["out"] = _matmul_res(mixed, x, w_f32=w_out, j=j)
            x, wl["ffn"] = _ffn(x, w["norm_ffn"], layer, w["norm_final"], final,
                                w_f32=(w["ffn_w_up"], w["ffn_w_down"]))
            made.append(wl)
        else:
            x = _matmul_res(mixed, x, w_bf16=wl["out"])
            x = _ffn(x, w["norm_ffn"], layer, w["norm_final"], final, w_bf16=wl["ffn"])
    return (x.reshape(nseq, tlen, d), jnp.stack(hs), jnp.stack(cs), s_stack), made


def _prepare_weights(norm_mix, norm_ffn, norm_final, rg_w_in, rg_conv_w, rg_conv_b, rg_w_a, rg_b_a, rg_w_x, rg_b_x,
                     rg_lambda, rg_w_out, gla_w_in, gla_w_gk2, gla_b_gk, gla_norm_w, gla_w_out, ffn_w_up, ffn_w_down):
    glr_pad = LANES - GATE_RANK
    return {
        "norm_mix": norm_mix[:, None, :],
        "norm_ffn": norm_ffn[:, None, :],
        "norm_final": norm_final[None, :],
        "rg_w_in": rg_w_in,
        "rg_conv_w": rg_conv_w,
        "rg_conv_b": rg_conv_b[:, None, :],
        "rg_w_a": rg_w_a.astype(BF16),
        "rg_b_a": rg_b_a[:, None, :],
        "rg_w_x": rg_w_x.astype(BF16),
        "rg_b_x": rg_b_x[:, None, :],
        "rg_lambda": rg_lambda[:, None, :],
        "rg_w_out": rg_w_out,
        "gla_w_in": gla_w_in,
        "gla_w_gk2": jnp.pad(gla_w_gk2, ((0, 0), (0, glr_pad), (0, 0))).astype(BF16),
        "gla_b_gk": gla_b_gk[:, None, :],
        "gla_norm_w": gla_norm_w[:, None, :],
        "gla_w_out": gla_w_out,
        "ffn_w_up": ffn_w_up,
        "ffn_w_down": ffn_w_down,
    }


def kernel(x_prompt, x_sample, state_rglru_h, state_rglru_conv, state_gla, norm_mix, norm_ffn, norm_final, rg_w_in, rg_conv_w, rg_conv_b, rg_w_a, rg_b_a, rg_w_x, rg_b_x, rg_lambda, rg_w_out, gla_w_in, gla_w_gk2, gla_b_gk, gla_norm_w, gla_w_out, ffn_w_up, ffn_w_down):
    w = _prepare_weights(norm_mix, norm_ffn, norm_final, rg_w_in, rg_conv_w, rg_conv_b, rg_w_a, rg_b_a, rg_w_x,
                         rg_b_x, rg_lambda, rg_w_out, gla_w_in, gla_w_gk2, gla_b_gk, gla_norm_w, gla_w_out,
                         ffn_w_up, ffn_w_down)
    bp = x_prompt.shape[0]
    n_rg = state_rglru_h.shape[0]
    n_gla = state_gla.shape[0]
    h0 = jnp.zeros((n_rg, bp, D_RNN), F32)
    c0 = jnp.zeros((n_rg, bp, CONV_W - 1, D_RNN), F32)
    s0 = jnp.zeros((n_gla, bp, GLA_HEADS, HEAD_K, HEAD_V), F32)
    (y_s, h_s, c_s, s_s), bf16 = _trunk(x_sample, state_rglru_h, state_rglru_conv, state_gla, w)
    (y_p, h_p, c_p, s_p), _ = _trunk(x_prompt, h0, c0, s0, w, bf16)
    return (y_p, y_s, h_p, c_p, s_p, h_s, c_s, s_s)
```

```python
import functools

import jax
import jax.numpy as jnp
from jax import lax
from jax.experimental import pallas as pl
from jax.experimental.pallas import tpu as pltpu

F32 = jnp.float32
BF16 = jnp.bfloat16

D_MODEL = 2048
DEPTH = 4
CHUNK = 64
EPS = 1e-6
D_RNN = D_MODEL
RG_BLOCKS = 8
RG_BW = D_RNN // RG_BLOCKS
CONV_W = 4
RG_C = 8.0
GLA_HEADS = 4
HEAD_K = 256
HEAD_V = 512
GATE_RANK = 16
GATE_NORM = 16.0
GLA_DQ = GLA_HEADS * HEAD_K
GLA_DV = GLA_HEADS * HEAD_V
GLA_MAIN = 2 * GLA_DQ + 2 * GLA_DV
D_FF = 5632

LANES = 128
SUBLANES = 8
NORM_ROWS = 16
NORM_UNROLL = 8
MIB = 1024 * 1024
TINY = 1e-30
GELU_C1 = 0.7978845608028654
GELU_C2 = 0.044715 * GELU_C1

MATMUL_TM = 1024
MATMUL_TN = 1024
VMEM_WINDOW_MIB = 46
FFN_TM = 1024
FFN_TF = 512
FFN_CAST_TF = 256
RGLRU_TC = 512
GLA_SEQS = 4


def _params(dims, vmem_mib):
    return pltpu.CompilerParams(dimension_semantics=dims, vmem_limit_bytes=vmem_mib * MIB)


def _rms(x, w):
    ms = jnp.mean(x * x, axis=-1, keepdims=True)
    return x * lax.rsqrt(ms + EPS) * w


def _log_sigmoid(x):
    return jnp.minimum(x, 0.0) - jnp.log(1.0 + jnp.exp(-jnp.abs(x)))


def _split3_bf16(x):
    hi = x.astype(BF16)
    r1 = x - hi.astype(F32)
    mid = r1.astype(BF16)
    lo = (r1 - mid.astype(F32)).astype(BF16)
    return hi, mid, lo


def _sqrt_nonneg(y):
    return y * lax.rsqrt(jnp.maximum(y, TINY))


def _gelu_tanh(x):
    inner = x * (GELU_C1 + GELU_C2 * (x * x))
    return (0.5 * x) * (1.0 + jnp.tanh(inner))


def _norm_rows(src_ref, nw_ref, dst_ref, rows, dtype):
    nw = nw_ref[...]

    def body(r, carry):
        sl = pl.ds(pl.multiple_of(r * NORM_ROWS, NORM_ROWS), NORM_ROWS)
        dst_ref[sl, :] = _rms(src_ref[sl, :], nw).astype(dtype)
        return carry

    lax.fori_loop(0, rows // NORM_ROWS, body, 0, unroll=NORM_UNROLL)


def _norm_matmul_kernel(x_ref, nw_ref, w_ref, w2_ref, o_ref, o2_ref, hn_ref, *, tm):
    @pl.when(pl.program_id(1) == 0)
    def _():
        _norm_rows(x_ref, nw_ref, hn_ref, tm, BF16)
        if w2_ref is not None:
            o2_ref[...] = jnp.dot(hn_ref[...], w2_ref[...], preferred_element_type=F32)

    o_ref[...] = jnp.dot(hn_ref[...], w_ref[...], preferred_element_type=F32)


def _norm_matmul_plain_kernel(x_ref, nw_ref, w_ref, o_ref, hn_ref, *, tm):
    _norm_matmul_kernel(x_ref, nw_ref, w_ref, None, o_ref, None, hn_ref, tm=tm)


def _norm_matmul_cast_kernel(x_ref, nw_ref, w_ref, o_ref, wo_ref, hn_ref, *, tm):
    wo_ref[...] = w_ref[...].astype(BF16)
    _norm_matmul_kernel(x_ref, nw_ref, wo_ref, None, o_ref, None, hn_ref, tm=tm)


def _norm_matmul_cast2_kernel(x_ref, nw_ref, w_ref, w2_ref, o_ref, o2_ref, wo_ref, w2o_ref, hn_ref, *, tm, side):
    wo_ref[...] = w_ref[...].astype(BF16)

    @pl.when(pl.program_id(1) == 0)
    def _():
        lane = lax.broadcasted_iota(jnp.int32, w2_ref.shape, 1)
        w2o_ref[...] = jnp.where(lane < side, w2_ref[...], 0.0).astype(BF16)

    _norm_matmul_kernel(x_ref, nw_ref, wo_ref, w2o_ref, o_ref, o2_ref, hn_ref, tm=tm)


def _norm_matmul(x, nws, layer, n, side=0, w_bf16=None, w_f32=None, j=None):
    m, d = x.shape
    tm = min(m, MATMUL_TM)
    tn = MATMUL_TN
    w_buffers = 2
    if w_bf16 is not None:
        def windows_mib(rows, cols, wbuf):
            return (2 * rows * d * 4 + rows * d * 2 + wbuf * d * cols * 2 + 2 * rows * cols * 4) / MIB
        if m > tm and windows_mib(tm // 2, n, 1) <= VMEM_WINDOW_MIB:
            tm, tn, w_buffers = tm // 2, n, 1
        else:
            tn = max(c for c in range(MATMUL_TN, n + 1, MATMUL_TN // 2)
                     if n % c == 0 and windows_mib(tm, c, 2) <= VMEM_WINDOW_MIB)
    grid = (m // tm, n // tn)
    x_spec = pl.BlockSpec((tm, d), lambda i, c: (i, 0))
    nw_spec = pl.BlockSpec((None, 1, d), lambda i, c: (layer, 0, 0))
    w_spec = pl.BlockSpec((d, tn), lambda i, c: (0, c), pipeline_mode=pl.Buffered(w_buffers))
    w2_spec = pl.BlockSpec((d, LANES), lambda i, c: (0, 0))
    o_spec = pl.BlockSpec((tm, tn), lambda i, c: (i, c))
    o2_spec = pl.BlockSpec((tm, LANES), lambda i, c: (i, 0))
    o_shape = jax.ShapeDtypeStruct((m, n), F32)
    o2_shape = jax.ShapeDtypeStruct((m, LANES), F32)
    scratch = [pltpu.VMEM((tm, d), BF16)]
    cp = _params(("parallel", "arbitrary"), 56)
    if w_bf16 is not None:
        w, w2 = w_bf16
        if not side:
            return pl.pallas_call(
                functools.partial(_norm_matmul_plain_kernel, tm=tm), grid=grid,
                in_specs=[x_spec, nw_spec, w_spec], out_specs=o_spec, out_shape=o_shape,
                scratch_shapes=scratch, compiler_params=cp, name="norm_matmul")(x, nws, w)
        return pl.pallas_call(
            functools.partial(_norm_matmul_kernel, tm=tm), grid=grid,
            in_specs=[x_spec, nw_spec, w_spec, w2_spec], out_specs=[o_spec, o2_spec],
            out_shape=[o_shape, o2_shape], scratch_shapes=scratch, compiler_params=cp,
            name="norm_matmul2")(x, nws, w, w2)
    assert m == tm, "the casting variant rewrites the bf16 weights once per row block"
    wf_spec = pl.BlockSpec((None, d, tn), lambda i, c: (j, 0, c))
    w_shape = jax.ShapeDtypeStruct((d, n), BF16)
    if not side:
        out, w = pl.pallas_call(
            functools.partial(_norm_matmul_cast_kernel, tm=tm), grid=grid,
            in_specs=[x_spec, nw_spec, wf_spec], out_specs=[o_spec, w_spec], out_shape=[o_shape, w_shape],
            scratch_shapes=scratch, compiler_params=cp, name="norm_matmul_cast")(x, nws, w_f32)
        return out, (w, None)
    out, out2, w, w2 = pl.pallas_call(
        functools.partial(_norm_matmul_cast2_kernel, tm=tm, side=side), grid=grid,
        in_specs=[x_spec, nw_spec, wf_spec, pl.BlockSpec((None, d, LANES), lambda i, c: (j, 0, n // LANES))],
        out_specs=[o_spec, o2_spec, w_spec, w2_spec],
        out_shape=[o_shape, o2_shape, w_shape, jax.ShapeDtypeStruct((d, LANES), BF16)],
        scratch_shapes=scratch, compiler_params=cp, name="norm_matmul2_cast")(x, nws, w_f32, w_f32)
    return (out, out2), (w, w2)


def _matmul_res_kernel(a_ref, w_ref, x_ref, o_ref):
    o_ref[...] = x_ref[...] + jnp.dot(a_ref[...], w_ref[...], preferred_element_type=F32)


def _matmul_res_cast_kernel(a_ref, w_ref, x_ref, o_ref, wo_ref):
    wo_ref[...] = w_ref[...].astype(BF16)
    _matmul_res_kernel(a_ref, wo_ref, x_ref, o_ref)


def _matmul_res(a, x, w_bf16=None, w_f32=None, j=None):
    m, k = a.shape
    n = x.shape[1]
    tm = min(m, MATMUL_TM)
    tn = n if w_bf16 is not None else MATMUL_TN
    grid = (m // tm, n // tn)
    a_spec = pl.BlockSpec((tm, k), lambda i, c: (i, 0))
    w_spec = pl.BlockSpec((k, tn), lambda i, c: (0, c))
    xo_spec = pl.BlockSpec((tm, tn), lambda i, c: (i, c))
    o_shape = jax.ShapeDtypeStruct((m, n), F32)
    cp = _params(("parallel", "arbitrary"), 56)
    if w_bf16 is not None:
        w_res_spec = pl.BlockSpec((k, n), lambda i, c: (0, 0), pipeline_mode=pl.Buffered(1))
        return pl.pallas_call(
            _matmul_res_kernel, grid=grid, in_specs=[a_spec, w_res_spec, xo_spec], out_specs=xo_spec,
            out_shape=o_shape, compiler_params=cp, name="matmul_res")(a, w_bf16, x)
    assert m == tm, "the casting variant rewrites the bf16 weight once per row block"
    return pl.pallas_call(
        _matmul_res_cast_kernel, grid=grid,
        in_specs=[a_spec, pl.BlockSpec((None, k, tn), lambda i, c: (j, 0, c)), xo_spec],
        out_specs=[xo_spec, w_spec], out_shape=[o_shape, jax.ShapeDtypeStruct((k, n), BF16)],
        compiler_params=cp, name="matmul_res_cast")(a, w_f32, x)


def _ffn_kernel(x_ref, nw_ref, wg_ref, wu_ref, wd_ref, fw_ref, o_ref, hn_ref, *, tm, nf, final):
    f = pl.program_id(1)

    @pl.when(f == 0)
    def _():
        _norm_rows(x_ref, nw_ref, hn_ref, tm, BF16)
        o_ref[...] = x_ref[...]

    hn = hn_ref[...]
    gt = jnp.dot(hn, wg_ref[...], preferred_element_type=F32)
    up = jnp.dot(hn, wu_ref[...], preferred_element_type=F32)
    act = ((gt * jax.nn.sigmoid(gt)) * up).astype(BF16)
    o_ref[...] += jnp.dot(act, wd_ref[...], preferred_element_type=F32)

    if final:
        @pl.when(f == nf - 1)
        def _():
            _norm_rows(o_ref, fw_ref, o_ref, tm, F32)


def _ffn_cast_kernel(x_ref, nw_ref, wg_ref, wu_ref, wd_ref, fw_ref, o_ref, wgo_ref, wuo_ref, wdo_ref, hn_ref, **kw):
    wgo_ref[...] = wg_ref[...].astype(BF16)
    wuo_ref[...] = wu_ref[...].astype(BF16)
    wdo_ref[...] = wd_ref[...].astype(BF16)
    _ffn_kernel(x_ref, nw_ref, wgo_ref, wuo_ref, wdo_ref, fw_ref, o_ref, hn_ref, **kw)


def _ffn(x, nws, layer, fw, final, w_bf16=None, w_f32=None):
    m, d = x.shape
    tm = min(m, FFN_TM)
    tf = FFN_TF if w_bf16 is not None else FFN_CAST_TF
    nf = D_FF // tf
    kw = dict(tm=tm, nf=nf, final=final)
    x_spec = pl.BlockSpec((tm, d), lambda i, f: (i, 0), pipeline_mode=pl.Buffered(1))
    nw_spec = pl.BlockSpec((None, 1, d), lambda i, f: (layer, 0, 0))
    fw_spec = pl.BlockSpec((1, d), lambda i, f: (0, 0))
    o_spec = pl.BlockSpec((tm, d), lambda i, f: (i, 0))
    o_shape = jax.ShapeDtypeStruct((m, d), F32)
    scratch = [pltpu.VMEM((tm, d), BF16)]
    cp = _params(("parallel", "arbitrary"), 56)
    up_spec = pl.BlockSpec((d, tf), lambda i, f: (0, f))
    down_spec = pl.BlockSpec((tf, d), lambda i, f: (f, 0))
    if w_bf16 is not None:
        return pl.pallas_call(
            functools.partial(_ffn_kernel, **kw), grid=(m // tm, nf),
            in_specs=[x_spec, nw_spec, up_spec, up_spec, down_spec, fw_spec],
            out_specs=o_spec, out_shape=o_shape, scratch_shapes=scratch,
            compiler_params=cp, name="ffn")(x, nws, *w_bf16, fw)
    assert m == tm, "the casting variant rewrites the bf16 weights once per row block"
    w_ups, w_downs = w_f32
    out, wg, wu, wd = pl.pallas_call(
        functools.partial(_ffn_cast_kernel, **kw), grid=(1, nf),
        in_specs=[x_spec, nw_spec,
                  pl.BlockSpec((None, d, tf), lambda i, f: (layer, 0, f)),
                  pl.BlockSpec((None, d, tf), lambda i, f: (layer, 0, f + nf)),
                  pl.BlockSpec((None, tf, d), lambda i, f: (layer, f, 0)),
                  fw_spec],
        out_specs=[o_spec, up_spec, up_spec, down_spec],
        out_shape=[o_shape, jax.ShapeDtypeStruct((d, D_FF), BF16), jax.ShapeDtypeStruct((d, D_FF), BF16),
                   jax.ShapeDtypeStruct((D_FF, d), BF16)],
        scratch_shapes=scratch, compiler_params=cp, name="ffn_cast")(x, nws, w_ups, w_ups, w_downs, fw)
    return out, (wg, wu, wd)


def _rglru_kernel(gate_ref, u_ref, h0_ref, c0_ref, cw_ref, cb_ref, wa_ref, ba_ref, wx_ref, bx_ref, lam_ref,
                  hg_ref, hout_ref, cout_ref, uext_ref, *, tc, nt):
    t = pl.program_id(2)
    pad = SUBLANES

    @pl.when(t == 0)
    def _():
        uext_ref[...] = jnp.zeros((pad, RG_BW), F32)
        uext_ref[pad - (CONV_W - 1):pad, :] = c0_ref[...]
        hout_ref[...] = h0_ref[...]

    u = u_ref[...]
    ext = jnp.concatenate([uext_ref[...], u], axis=0)
    cw = cw_ref[...]
    acc = pltpu.roll(ext, 3, 0)[pad:, :] * cw[0:1, :]
    acc = acc + pltpu.roll(ext, 2, 0)[pad:, :] * cw[1:2, :]
    acc = acc + pltpu.roll(ext, 1, 0)[pad:, :] * cw[2:3, :]
    acc = acc + u * cw[3:4, :]
    uc = cb_ref[...] + acc
    ucb = uc.astype(BF16)
    r = jax.nn.sigmoid(jnp.dot(ucb, wa_ref[...], preferred_element_type=F32) + ba_ref[...])
    i = jax.nn.sigmoid(jnp.dot(ucb, wx_ref[...], preferred_element_type=F32) + bx_ref[...])
    log_a = (RG_C * r) * _log_sigmoid(lam_ref[...])
    a = jnp.exp(log_a)
    b = _sqrt_nonneg(1.0 - a * a) * (i * uc)

    groups = tc // SUBLANES
    a3 = a.reshape(groups, SUBLANES, RG_BW)
    b3 = b.reshape(groups, SUBLANES, RG_BW)
    row = lax.broadcasted_iota(jnp.int32, (groups, SUBLANES, RG_BW), 1)
    for s in (1, 2, 4):
        keep = row >= s
        a_prev = pltpu.roll(a3, s, 1)
        b_prev = pltpu.roll(b3, s, 1)
        b3 = jnp.where(keep, b3 + a3 * b_prev, b3)
        a3 = jnp.where(keep, a3 * a_prev, a3)
    h_prev = jnp.broadcast_to(hout_ref[...], (SUBLANES, RG_BW))
    hs = []
    for g in range(groups):
        h_g = a3[g] * h_prev + b3[g]
        hs.append(h_g)
        h_prev = jnp.broadcast_to(h_g[SUBLANES - 1:SUBLANES, :], (SUBLANES, RG_BW))
    h = jnp.concatenate(hs, axis=0)
    hout_ref[...] = hs[-1][SUBLANES - 1:SUBLANES, :]
    hg_ref[...] = (h * _gelu_tanh(gate_ref[...])).astype(BF16)

    uext_ref[...] = u[tc - pad:, :]

    @pl.when(t == nt - 1)
    def _():
        cout_ref[...] = uext_ref[pad - (CONV_W - 1):pad, :]


def _rglru_core(gu, nseq, tlen, j, h_all, c_all, w):
    m = nseq * tlen
    tc = min(tlen, RGLRU_TC)
    nt = tlen // tc
    bw = RG_BW
    row = lambda b, n, t: b * nt + t
    vec = pl.BlockSpec((None, 1, bw), lambda b, n, t: (j, 0, n))
    gate_spec = pl.BlockSpec((tc, bw), lambda b, n, t: (row(b, n, t), n))
    u_spec = pl.BlockSpec((tc, bw), lambda b, n, t: (row(b, n, t), RG_BLOCKS + n))
    hin_spec = pl.BlockSpec((None, None, 1, bw), lambda b, n, t: (j, b, 0, n))
    cin_spec = pl.BlockSpec((None, None, CONV_W - 1, bw), lambda b, n, t: (j, b, 0, n))
    hout_spec = pl.BlockSpec((None, 1, bw), lambda b, n, t: (b, 0, n))
    cout_spec = pl.BlockSpec((None, CONV_W - 1, bw), lambda b, n, t: (b, 0, n))
    w_spec = pl.BlockSpec((None, None, bw, bw), lambda b, n, t: (j, n, 0, 0))
    return pl.pallas_call(
        functools.partial(_rglru_kernel, tc=tc, nt=nt),
        grid=(nseq, RG_BLOCKS, nt),
        in_specs=[gate_spec, u_spec, hin_spec, cin_spec,
                  pl.BlockSpec((None, CONV_W, bw), lambda b, n, t: (j, 0, n)), vec,
                  w_spec, vec, w_spec, vec, vec],
        out_specs=[pl.BlockSpec((tc, bw), lambda b, n, t: (row(b, n, t), n)), hout_spec, cout_spec],
        out_shape=[jax.ShapeDtypeStruct((m, D_RNN), BF16),
                   jax.ShapeDtypeStruct((nseq, 1, D_RNN), F32),
                   jax.ShapeDtypeStruct((nseq, CONV_W - 1, D_RNN), F32)],
        scratch_shapes=[pltpu.VMEM((SUBLANES, bw), F32)],
        compiler_params=_params(("parallel", "parallel", "arbitrary"), 32),
        name="rglru_core")(gu, gu, h_all, c_all, w["rg_conv_w"], w["rg_conv_b"], w["rg_w_a"], w["rg_b_a"],
                           w["rg_w_x"], w["rg_b_x"], w["rg_lambda"])


def _gla_kernel(q_ref, k_ref, v_ref, g_ref, glr_ref, s0_ref, w2_ref, bgk_ref, gnw_ref,
                o_ref, sout_ref, st_ref, *, nt, nb):
    t = pl.program_id(1)

    @pl.when(t == 0)
    def _():
        for s in range(nb):
            for h in range(GLA_HEADS):
                st_ref[s, h] = s0_ref[s, h].T

    c = CHUNK
    glr = glr_ref[...].reshape(nb * c, LANES).astype(BF16)
    gk_all = jnp.dot(glr, w2_ref[...], preferred_element_type=F32) + bgk_ref[...]
    gk_all = _log_sigmoid(gk_all) * (1.0 / GATE_NORM)
    rows = lax.broadcasted_iota(jnp.int32, (c, c), 0)
    cols = lax.broadcasted_iota(jnp.int32, (c, c), 1)
    tri = rows >= cols
    tri_b = tri.astype(BF16)
    gnw = gnw_ref[...]
    nt_dims = (((1,), (1,)), ((), ()))
    tn_dims = (((0,), (0,)), ((), ()))
    prep = []
    for s in range(nb):
        gk = gk_all[s * c:(s + 1) * c, :]
        bcum = sum(jnp.dot(tri_b, piece, preferred_element_type=F32) for piece in _split3_bf16(gk))
        blast = bcum[c - 1:c, :]
        k = k_ref[s]
        qe = ((q_ref[s] * (HEAD_K ** -0.5)) * jnp.exp(bcum)).astype(BF16)
        ke = (k * jnp.exp(-bcum)).astype(BF16)
        kd = (k * jnp.exp(blast - bcum)).astype(BF16)
        prep.append((qe, ke, kd, jnp.exp(blast)))
    for h in range(GLA_HEADS):
        ks = slice(h * HEAD_K, (h + 1) * HEAD_K)
        vs = slice(h * HEAD_V, (h + 1) * HEAD_V)
        for s in range(nb):
            qe, ke, kd, decay = prep[s]
            v_h = v_ref[s, :, vs].astype(BF16)
            att = lax.dot_general(qe[:, ks], ke[:, ks], nt_dims, preferred_element_type=F32)
            att = jnp.where(tri, att, 0.0).astype(BF16)
            st = st_ref[s, h]
            o = jnp.dot(att, v_h, preferred_element_type=F32)
            o = o + lax.dot_general(qe[:, ks], st.astype(BF16), nt_dims, preferred_element_type=F32)
            st_ref[s, h] = st * decay[:, ks] + lax.dot_general(v_h, kd[:, ks], tn_dims,
                                                               preferred_element_type=F32)
            on = o * lax.rsqrt(jnp.mean(o * o, axis=-1, keepdims=True) + EPS) * gnw
            g_h = g_ref[s, :, vs]
            o_ref[s, :, vs] = (on * (g_h * jax.nn.sigmoid(g_h))).astype(BF16)

    @pl.when(t == nt - 1)
    def _():
        for s in range(nb):
            for h in range(GLA_HEADS):
                sout_ref[s, h] = st_ref[s, h].T


def _gla_aliased_kernel(*refs, **kw):
    _gla_kernel(*refs[:9], *refs[10:], **kw)


def _gla_core(qkvg, glr, nseq, tlen, j, s_all, w, s_stack=None):
    c = CHUNK
    nt = tlen // c
    nb = GLA_SEQS
    n_gla = s_all.shape[0]
    qkvg = qkvg.reshape(nseq, tlen, GLA_MAIN)
    glr = glr.reshape(nseq, tlen, LANES)
    in_specs = [pl.BlockSpec((nb, c, GLA_DQ), lambda b, t: (b, t, 0)),
                pl.BlockSpec((nb, c, GLA_DQ), lambda b, t: (b, t, 1)),
                pl.BlockSpec((nb, c, GLA_DV), lambda b, t: (b, t, 1)),
                pl.BlockSpec((nb, c, GLA_DV), lambda b, t: (b, t, 2)),
                pl.BlockSpec((nb, c, LANES), lambda b, t: (b, t, 0)),
                pl.BlockSpec((None, nb, GLA_HEADS, HEAD_K, HEAD_V), lambda b, t: (j, b, 0, 0, 0),
                             pipeline_mode=pl.Buffered(1)),
                pl.BlockSpec((None, LANES, GLA_DQ), lambda b, t: (j, 0, 0)),
                pl.BlockSpec((None, 1, GLA_DQ), lambda b, t: (j, 0, 0)),
                pl.BlockSpec((None, 1, HEAD_V), lambda b, t: (j, 0, 0))]
    args = [qkvg, qkvg, qkvg, qkvg, glr, s_all, w["gla_w_gk2"], w["gla_b_gk"], w["gla_norm_w"]]
    body, aliases = _gla_kernel, {}
    if s_stack is not None:
        in_specs.append(pl.BlockSpec(memory_space=pl.ANY))
        args.append(s_stack)
        body, aliases = _gla_aliased_kernel, {len(args) - 1: 1}
    og, s_stack = pl.pallas_call(
        functools.partial(body, nt=nt, nb=nb),
        grid=(nseq // nb, nt),
        in_specs=in_specs,
        out_specs=[pl.BlockSpec((nb, c, GLA_DV), lambda b, t: (b, t, 0)),
                   pl.BlockSpec((None, nb, GLA_HEADS, HEAD_K, HEAD_V), lambda b, t: (j, b, 0, 0, 0))],
        out_shape=[jax.ShapeDtypeStruct((nseq, tlen, GLA_DV), BF16),
                   jax.ShapeDtypeStruct((n_gla, nseq, GLA_HEADS, HEAD_K, HEAD_V), F32)],
        scratch_shapes=[pltpu.VMEM((nb, GLA_HEADS, HEAD_V, HEAD_K), F32)],
        input_output_aliases=aliases,
        compiler_params=_params(("parallel", "arbitrary"), 56),
        name="gla_core")(*args)
    return og.reshape(nseq * tlen, GLA_DV), s_stack


def _trunk(x, h_all, conv_all, s_all, w, bf16=None):
    nseq, tlen, d = x.shape
    x = x.reshape(nseq * tlen, d)
    h_all = h_all[:, :, None, :]
    cast = bf16 is None
    made = []
    hs, cs, s_stack = [], [], None
    for layer in range(DEPTH):
        j = layer // 2
        wl = {} if cast else bf16[layer]
        if layer % 2 == 0:
            if cast:
                gu, wl["in"] = _norm_matmul(x, w["norm_mix"], layer, 2 * D_RNN, w_f32=w["rg_w_in"], j=j)
            else:
                gu = _norm_matmul(x, w["norm_mix"], layer, 2 * D_RNN, w_bf16=wl["in"])
            mixed, h_new, c_new = _rglru_core(gu, nseq, tlen, j, h_all, conv_all, w)
            hs.append(h_new[:, 0, :])
            cs.append(c_new)
            w_out = w["rg_w_out"]
        else:
            if cast:
                (qkvg, glr), wl["in"] = _norm_matmul(x, w["norm_mix"], layer, GLA_MAIN, side=GATE_RANK,
                                                     w_f32=w["gla_w_in"], j=j)
            else:
                qkvg, glr = _norm_matmul(x, w["norm_mix"], layer, GLA_MAIN, side=GATE_RANK, w_bf16=wl["in"])
            mixed, s_stack = _gla_core(qkvg, glr, nseq, tlen, j, s_all, w, s_stack)
            w_out = w["gla_w_out"]
        final = layer == DEPTH - 1
        if cast:
            x, wl["out"] = _matmul_res(mixed, x, w_f32=w_out, j=j)
            x, wl["ffn"] = _ffn(x, w["norm_ffn"], layer, w["norm_final"], final,
                                w_f32=(w["ffn_w_up"], w["ffn_w_down"]))
            made.append(wl)
        else:
            x = _matmul_res(mixed, x, w_bf16=wl["out"])
            x = _ffn(x, w["norm_ffn"], layer, w["norm_final"], final, w_bf16=wl["ffn"])
    return (x.reshape(nseq, tlen, d), jnp.stack(hs), jnp.stack(cs), s_stack), made


def _prepare_weights(norm_mix, norm_ffn, norm_final, rg_w_in, rg_conv_w, rg_conv_b, rg_w_a, rg_b_a, rg_w_x, rg_b_x,
                     rg_lambda, rg_w_out, gla_w_in, gla_w_gk2, gla_b_gk, gla_norm_w, gla_w_out, ffn_w_up, ffn_w_down):
    glr_pad = LANES - GATE_RANK
    return {
        "norm_mix": norm_mix[:, None, :],
        "norm_ffn": norm_ffn[:, None, :],
        "norm_final": norm_final[None, :],
        "rg_w_in": rg_w_in,
        "rg_conv_w": rg_conv_w,
        "rg_conv_b": rg_conv_b[:, None, :],
        "rg_w_a": rg_w_a.astype(BF16),
        "rg_b_a": rg_b_a[:, None, :],
        "rg_w_x": rg_w_x.astype(BF16),
        "rg_b_x": rg_b_x[:, None, :],
        "rg_lambda": rg_lambda[:, None, :],
        "rg_w_out": rg_w_out,
        "gla_w_in": gla_w_in,
        "gla_w_gk2": jnp.pad(gla_w_gk2, ((0, 0), (0, glr_pad), (0, 0))).astype(BF16),
        "gla_b_gk": gla_b_gk[:, None, :],
        "gla_norm_w": gla_norm_w[:, None, :],
        "gla_w_out": gla_w_out,
        "ffn_w_up": ffn_w_up,
        "ffn_w_down": ffn_w_down,
    }


def kernel(x_prompt, x_sample, state_rglru_h, state_rglru_conv, state_gla, norm_mix, norm_ffn, norm_final, rg_w_in, rg_conv_w, rg_conv_b, rg_w_a, rg_b_a, rg_w_x, rg_b_x, rg_lambda, rg_w_out, gla_w_in, gla_w_gk2, gla_b_gk, gla_norm_w, gla_w_out, ffn_w_up, ffn_w_down):
    w = _prepare_weights(norm_mix, norm_ffn, norm_final, rg_w_in, rg_conv_w, rg_conv_b, rg_w_a, rg_b_a, rg_w_x,
                         rg_b_x, rg_lambda, rg_w_out, gla_w_in, gla_w_gk2, gla_b_gk, gla_norm_w, gla_w_out,
                         ffn_w_up, ffn_w_down)
    bp = x_prompt.shape[0]
    n_rg = state_rglru_h.shape[0]
    n_gla = state_gla.shape[0]
    h0 = jnp.zeros((n_rg, bp, D_RNN), F32)
    c0 = jnp.zeros((n_rg, bp, CONV_W - 1, D_RNN), F32)
    s0 = jnp.zeros((n_gla, bp, GLA_HEADS, HEAD_K, HEAD_V), F32)
    (y_s, h_s, c_s, s_s), bf16 = _trunk(x_sample, state_rglru_h, state_rglru_conv, state_gla, w)
    (y_p, h_p, c_p, s_p), _ = _trunk(x_prompt, h0, c0, s0, w, bf16)
    return (y_p, y_s, h_p, c_p, s_p, h_s, c_s, s_s)
```

```python
import functools

import jax
import jax.numpy as jnp
from jax import lax
from jax.experimental import pallas as pl
from jax.experimental.pallas import tpu as pltpu

F32 = jnp.float32
BF16 = jnp.bfloat16

D_MODEL = 2048
DEPTH = 4
CHUNK = 64
EPS = 1e-6
D_RNN = D_MODEL
RG_BLOCKS = 8
RG_BW = D_RNN // RG_BLOCKS
CONV_W = 4
RG_C = 8.0
GLA_HEADS = 4
HEAD_K = 256
HEAD_V = 512
GATE_RANK = 16
GATE_NORM = 16.0
GLA_DQ = GLA_HEADS * HEAD_K
GLA_DV = GLA_HEADS * HEAD_V
GLA_MAIN = 2 * GLA_DQ + 2 * GLA_DV
D_FF = 5632

LANES = 128
SUBLANES = 8
NORM_ROWS = 16
NORM_UNROLL = 8
MIB = 1024 * 1024
TINY = 1e-30
GELU_C1 = 0.7978845608028654
GELU_C2 = 0.044715 * GELU_C1

MATMUL_TM = 1024
MATMUL_TN = 1024
VMEM_WINDOW_MIB = 46
FFN_TM = 1024
FFN_TF = 512
FFN_CAST_TF = 256
RGLRU_TC = 512
GLA_SEQS = 4


def _params(dims, vmem_mib):
    return pltpu.CompilerParams(dimension_semantics=dims, vmem_limit_bytes=vmem_mib * MIB)


def _rms(x, w):
    ms = jnp.mean(x * x, axis=-1, keepdims=True)
    return x * lax.rsqrt(ms + EPS) * w


def _log_sigmoid(x):
    return jnp.minimum(x, 0.0) - jnp.log(1.0 + jnp.exp(-jnp.abs(x)))


def _split3_bf16(x):
    hi = x.astype(BF16)
    r1 = x - hi.astype(F32)
    mid = r1.astype(BF16)
    lo = (r1 - mid.astype(F32)).astype(BF16)
    return hi, mid, lo


def _sqrt_nonneg(y):
    return y * lax.rsqrt(jnp.maximum(y, TINY))


def _gelu_tanh(x):
    inner = x * (GELU_C1 + GELU_C2 * (x * x))
    return (0.5 * x) * (1.0 + jnp.tanh(inner))


def _norm_rows(src_ref, nw_ref, dst_ref, rows, dtype):
    nw = nw_ref[...]

    def body(r, carry):
        sl = pl.ds(pl.multiple_of(r * NORM_ROWS, NORM_ROWS), NORM_ROWS)
        dst_ref[sl, :] = _rms(src_ref[sl, :], nw).astype(dtype)
        return carry

    lax.fori_loop(0, rows // NORM_ROWS, body, 0, unroll=NORM_UNROLL)


def _norm_rows_inplace(ref, nw_ref, scale_ref, rows):
    def scales(r, carry):
        sl = pl.ds(pl.multiple_of(r * NORM_ROWS, NORM_ROWS), NORM_ROWS)
        x = ref[sl, :]
        scale_ref[sl, :] = lax.rsqrt(jnp.mean(x * x, axis=-1, keepdims=True) + EPS)
        return carry

    lax.fori_loop(0, rows // NORM_ROWS, scales, 0, unroll=NORM_UNROLL)
    nw = nw_ref[...]

    def apply(r, carry):
        sl = pl.ds(pl.multiple_of(r * NORM_ROWS, NORM_ROWS), NORM_ROWS)
        ref[sl, :] = ref[sl, :] * scale_ref[sl, :] * nw
        return carry

    lax.fori_loop(0, rows // NORM_ROWS, apply, 0, unroll=NORM_UNROLL)


def _norm_matmul_kernel(x_ref, nw_ref, w_ref, w2_ref, o_ref, o2_ref, hn_ref, *, tm):
    @pl.when(pl.program_id(1) == 0)
    def _():
        _norm_rows(x_ref, nw_ref, hn_ref, tm, BF16)
        if w2_ref is not None:
            o2_ref[...] = jnp.dot(hn_ref[...], w2_ref[...], preferred_element_type=F32)

    o_ref[...] = jnp.dot(hn_ref[...], w_ref[...], preferred_element_type=F32)


def _norm_matmul_plain_kernel(x_ref, nw_ref, w_ref, o_ref, hn_ref, *, tm):
    _norm_matmul_kernel(x_ref, nw_ref, w_ref, None, o_ref, None, hn_ref, tm=tm)


def _norm_matmul_cast_kernel(x_ref, nw_ref, w_ref, o_ref, wo_ref, hn_ref, *, tm):
    wo_ref[...] = w_ref[...].astype(BF16)
    _norm_matmul_kernel(x_ref, nw_ref, wo_ref, None, o_ref, None, hn_ref, tm=tm)


def _norm_matmul_cast2_kernel(x_ref, nw_ref, w_ref, w2_ref, o_ref, o2_ref, wo_ref, w2o_ref, hn_ref, *, tm, side):
    wo_ref[...] = w_ref[...].astype(BF16)

    @pl.when(pl.program_id(1) == 0)
    def _():
        lane = lax.broadcasted_iota(jnp.int32, w2_ref.shape, 1)
        w2o_ref[...] = jnp.where(lane < side, w2_ref[...], 0.0).astype(BF16)

    _norm_matmul_kernel(x_ref, nw_ref, wo_ref, w2o_ref, o_ref, o2_ref, hn_ref, tm=tm)


def _norm_matmul(x, nws, layer, n, side=0, w_bf16=None, w_f32=None, j=None):
    m, d = x.shape
    tm = min(m, MATMUL_TM)
    tn = MATMUL_TN
    w_buffers = 2
    if w_bf16 is not None:
        def windows_mib(rows, cols, wbuf):
            return (2 * rows * d * 4 + rows * d * 2 + wbuf * d * cols * 2 + 2 * rows * cols * 4) / MIB
        if m > tm and windows_mib(tm // 2, n, 1) <= VMEM_WINDOW_MIB:
            tm, tn, w_buffers = tm // 2, n, 1
        else:
            tn = max(c for c in range(MATMUL_TN, n + 1, MATMUL_TN // 2)
                     if n % c == 0 and windows_mib(tm, c, 2) <= VMEM_WINDOW_MIB)
    grid = (m // tm, n // tn)
    x_spec = pl.BlockSpec((tm, d), lambda i, c: (i, 0))
    nw_spec = pl.BlockSpec((None, 1, d), lambda i, c: (layer, 0, 0))
    w_spec = pl.BlockSpec((d, tn), lambda i, c: (0, c), pipeline_mode=pl.Buffered(w_buffers))
    w2_spec = pl.BlockSpec((d, LANES), lambda i, c: (0, 0))
    o_spec = pl.BlockSpec((tm, tn), lambda i, c: (i, c))
    o2_spec = pl.BlockSpec((tm, LANES), lambda i, c: (i, 0))
    o_shape = jax.ShapeDtypeStruct((m, n), F32)
    o2_shape = jax.ShapeDtypeStruct((m, LANES), F32)
    scratch = [pltpu.VMEM((tm, d), BF16)]
    cp = _params(("parallel", "arbitrary"), 56)
    if w_bf16 is not None:
        w, w2 = w_bf16
        if not side:
            return pl.pallas_call(
                functools.partial(_norm_matmul_plain_kernel, tm=tm), grid=grid,
                in_specs=[x_spec, nw_spec, w_spec], out_specs=o_spec, out_shape=o_shape,
                scratch_shapes=scratch, compiler_params=cp, name="norm_matmul")(x, nws, w)
        return pl.pallas_call(
            functools.partial(_norm_matmul_kernel, tm=tm), grid=grid,
            in_specs=[x_spec, nw_spec, w_spec, w2_spec], out_specs=[o_spec, o2_spec],
            out_shape=[o_shape, o2_shape], scratch_shapes=scratch, compiler_params=cp,
            name="norm_matmul2")(x, nws, w, w2)
    assert m == tm, "the casting variant rewrites the bf16 weights once per row block"
    wf_spec = pl.BlockSpec((None, d, tn), lambda i, c: (j, 0, c))
    w_shape = jax.ShapeDtypeStruct((d, n), BF16)
    if not side:
        out, w = pl.pallas_call(
            functools.partial(_norm_matmul_cast_kernel, tm=tm), grid=grid,
            in_specs=[x_spec, nw_spec, wf_spec], out_specs=[o_spec, w_spec], out_shape=[o_shape, w_shape],
            scratch_shapes=scratch, compiler_params=cp, name="norm_matmul_cast")(x, nws, w_f32)
        return out, (w, None)
    out, out2, w, w2 = pl.pallas_call(
        functools.partial(_norm_matmul_cast2_kernel, tm=tm, side=side), grid=grid,
        in_specs=[x_spec, nw_spec, wf_spec, pl.BlockSpec((None, d, LANES), lambda i, c: (j, 0, n // LANES))],
        out_specs=[o_spec, o2_spec, w_spec, w2_spec],
        out_shape=[o_shape, o2_shape, w_shape, jax.ShapeDtypeStruct((d, LANES), BF16)],
        scratch_shapes=scratch, compiler_params=cp, name="norm_matmul2_cast")(x, nws, w_f32, w_f32)
    return (out, out2), (w, w2)


def _matmul_res_kernel(a_ref, w_ref, x_ref, o_ref):
    o_ref[...] = x_ref[...] + jnp.dot(a_ref[...], w_ref[...], preferred_element_type=F32)


def _matmul_res_cast_kernel(a_ref, w_ref, x_ref, o_ref, wo_ref):
    wo_ref[...] = w_ref[...].astype(BF16)
    _matmul_res_kernel(a_ref, wo_ref, x_ref, o_ref)


def _matmul_res(a, x, w_bf16=None, w_f32=None, j=None):
    m, k = a.shape
    n = x.shape[1]
    tm = min(m, MATMUL_TM)
    tn = n if w_bf16 is not None else MATMUL_TN
    grid = (m // tm, n // tn)
    a_spec = pl.BlockSpec((tm, k), lambda i, c: (i, 0))
    w_spec = pl.BlockSpec((k, tn), lambda i, c: (0, c))
    xo_spec = pl.BlockSpec((tm, tn), lambda i, c: (i, c))
    o_shape = jax.ShapeDtypeStruct((m, n), F32)
    cp = _params(("parallel", "arbitrary"), 56)
    if w_bf16 is not None:
        w_res_spec = pl.BlockSpec((k, n), lambda i, c: (0, 0), pipeline_mode=pl.Buffered(1))
        return pl.pallas_call(
            _matmul_res_kernel, grid=grid, in_specs=[a_spec, w_res_spec, xo_spec], out_specs=xo_spec,
            out_shape=o_shape, compiler_params=cp, name="matmul_res")(a, w_bf16, x)
    assert m == tm, "the casting variant rewrites the bf16 weight once per row block"
    return pl.pallas_call(
        _matmul_res_cast_kernel, grid=grid,
        in_specs=[a_spec, pl.BlockSpec((None, k, tn), lambda i, c: (j, 0, c)), xo_spec],
        out_specs=[xo_spec, w_spec], out_shape=[o_shape, jax.ShapeDtypeStruct((k, n), BF16)],
        compiler_params=cp, name="matmul_res_cast")(a, w_f32, x)


def _ffn_kernel(x_ref, nw_ref, wg_ref, wu_ref, wd_ref, fw_ref, o_ref, hn_ref, scale_ref, *, tm, nf, final):
    f = pl.program_id(1)

    @pl.when(f == 0)
    def _():
        _norm_rows(x_ref, nw_ref, hn_ref, tm, BF16)
        o_ref[...] = x_ref[...]

    hn = hn_ref[...]
    gt = jnp.dot(hn, wg_ref[...], preferred_element_type=F32)
    up = jnp.dot(hn, wu_ref[...], preferred_element_type=F32)
    act = ((gt * jax.nn.sigmoid(gt)) * up).astype(BF16)
    o_ref[...] += jnp.dot(act, wd_ref[...], preferred_element_type=F32)

    if final:
        @pl.when(f == nf - 1)
        def _():
            _norm_rows_inplace(o_ref, fw_ref, scale_ref, tm)


def _ffn_cast_kernel(x_ref, nw_ref, wg_ref, wu_ref, wd_ref, fw_ref, o_ref, wgo_ref, wuo_ref, wdo_ref, hn_ref, scale_ref,
                     **kw):
    wgo_ref[...] = wg_ref[...].astype(BF16)
    wuo_ref[...] = wu_ref[...].astype(BF16)
    wdo_ref[...] = wd_ref[...].astype(BF16)
    _ffn_kernel(x_ref, nw_ref, wgo_ref, wuo_ref, wdo_ref, fw_ref, o_ref, hn_ref, scale_ref, **kw)


def _ffn(x, nws, layer, fw, final, w_bf16=None, w_f32=None):
    m, d = x.shape
    tm = min(m, FFN_TM)
    tf = FFN_TF if w_bf16 is not None else FFN_CAST_TF
    nf = D_FF // tf
    kw = dict(tm=tm, nf=nf, final=final)
    x_spec = pl.BlockSpec((tm, d), lambda i, f: (i, 0), pipeline_mode=pl.Buffered(1))
    nw_spec = pl.BlockSpec((None, 1, d), lambda i, f: (layer, 0, 0))
    fw_spec = pl.BlockSpec((1, d), lambda i, f: (0, 0))
    o_spec = pl.BlockSpec((tm, d), lambda i, f: (i, 0))
    o_shape = jax.ShapeDtypeStruct((m, d), F32)
    scratch = [pltpu.VMEM((tm, d), BF16), pltpu.VMEM((tm, 1), F32)]
    cp = _params(("parallel", "arbitrary"), 56)
    up_spec = pl.BlockSpec((d, tf), lambda i, f: (0, f))
    down_spec = pl.BlockSpec((tf, d), lambda i, f: (f, 0))
    if w_bf16 is not None:
        return pl.pallas_call(
            functools.partial(_ffn_kernel, **kw), grid=(m // tm, nf),
            in_specs=[x_spec, nw_spec, up_spec, up_spec, down_spec, fw_spec],
            out_specs=o_spec, out_shape=o_shape, scratch_shapes=scratch,
            compiler_params=cp, name="ffn")(x, nws, *w_bf16, fw)
    assert m == tm, "the casting variant rewrites the bf16 weights once per row block"
    w_ups, w_downs = w_f32
    out, wg, wu, wd = pl.pallas_call(
        functools.partial(_ffn_cast_kernel, **kw), grid=(1, nf),
        in_specs=[x_spec, nw_spec,
                  pl.BlockSpec((None, d, tf), lambda i, f: (layer, 0, f)),
                  pl.BlockSpec((None, d, tf), lambda i, f: (layer, 0, f + nf)),
                  pl.BlockSpec((None, tf, d), lambda i, f: (layer, f, 0)),
                  fw_spec],
        out_specs=[o_spec, up_spec, up_spec, down_spec],
        out_shape=[o_shape, jax.ShapeDtypeStruct((d, D_FF), BF16), jax.ShapeDtypeStruct((d, D_FF), BF16),
                   jax.ShapeDtypeStruct((D_FF, d), BF16)],
        scratch_shapes=scratch, compiler_params=cp, name="ffn_cast")(x, nws, w_ups, w_ups, w_downs, fw)
    return out, (wg, wu, wd)


def _rglru_kernel(gate_ref, u_ref, h0_ref, c0_ref, cw_ref, cb_ref, wa_ref, ba_ref, wx_ref, bx_ref, lam_ref,
                  hg_ref, hout_ref, cout_ref, uext_ref, *, tc, nt):
    t = pl.program_id(2)
    pad = SUBLANES

    @pl.when(t == 0)
    def _():
        uext_ref[...] = jnp.zeros((pad, RG_BW), F32)
        uext_ref[pad - (CONV_W - 1):pad, :] = c0_ref[...]
        hout_ref[...] = h0_ref[...]

    u = u_ref[...]
    ext = jnp.concatenate([uext_ref[...], u], axis=0)
    cw = cw_ref[...]
    acc = pltpu.roll(ext, 3, 0)[pad:, :] * cw[0:1, :]
    acc = acc + pltpu.roll(ext, 2, 0)[pad:, :] * cw[1:2, :]
    acc = acc + pltpu.roll(ext, 1, 0)[pad:, :] * cw[2:3, :]
    acc = acc + u * cw[3:4, :]
    uc = cb_ref[...] + acc
    ucb = uc.astype(BF16)
    r = jax.nn.sigmoid(jnp.dot(ucb, wa_ref[...], preferred_element_type=F32) + ba_ref[...])
    i = jax.nn.sigmoid(jnp.dot(ucb, wx_ref[...], preferred_element_type=F32) + bx_ref[...])
    log_a = (RG_C * r) * _log_sigmoid(lam_ref[...])
    a = jnp.exp(log_a)
    b = _sqrt_nonneg(1.0 - a * a) * (i * uc)

    groups = tc // SUBLANES
    a3 = a.reshape(groups, SUBLANES, RG_BW)
    b3 = b.reshape(groups, SUBLANES, RG_BW)
    row = lax.broadcasted_iota(jnp.int32, (groups, SUBLANES, RG_BW), 1)
    for s in (1, 2, 4):
        keep = row >= s
        a_prev = pltpu.roll(a3, s, 1)
        b_prev = pltpu.roll(b3, s, 1)
        b3 = jnp.where(keep, b3 + a3 * b_prev, b3)
        a3 = jnp.where(keep, a3 * a_prev, a3)
    h_prev = jnp.broadcast_to(hout_ref[...], (SUBLANES, RG_BW))
    hs = []
    for g in range(groups):
        h_g = a3[g] * h_prev + b3[g]
        hs.append(h_g)
        h_prev = jnp.broadcast_to(h_g[SUBLANES - 1:SUBLANES, :], (SUBLANES, RG_BW))
    h = jnp.concatenate(hs, axis=0)
    hout_ref[...] = hs[-1][SUBLANES - 1:SUBLANES, :]
    hg_ref[...] = (h * _gelu_tanh(gate_ref[...])).astype(BF16)

    uext_ref[...] = u[tc - pad:, :]

    @pl.when(t == nt - 1)
    def _():
        cout_ref[...] = uext_ref[pad - (CONV_W - 1):pad, :]


def _rglru_core(gu, nseq, tlen, j, h_all, c_all, w):
    m = nseq * tlen
    tc = min(tlen, RGLRU_TC)
    nt = tlen // tc
    bw = RG_BW
    row = lambda b, n, t: b * nt + t
    vec = pl.BlockSpec((None, 1, bw), lambda b, n, t: (j, 0, n))
    gate_spec = pl.BlockSpec((tc, bw), lambda b, n, t: (row(b, n, t), n))
    u_spec = pl.BlockSpec((tc, bw), lambda b, n, t: (row(b, n, t), RG_BLOCKS + n))
    hin_spec = pl.BlockSpec((None, None, 1, bw), lambda b, n, t: (j, b, 0, n))
    cin_spec = pl.BlockSpec((None, None, CONV_W - 1, bw), lambda b, n, t: (j, b, 0, n))
    hout_spec = pl.BlockSpec((None, 1, bw), lambda b, n, t: (b, 0, n))
    cout_spec = pl.BlockSpec((None, CONV_W - 1, bw), lambda b, n, t: (b, 0, n))
    w_spec = pl.BlockSpec((None, None, bw, bw), lambda b, n, t: (j, n, 0, 0))
    return pl.pallas_call(
        functools.partial(_rglru_kernel, tc=tc, nt=nt),
        grid=(nseq, RG_BLOCKS, nt),
        in_specs=[gate_spec, u_spec, hin_spec, cin_spec,
                  pl.BlockSpec((None, CONV_W, bw), lambda b, n, t: (j, 0, n)), vec,
                  w_spec, vec, w_spec, vec, vec],
        out_specs=[pl.BlockSpec((tc, bw), lambda b, n, t: (row(b, n, t), n)), hout_spec, cout_spec],
        out_shape=[jax.ShapeDtypeStruct((m, D_RNN), BF16),
                   jax.ShapeDtypeStruct((nseq, 1, D_RNN), F32),
                   jax.ShapeDtypeStruct((nseq, CONV_W - 1, D_RNN), F32)],
        scratch_shapes=[pltpu.VMEM((SUBLANES, bw), F32)],
        compiler_params=_params(("parallel", "parallel", "arbitrary"), 32),
        name="rglru_core")(gu, gu, h_all, c_all, w["rg_conv_w"], w["rg_conv_b"], w["rg_w_a"], w["rg_b_a"],
                           w["rg_w_x"], w["rg_b_x"], w["rg_lambda"])


def _gla_kernel(q_ref, k_ref, v_ref, g_ref, glr_ref, s0_ref, w2_ref, bgk_ref, gnw_ref,
                o_ref, sout_ref, st_ref, *, nt, nb):
    t = pl.program_id(1)

    @pl.when(t == 0)
    def _():
        for s in range(nb):
            for h in range(GLA_HEADS):
                st_ref[s, h] = s0_ref[s, h].T

    c = CHUNK
    glr = glr_ref[...].reshape(nb * c, LANES).astype(BF16)
    gk_all = jnp.dot(glr, w2_ref[...], preferred_element_type=F32) + bgk_ref[...]
    gk_all = _log_sigmoid(gk_all) * (1.0 / GATE_NORM)
    rows = lax.broadcasted_iota(jnp.int32, (c, c), 0)
    cols = lax.broadcasted_iota(jnp.int32, (c, c), 1)
    tri = rows >= cols
    tri_b = tri.astype(BF16)
    gnw = gnw_ref[...]
    nt_dims = (((1,), (1,)), ((), ()))
    tn_dims = (((0,), (0,)), ((), ()))
    prep = []
    for s in range(nb):
        gk = gk_all[s * c:(s + 1) * c, :]
        bcum = sum(jnp.dot(tri_b, piece, preferred_element_type=F32) for piece in _split3_bf16(gk))
        blast = bcum[c - 1:c, :]
        k = k_ref[s]
        qe = ((q_ref[s] * (HEAD_K ** -0.5)) * jnp.exp(bcum)).astype(BF16)
        ke = (k * jnp.exp(-bcum)).astype(BF16)
        kd = (k * jnp.exp(blast - bcum)).astype(BF16)
        prep.append((qe, ke, kd, jnp.exp(blast)))
    for h in range(GLA_HEADS):
        ks = slice(h * HEAD_K, (h + 1) * HEAD_K)
        vs = slice(h * HEAD_V, (h + 1) * HEAD_V)
        for s in range(nb):
            qe, ke, kd, decay = prep[s]
            v_h = v_ref[s, :, vs].astype(BF16)
            att = lax.dot_general(qe[:, ks], ke[:, ks], nt_dims, preferred_element_type=F32)
            att = jnp.where(tri, att, 0.0).astype(BF16)
            st = st_ref[s, h]
            o = jnp.dot(att, v_h, preferred_element_type=F32)
            o = o + lax.dot_general(qe[:, ks], st.astype(BF16), nt_dims, preferred_element_type=F32)
            st_ref[s, h] = st * decay[:, ks] + lax.dot_general(v_h, kd[:, ks], tn_dims,
                                                               preferred_element_type=F32)
            on = o * lax.rsqrt(jnp.mean(o * o, axis=-1, keepdims=True) + EPS) * gnw
            g_h = g_ref[s, :, vs]
            o_ref[s, :, vs] = (on * (g_h * jax.nn.sigmoid(g_h))).astype(BF16)

    @pl.when(t == nt - 1)
    def _():
        for s in range(nb):
            for h in range(GLA_HEADS):
                sout_ref[s, h] = st_ref[s, h].T


def _gla_aliased_kernel(*refs, **kw):
    _gla_kernel(*refs[:9], *refs[10:], **kw)


def _gla_core(qkvg, glr, nseq, tlen, j, s_all, w, s_stack=None):
    c = CHUNK
    nt = tlen // c
    nb = GLA_SEQS
    n_gla = s_all.shape[0]
    qkvg = qkvg.reshape(nseq, tlen, GLA_MAIN)
    glr = glr.reshape(nseq, tlen, LANES)
    in_specs = [pl.BlockSpec((nb, c, GLA_DQ), lambda b, t: (b, t, 0)),
                pl.BlockSpec((nb, c, GLA_DQ), lambda b, t: (b, t, 1)),
                pl.BlockSpec((nb, c, GLA_DV), lambda b, t: (b, t, 1)),
                pl.BlockSpec((nb, c, GLA_DV), lambda b, t: (b, t, 2)),
                pl.BlockSpec((nb, c, LANES), lambda b, t: (b, t, 0)),
                pl.BlockSpec((None, nb, GLA_HEADS, HEAD_K, HEAD_V), lambda b, t: (j, b, 0, 0, 0),
                             pipeline_mode=pl.Buffered(1)),
                pl.BlockSpec((None, LANES, GLA_DQ), lambda b, t: (j, 0, 0)),
                pl.BlockSpec((None, 1, GLA_DQ), lambda b, t: (j, 0, 0)),
                pl.BlockSpec((None, 1, HEAD_V), lambda b, t: (j, 0, 0))]
    args = [qkvg, qkvg, qkvg, qkvg, glr, s_all, w["gla_w_gk2"], w["gla_b_gk"], w["gla_norm_w"]]
    body, aliases = _gla_kernel, {}
    if s_stack is not None:
        in_specs.append(pl.BlockSpec(memory_space=pl.ANY))
        args.append(s_stack)
        body, aliases = _gla_aliased_kernel, {len(args) - 1: 1}
    og, s_stack = pl.pallas_call(
        functools.partial(body, nt=nt, nb=nb),
        grid=(nseq // nb, nt),
        in_specs=in_specs,
        out_specs=[pl.BlockSpec((nb, c, GLA_DV), lambda b, t: (b, t, 0)),
                   pl.BlockSpec((None, nb, GLA_HEADS, HEAD_K, HEAD_V), lambda b, t: (j, b, 0, 0, 0))],
        out_shape=[jax.ShapeDtypeStruct((nseq, tlen, GLA_DV), BF16),
                   jax.ShapeDtypeStruct((n_gla, nseq, GLA_HEADS, HEAD_K, HEAD_V), F32)],
        scratch_shapes=[pltpu.VMEM((nb, GLA_HEADS, HEAD_V, HEAD_K), F32)],
        input_output_aliases=aliases,
        compiler_params=_params(("parallel", "arbitrary"), 56),
        name="gla_core")(*args)
    return og.reshape(nseq * tlen, GLA_DV), s_stack


def _trunk(x, h_all, conv_all, s_all, w, bf16=None):
    nseq, tlen, d = x.shape
    x = x.reshape(nseq * tlen, d)
    h_all = h_all[:, :, None, :]
    cast = bf16 is None
    made = []
    hs, cs, s_stack = [], [], None
    for layer in range(DEPTH):
        j = layer // 2
        wl = {} if cast else bf16[layer]
        if layer % 2 == 0:
            if cast:
                gu, wl["in"] = _norm_matmul(x, w["norm_mix"], layer, 2 * D_RNN, w_f32=w["rg_w_in"], j=j)
            else:
                gu = _norm_matmul(x, w["norm_mix"], layer, 2 * D_RNN, w_bf16=wl["in"])
            mixed, h_new, c_new = _rglru_core(gu, nseq, tlen, j, h_all, conv_all, w)
            hs.append(h_new[:, 0, :])
            cs.append(c_new)
            w_out = w["rg_w_out"]
        else:
            if cast:
                (qkvg, glr), wl["in"] = _norm_matmul(x, w["norm_mix"], layer, GLA_MAIN, side=GATE_RANK,
                                                     w_f32=w["gla_w_in"], j=j)
            else:
                qkvg, glr = _norm_matmul(x, w["norm_mix"], layer, GLA_MAIN, side=GATE_RANK, w_bf16=wl["in"])
            mixed, s_stack = _gla_core(qkvg, glr, nseq, tlen, j, s_all, w, s_stack)
            w_out = w["gla_w_out"]
        final = layer == DEPTH - 1
        if cast:
            x, wl["out"] = _matmul_res(mixed, x, w_f32=w_out, j=j)
            x, wl["ffn"] = _ffn(x, w["norm_ffn"], layer, w["norm_final"], final,
                                w_f32=(w["ffn_w_up"], w["ffn_w_down"]))
            made.append(wl)
        else:
            x = _matmul_res(mixed, x, w_bf16=wl["out"])
            x = _ffn(x, w["norm_ffn"], layer, w["norm_final"], final, w_bf16=wl["ffn"])
    return (x.reshape(nseq, tlen, d), jnp.stack(hs), jnp.stack(cs), s_stack), made


def _prepare_weights(norm_mix, norm_ffn, norm_final, rg_w_in, rg_conv_w, rg_conv_b, rg_w_a, rg_b_a, rg_w_x, rg_b_x,
                     rg_lambda, rg_w_out, gla_w_in, gla_w_gk2, gla_b_gk, gla_norm_w, gla_w_out, ffn_w_up, ffn_w_down):
    glr_pad = LANES - GATE_RANK
    return {
        "norm_mix": norm_mix[:, None, :],
        "norm_ffn": norm_ffn[:, None, :],
        "norm_final": norm_final[None, :],
        "rg_w_in": rg_w_in,
        "rg_conv_w": rg_conv_w,
        "rg_conv_b": rg_conv_b[:, None, :],
        "rg_w_a": rg_w_a.astype(BF16),
        "rg_b_a": rg_b_a[:, None, :],
        "rg_w_x": rg_w_x.astype(BF16),
        "rg_b_x": rg_b_x[:, None, :],
        "rg_lambda": rg_lambda[:, None, :],
        "rg_w_out": rg_w_out,
        "gla_w_in": gla_w_in,
        "gla_w_gk2": jnp.pad(gla_w_gk2, ((0, 0), (0, glr_pad), (0, 0))).astype(BF16),
        "gla_b_gk": gla_b_gk[:, None, :],
        "gla_norm_w": gla_norm_w[:, None, :],
        "gla_w_out": gla_w_out,
        "ffn_w_up": ffn_w_up,
        "ffn_w_down": ffn_w_down,
    }


def kernel(x_prompt, x_sample, state_rglru_h, state_rglru_conv, state_gla, norm_mix, norm_ffn, norm_final, rg_w_in, rg_conv_w, rg_conv_b, rg_w_a, rg_b_a, rg_w_x, rg_b_x, rg_lambda, rg_w_out, gla_w_in, gla_w_gk2, gla_b_gk, gla_norm_w, gla_w_out, ffn_w_up, ffn_w_down):
    w = _prepare_weights(norm_mix, norm_ffn, norm_final, rg_w_in, rg_conv_w, rg_conv_b, rg_w_a, rg_b_a, rg_w_x,
                         rg_b_x, rg_lambda, rg_w_out, gla_w_in, gla_w_gk2, gla_b_gk, gla_norm_w, gla_w_out,
                         ffn_w_up, ffn_w_down)
    bp = x_prompt.shape[0]
    n_rg = state_rglru_h.shape[0]
    n_gla = state_gla.shape[0]
    h0 = jnp.zeros((n_rg, bp, D_RNN), F32)
    c0 = jnp.zeros((n_rg, bp, CONV_W - 1, D_RNN), F32)
    s0 = jnp.zeros((n_gla, bp, GLA_HEADS, HEAD_K, HEAD_V), F32)
    (y_s, h_s, c_s, s_s), bf16 = _trunk(x_sample, state_rglru_h, state_rglru_conv, state_gla, w)
    (y_p, h_p, c_p, s_p), _ = _trunk(x_prompt, h0, c0, s0, w, bf16)
    return (y_p, y_s, h_p, c_p, s_p, h_s, c_s, s_s)
```

```python
import functools

import jax
import jax.numpy as jnp
from jax import lax
from jax.experimental import pallas as pl
from jax.experimental.pallas import tpu as pltpu

F32 = jnp.float32
BF16 = jnp.bfloat16

D_MODEL = 2048
DEPTH = 4
CHUNK = 64
EPS = 1e-6
D_RNN = D_MODEL
RG_BLOCKS = 8
RG_BW = D_RNN // RG_BLOCKS
CONV_W = 4
RG_C = 8.0
GLA_HEADS = 4
HEAD_K = 256
HEAD_V = 512
GATE_RANK = 16
GATE_NORM = 16.0
GLA_DQ = GLA_HEADS * HEAD_K
GLA_DV = GLA_HEADS * HEAD_V
GLA_MAIN = 2 * GLA_DQ + 2 * GLA_DV
D_FF = 5632

LANES = 128
SUBLANES = 8
NORM_ROWS = 16
NORM_UNROLL = 8
MIB = 1024 * 1024
LOG2E = 1.4426950408889634
GELU_C1 = 0.7978845608028654
GELU_C2 = 0.044715 * GELU_C1

MATMUL_TM = 1024
MATMUL_TN = 1024
VMEM_WINDOW_MIB = 46
FFN_TM = 1024
FFN_TF = 512
FFN_CAST_TF = 256
RGLRU_TC = 512
GLA_SEQS = 4


def _params(dims, vmem_mib):
    return pltpu.CompilerParams(dimension_semantics=dims, vmem_limit_bytes=vmem_mib * MIB)


def _rms(x, w):
    ms = jnp.mean(x * x, axis=-1, keepdims=True)
    return x * lax.rsqrt(ms + EPS) * w


def _log_sigmoid(x):
    return jnp.minimum(x, 0.0) - jnp.log(1.0 + jnp.exp(-jnp.abs(x)))


def _split3_bf16(x):
    hi = x.astype(BF16)
    r1 = x - hi.astype(F32)
    mid = r1.astype(BF16)
    lo = (r1 - mid.astype(F32)).astype(BF16)
    return hi, mid, lo


def _sqrt_nonneg(y):
    return jnp.exp2((0.5 * LOG2E) * jnp.log(y))


def _gelu_tanh(x):
    inner = x * (GELU_C1 + GELU_C2 * (x * x))
    return (0.5 * x) * (1.0 + jnp.tanh(inner))


def _norm_rows(src_ref, nw_ref, dst_ref, rows, dtype):
    nw = nw_ref[...]

    def body(r, carry):
        sl = pl.ds(pl.multiple_of(r * NORM_ROWS, NORM_ROWS), NORM_ROWS)
        dst_ref[sl, :] = _rms(src_ref[sl, :], nw).astype(dtype)
        return carry

    lax.fori_loop(0, rows // NORM_ROWS, body, 0, unroll=NORM_UNROLL)


def _norm_rows_inplace(ref, nw_ref, scale_ref, rows):
    def scales(r, carry):
        sl = pl.ds(pl.multiple_of(r * NORM_ROWS, NORM_ROWS), NORM_ROWS)
        x = ref[sl, :]
        scale_ref[sl, :] = lax.rsqrt(jnp.mean(x * x, axis=-1, keepdims=True) + EPS)
        return carry

    lax.fori_loop(0, rows // NORM_ROWS, scales, 0, unroll=NORM_UNROLL)
    nw = nw_ref[...]

    def apply(r, carry):
        sl = pl.ds(pl.multiple_of(r * NORM_ROWS, NORM_ROWS), NORM_ROWS)
        ref[sl, :] = ref[sl, :] * scale_ref[sl, :] * nw
        return carry

    lax.fori_loop(0, rows // NORM_ROWS, apply, 0, unroll=NORM_UNROLL)


def _norm_matmul_kernel(x_ref, nw_ref, w_ref, w2_ref, o_ref, o2_ref, hn_ref, *, tm):
    @pl.when(pl.program_id(1) == 0)
    def _():
        _norm_rows(x_ref, nw_ref, hn_ref, tm, BF16)
        if w2_ref is not None:
            o2_ref[...] = jnp.dot(hn_ref[...], w2_ref[...], preferred_element_type=F32)

    o_ref[...] = jnp.dot(hn_ref[...], w_ref[...], preferred_element_type=F32)


def _norm_matmul_plain_kernel(x_ref, nw_ref, w_ref, o_ref, hn_ref, *, tm):
    _norm_matmul_kernel(x_ref, nw_ref, w_ref, None, o_ref, None, hn_ref, tm=tm)


def _norm_matmul_cast_kernel(x_ref, nw_ref, w_ref, o_ref, wo_ref, hn_ref, *, tm):
    wo_ref[...] = w_ref[...].astype(BF16)
    _norm_matmul_kernel(x_ref, nw_ref, wo_ref, None, o_ref, None, hn_ref, tm=tm)


def _norm_matmul_cast2_kernel(x_ref, nw_ref, w_ref, w2_ref, o_ref, o2_ref, wo_ref, w2o_ref, hn_ref, *, tm):
    wo_ref[...] = w_ref[...].astype(BF16)

    @pl.when(pl.program_id(1) == 0)
    def _():
        w2o_ref[...] = w2_ref[...].astype(BF16)

    _norm_matmul_kernel(x_ref, nw_ref, wo_ref, w2o_ref, o_ref, o2_ref, hn_ref, tm=tm)


def _norm_matmul(x, nws, layer, n, side=False, w_bf16=None, w_f32=None, j=None):
    m, d = x.shape
    tm = min(m, MATMUL_TM)
    tn = MATMUL_TN
    w_buffers = 2
    if w_bf16 is not None:
        def windows_mib(rows, cols, wbuf):
            return (2 * rows * d * 4 + rows * d * 2 + wbuf * d * cols * 2 + 2 * rows * cols * 4) / MIB
        if m > tm and windows_mib(tm // 2, n, 1) <= VMEM_WINDOW_MIB:
            tm, tn, w_buffers = tm // 2, n, 1
        else:
            tn = max(c for c in range(MATMUL_TN, n + 1, MATMUL_TN // 2)
                     if n % c == 0 and windows_mib(tm, c, 2) <= VMEM_WINDOW_MIB)
    grid = (m // tm, n // tn)
    x_spec = pl.BlockSpec((tm, d), lambda i, c: (i, 0))
    nw_spec = pl.BlockSpec((None, 1, d), lambda i, c: (layer, 0, 0))
    w_spec = pl.BlockSpec((d, tn), lambda i, c: (0, c), pipeline_mode=pl.Buffered(w_buffers))
    w2_spec = pl.BlockSpec((d, LANES), lambda i, c: (0, 0))
    o_spec = pl.BlockSpec((tm, tn), lambda i, c: (i, c))
    o2_spec = pl.BlockSpec((tm, LANES), lambda i, c: (i, 0))
    o_shape = jax.ShapeDtypeStruct((m, n), F32)
    o2_shape = jax.ShapeDtypeStruct((m, LANES), F32)
    scratch = [pltpu.VMEM((tm, d), BF16)]
    cp = _params(("parallel", "arbitrary"), 56)
    if w_bf16 is not None:
        w, w2 = w_bf16
        if not side:
            return pl.pallas_call(
                functools.partial(_norm_matmul_plain_kernel, tm=tm), grid=grid,
                in_specs=[x_spec, nw_spec, w_spec], out_specs=o_spec, out_shape=o_shape,
                scratch_shapes=scratch, compiler_params=cp, name="norm_matmul")(x, nws, w)
        return pl.pallas_call(
            functools.partial(_norm_matmul_kernel, tm=tm), grid=grid,
            in_specs=[x_spec, nw_spec, w_spec, w2_spec], out_specs=[o_spec, o2_spec],
            out_shape=[o_shape, o2_shape], scratch_shapes=scratch, compiler_params=cp,
            name="norm_matmul2")(x, nws, w, w2)
    assert m == tm, "the casting variant rewrites the bf16 weights once per row block"
    wf_spec = pl.BlockSpec((None, d, tn), lambda i, c: (j, 0, c))
    w_shape = jax.ShapeDtypeStruct((d, n), BF16)
    if not side:
        out, w = pl.pallas_call(
            functools.partial(_norm_matmul_cast_kernel, tm=tm), grid=grid,
            in_specs=[x_spec, nw_spec, wf_spec], out_specs=[o_spec, w_spec], out_shape=[o_shape, w_shape],
            scratch_shapes=scratch, compiler_params=cp, name="norm_matmul_cast")(x, nws, w_f32)
        return out, (w, None)
    out, out2, w, w2 = pl.pallas_call(
        functools.partial(_norm_matmul_cast2_kernel, tm=tm), grid=grid,
        in_specs=[x_spec, nw_spec, wf_spec, pl.BlockSpec((None, d, LANES), lambda i, c: (j, 0, 0))],
        out_specs=[o_spec, o2_spec, w_spec, w2_spec],
        out_shape=[o_shape, o2_shape, w_shape, jax.ShapeDtypeStruct((d, LANES), BF16)],
        scratch_shapes=scratch, compiler_params=cp, name="norm_matmul2_cast")(x, nws, *w_f32)
    return (out, out2), (w, w2)


def _matmul_res_kernel(a_ref, w_ref, x_ref, o_ref):
    o_ref[...] = x_ref[...] + jnp.dot(a_ref[...], w_ref[...], preferred_element_type=F32)


def _matmul_res_cast_kernel(a_ref, w_ref, x_ref, o_ref, wo_ref):
    wo_ref[...] = w_ref[...].astype(BF16)
    _matmul_res_kernel(a_ref, wo_ref, x_ref, o_ref)


def _matmul_res(a, x, w_bf16=None, w_f32=None, j=None):
    m, k = a.shape
    n = x.shape[1]
    tm = min(m, MATMUL_TM)
    tn = n if w_bf16 is not None else MATMUL_TN
    grid = (m // tm, n // tn)
    a_spec = pl.BlockSpec((tm, k), lambda i, c: (i, 0))
    w_spec = pl.BlockSpec((k, tn), lambda i, c: (0, c))
    xo_spec = pl.BlockSpec((tm, tn), lambda i, c: (i, c))
    o_shape = jax.ShapeDtypeStruct((m, n), F32)
    cp = _params(("parallel", "arbitrary"), 56)
    if w_bf16 is not None:
        w_res_spec = pl.BlockSpec((k, n), lambda i, c: (0, 0), pipeline_mode=pl.Buffered(1))
        return pl.pallas_call(
            _matmul_res_kernel, grid=grid, in_specs=[a_spec, w_res_spec, xo_spec], out_specs=xo_spec,
            out_shape=o_shape, compiler_params=cp, name="matmul_res")(a, w_bf16, x)
    assert m == tm, "the casting variant rewrites the bf16 weight once per row block"
    return pl.pallas_call(
        _matmul_res_cast_kernel, grid=grid,
        in_specs=[a_spec, pl.BlockSpec((None, k, tn), lambda i, c: (j, 0, c)), xo_spec],
        out_specs=[xo_spec, w_spec], out_shape=[o_shape, jax.ShapeDtypeStruct((k, n), BF16)],
        compiler_params=cp, name="matmul_res_cast")(a, w_f32, x)


def _ffn_kernel(x_ref, nw_ref, wg_ref, wu_ref, wd_ref, fw_ref, o_ref, hn_ref, scale_ref, *, tm, nf, final):
    f = pl.program_id(1)

    @pl.when(f == 0)
    def _():
        _norm_rows(x_ref, nw_ref, hn_ref, tm, BF16)
        o_ref[...] = x_ref[...]

    hn = hn_ref[...]
    gt = jnp.dot(hn, wg_ref[...], preferred_element_type=F32)
    up = jnp.dot(hn, wu_ref[...], preferred_element_type=F32)
    act = ((gt * jax.nn.sigmoid(gt)) * up).astype(BF16)
    o_ref[...] += jnp.dot(act, wd_ref[...], preferred_element_type=F32)

    if final:
        @pl.when(f == nf - 1)
        def _():
            _norm_rows_inplace(o_ref, fw_ref, scale_ref, tm)


def _ffn_cast_kernel(x_ref, nw_ref, wg_ref, wu_ref, wd_ref, fw_ref, o_ref, wgo_ref, wuo_ref, wdo_ref, hn_ref, scale_ref,
                     **kw):
    wgo_ref[...] = wg_ref[...].astype(BF16)
    wuo_ref[...] = wu_ref[...].astype(BF16)
    wdo_ref[...] = wd_ref[...].astype(BF16)
    _ffn_kernel(x_ref, nw_ref, wgo_ref, wuo_ref, wdo_ref, fw_ref, o_ref, hn_ref, scale_ref, **kw)


def _ffn(x, nws, layer, fw, final, w_bf16=None, w_f32=None):
    m, d = x.shape
    tm = min(m, FFN_TM)
    tf = FFN_TF if w_bf16 is not None else FFN_CAST_TF
    nf = D_FF // tf
    kw = dict(tm=tm, nf=nf, final=final)
    x_spec = pl.BlockSpec((tm, d), lambda i, f: (i, 0), pipeline_mode=pl.Buffered(1))
    nw_spec = pl.BlockSpec((None, 1, d), lambda i, f: (layer, 0, 0))
    fw_spec = pl.BlockSpec((1, d), lambda i, f: (0, 0))
    o_spec = pl.BlockSpec((tm, d), lambda i, f: (i, 0))
    o_shape = jax.ShapeDtypeStruct((m, d), F32)
    scratch = [pltpu.VMEM((tm, d), BF16), pltpu.VMEM((tm, 1), F32)]
    cp = _params(("parallel", "arbitrary"), 56)
    up_spec = pl.BlockSpec((d, tf), lambda i, f: (0, f))
    down_spec = pl.BlockSpec((tf, d), lambda i, f: (f, 0))
    if w_bf16 is not None:
        return pl.pallas_call(
            functools.partial(_ffn_kernel, **kw), grid=(m // tm, nf),
            in_specs=[x_spec, nw_spec, up_spec, up_spec, down_spec, fw_spec],
            out_specs=o_spec, out_shape=o_shape, scratch_shapes=scratch,
            compiler_params=cp, name="ffn")(x, nws, *w_bf16, fw)
    assert m == tm, "the casting variant rewrites the bf16 weights once per row block"
    w_ups, w_downs = w_f32
    out, wg, wu, wd = pl.pallas_call(
        functools.partial(_ffn_cast_kernel, **kw), grid=(1, nf),
        in_specs=[x_spec, nw_spec,
                  pl.BlockSpec((None, d, tf), lambda i, f: (layer, 0, f)),
                  pl.BlockSpec((None, d, tf), lambda i, f: (layer, 0, f + nf)),
                  pl.BlockSpec((None, tf, d), lambda i, f: (layer, f, 0)),
                  fw_spec],
        out_specs=[o_spec, up_spec, up_spec, down_spec],
        out_shape=[o_shape, jax.ShapeDtypeStruct((d, D_FF), BF16), jax.ShapeDtypeStruct((d, D_FF), BF16),
                   jax.ShapeDtypeStruct((D_FF, d), BF16)],
        scratch_shapes=scratch, compiler_params=cp, name="ffn_cast")(x, nws, w_ups, w_ups, w_downs, fw)
    return out, (wg, wu, wd)


def _rglru_kernel(gate_ref, u_ref, h0_ref, c0_ref, cw_ref, cb_ref, wa_ref, ba_ref, wx_ref, bx_ref, lam_ref,
                  hg_ref, hout_ref, cout_ref, uext_ref, *, tc, nt):
    t = pl.program_id(2)
    pad = SUBLANES

    @pl.when(t == 0)
    def _():
        uext_ref[...] = jnp.zeros((pad, RG_BW), F32)
        uext_ref[pad - (CONV_W - 1):pad, :] = c0_ref[...]
        hout_ref[...] = h0_ref[...]

    u = u_ref[...]
    ext = jnp.concatenate([uext_ref[...], u], axis=0)
    cw = cw_ref[...]
    acc = pltpu.roll(ext, 3, 0)[pad:, :] * cw[0:1, :]
    acc = acc + pltpu.roll(ext, 2, 0)[pad:, :] * cw[1:2, :]
    acc = acc + pltpu.roll(ext, 1, 0)[pad:, :] * cw[2:3, :]
    acc = acc + u * cw[3:4, :]
    uc = cb_ref[...] + acc
    ucb = uc.astype(BF16)
    r = jax.nn.sigmoid(jnp.dot(ucb, wa_ref[...], preferred_element_type=F32) + ba_ref[...])
    i = jax.nn.sigmoid(jnp.dot(ucb, wx_ref[...], preferred_element_type=F32) + bx_ref[...])
    a = jnp.exp2(r * ((RG_C * LOG2E) * _log_sigmoid(lam_ref[...])))
    b = _sqrt_nonneg(1.0 - a * a) * (i * uc)

    groups = tc // SUBLANES
    a3 = a.reshape(groups, SUBLANES, RG_BW)
    b3 = b.reshape(groups, SUBLANES, RG_BW)
    row = lax.broadcasted_iota(jnp.int32, (groups, SUBLANES, RG_BW), 1)
    for s in (1, 2, 4):
        keep = row >= s
        a_prev = pltpu.roll(a3, s, 1)
        b_prev = pltpu.roll(b3, s, 1)
        b3 = jnp.where(keep, b3 + a3 * b_prev, b3)
        a3 = jnp.where(keep, a3 * a_prev, a3)
    h_prev = jnp.broadcast_to(hout_ref[...], (SUBLANES, RG_BW))
    hs = []
    for g in range(groups):
        h_g = a3[g] * h_prev + b3[g]
        hs.append(h_g)
        h_prev = jnp.broadcast_to(h_g[SUBLANES - 1:SUBLANES, :], (SUBLANES, RG_BW))
    h = jnp.concatenate(hs, axis=0)
    hout_ref[...] = hs[-1][SUBLANES - 1:SUBLANES, :]
    hg_ref[...] = (h * _gelu_tanh(gate_ref[...])).astype(BF16)

    uext_ref[...] = u[tc - pad:, :]

    @pl.when(t == nt - 1)
    def _():
        cout_ref[...] = uext_ref[pad - (CONV_W - 1):pad, :]


def _rglru_core(gu, nseq, tlen, j, h_all, c_all, w):
    m = nseq * tlen
    tc = min(tlen, RGLRU_TC)
    nt = tlen // tc
    bw = RG_BW
    row = lambda b, n, t: b * nt + t
    vec = pl.BlockSpec((None, 1, bw), lambda b, n, t: (j, 0, n))
    gate_spec = pl.BlockSpec((tc, bw), lambda b, n, t: (row(b, n, t), n))
    u_spec = pl.BlockSpec((tc, bw), lambda b, n, t: (row(b, n, t), RG_BLOCKS + n))
    hin_spec = pl.BlockSpec((None, None, 1, bw), lambda b, n, t: (j, b, 0, n))
    cin_spec = pl.BlockSpec((None, None, CONV_W - 1, bw), lambda b, n, t: (j, b, 0, n))
    hout_spec = pl.BlockSpec((None, 1, bw), lambda b, n, t: (b, 0, n))
    cout_spec = pl.BlockSpec((None, CONV_W - 1, bw), lambda b, n, t: (b, 0, n))
    w_spec = pl.BlockSpec((None, None, bw, bw), lambda b, n, t: (j, n, 0, 0))
    return pl.pallas_call(
        functools.partial(_rglru_kernel, tc=tc, nt=nt),
        grid=(nseq, RG_BLOCKS, nt),
        in_specs=[gate_spec, u_spec, hin_spec, cin_spec,
                  pl.BlockSpec((None, CONV_W, bw), lambda b, n, t: (j, 0, n)), vec,
                  w_spec, vec, w_spec, vec, vec],
        out_specs=[pl.BlockSpec((tc, bw), lambda b, n, t: (row(b, n, t), n)), hout_spec, cout_spec],
        out_shape=[jax.ShapeDtypeStruct((m, D_RNN), BF16),
                   jax.ShapeDtypeStruct((nseq, 1, D_RNN), F32),
                   jax.ShapeDtypeStruct((nseq, CONV_W - 1, D_RNN), F32)],
        scratch_shapes=[pltpu.VMEM((SUBLANES, bw), F32)],
        compiler_params=_params(("parallel", "parallel", "arbitrary"), 32),
        name="rglru_core")(gu, gu, h_all, c_all, w["rg_conv_w"], w["rg_conv_b"], w["rg_w_a"], w["rg_b_a"],
                           w["rg_w_x"], w["rg_b_x"], w["rg_lambda"])


def _gla_kernel(q_ref, k_ref, v_ref, g_ref, glr_ref, s0_ref, w2_ref, bgk_ref, gnw_ref,
                o_ref, sout_ref, st_ref, *, nt, nb):
    t = pl.program_id(1)

    @pl.when(t == 0)
    def _():
        for s in range(nb):
            for h in range(GLA_HEADS):
                st_ref[s, h] = s0_ref[s, h].T

    c = CHUNK
    glr = glr_ref[...].reshape(nb * c, LANES).astype(BF16)
    gk_all = jnp.dot(glr, w2_ref[...], preferred_element_type=F32) + bgk_ref[...]
    gk_all = _log_sigmoid(gk_all) * (1.0 / GATE_NORM)
    rows = lax.broadcasted_iota(jnp.int32, (c, c), 0)
    cols = lax.broadcasted_iota(jnp.int32, (c, c), 1)
    tri = rows >= cols
    tri_b = tri.astype(BF16)
    gnw = gnw_ref[...]
    nt_dims = (((1,), (1,)), ((), ()))
    tn_dims = (((0,), (0,)), ((), ()))
    prep = []
    for s in range(nb):
        gk = gk_all[s * c:(s + 1) * c, :]
        bcum = sum(jnp.dot(tri_b, piece, preferred_element_type=F32) for piece in _split3_bf16(gk))
        blast = bcum[c - 1:c, :]
        k = k_ref[s]
        qe = ((q_ref[s] * (HEAD_K ** -0.5)) * jnp.exp(bcum)).astype(BF16)
        ke = (k * jnp.exp(-bcum)).astype(BF16)
        kd = (k * jnp.exp(blast - bcum)).astype(BF16)
        prep.append((qe, ke, kd, jnp.exp(blast)))
    for h in range(GLA_HEADS):
        ks = slice(h * HEAD_K, (h + 1) * HEAD_K)
        vs = slice(h * HEAD_V, (h + 1) * HEAD_V)
        for s in range(nb):
            qe, ke, kd, decay = prep[s]
            v_h = v_ref[s, :, vs].astype(BF16)
            att = lax.dot_general(qe[:, ks], ke[:, ks], nt_dims, preferred_element_type=F32)
            att = jnp.where(tri, att, 0.0).astype(BF16)
            st = st_ref[s, h]
            o = jnp.dot(att, v_h, preferred_element_type=F32)
            o = o + lax.dot_general(qe[:, ks], st.astype(BF16), nt_dims, preferred_element_type=F32)
            st_ref[s, h] = st * decay[:, ks] + lax.dot_general(v_h, kd[:, ks], tn_dims,
                                                               preferred_element_type=F32)
            on = o * lax.rsqrt(jnp.mean(o * o, axis=-1, keepdims=True) + EPS) * gnw
            g_h = g_ref[s, :, vs]
            o_ref[s, :, vs] = (on * (g_h * jax.nn.sigmoid(g_h))).astype(BF16)

    @pl.when(t == nt - 1)
    def _():
        for s in range(nb):
            for h in range(GLA_HEADS):
                sout_ref[s, h] = st_ref[s, h].T


def _gla_aliased_kernel(*refs, **kw):
    _gla_kernel(*refs[:9], *refs[10:], **kw)


def _gla_core(qkvg, glr, nseq, tlen, j, s_all, w, s_stack=None):
    c = CHUNK
    nt = tlen // c
    nb = GLA_SEQS
    n_gla = s_all.shape[0]
    qkvg = qkvg.reshape(nseq, tlen, GLA_MAIN)
    glr = glr.reshape(nseq, tlen, LANES)
    in_specs = [pl.BlockSpec((nb, c, GLA_DQ), lambda b, t: (b, t, 0)),
                pl.BlockSpec((nb, c, GLA_DQ), lambda b, t: (b, t, 1)),
                pl.BlockSpec((nb, c, GLA_DV), lambda b, t: (b, t, 1)),
                pl.BlockSpec((nb, c, GLA_DV), lambda b, t: (b, t, 2)),
                pl.BlockSpec((nb, c, LANES), lambda b, t: (b, t, 0)),
                pl.BlockSpec((None, nb, GLA_HEADS, HEAD_K, HEAD_V), lambda b, t: (j, b, 0, 0, 0),
                             pipeline_mode=pl.Buffered(1)),
                pl.BlockSpec((None, LANES, GLA_DQ), lambda b, t: (j, 0, 0)),
                pl.BlockSpec((None, 1, GLA_DQ), lambda b, t: (j, 0, 0)),
                pl.BlockSpec((None, 1, HEAD_V), lambda b, t: (j, 0, 0))]
    args = [qkvg, qkvg, qkvg, qkvg, glr, s_all, w["gla_w_gk2"], w["gla_b_gk"], w["gla_norm_w"]]
    body, aliases = _gla_kernel, {}
    if s_stack is not None:
        in_specs.append(pl.BlockSpec(memory_space=pl.ANY))
        args.append(s_stack)
        body, aliases = _gla_aliased_kernel, {len(args) - 1: 1}
    og, s_stack = pl.pallas_call(
        functools.partial(body, nt=nt, nb=nb),
        grid=(nseq // nb, nt),
        in_specs=in_specs,
        out_specs=[pl.BlockSpec((nb, c, GLA_DV), lambda b, t: (b, t, 0)),
                   pl.BlockSpec((None, nb, GLA_HEADS, HEAD_K, HEAD_V), lambda b, t: (j, b, 0, 0, 0))],
        out_shape=[jax.ShapeDtypeStruct((nseq, tlen, GLA_DV), BF16),
                   jax.ShapeDtypeStruct((n_gla, nseq, GLA_HEADS, HEAD_K, HEAD_V), F32)],
        scratch_shapes=[pltpu.VMEM((nb, GLA_HEADS, HEAD_V, HEAD_K), F32)],
        input_output_aliases=aliases,
        compiler_params=_params(("parallel", "arbitrary"), 56),
        name="gla_core")(*args)
    return og.reshape(nseq * tlen, GLA_DV), s_stack


def _trunk(x, h_all, conv_all, s_all, w, bf16=None):
    nseq, tlen, d = x.shape
    x = x.reshape(nseq * tlen, d)
    h_all = h_all[:, :, None, :]
    cast = bf16 is None
    made = []
    hs, cs, s_stack = [], [], None
    for layer in range(DEPTH):
        j = layer // 2
        wl = {} if cast else bf16[layer]
        if layer % 2 == 0:
            if cast:
                gu, wl["in"] = _norm_matmul(x, w["norm_mix"], layer, 2 * D_RNN, w_f32=w["rg_w_in"], j=j)
            else:
                gu = _norm_matmul(x, w["norm_mix"], layer, 2 * D_RNN, w_bf16=wl["in"])
            mixed, h_new, c_new = _rglru_core(gu, nseq, tlen, j, h_all, conv_all, w)
            hs.append(h_new[:, 0, :])
            cs.append(c_new)
            w_out = w["rg_w_out"]
        else:
            if cast:
                (qkvg, glr), wl["in"] = _norm_matmul(x, w["norm_mix"], layer, GLA_MAIN, side=True,
                                                     w_f32=(w["gla_w_main"], w["gla_w_side"]), j=j)
            else:
                qkvg, glr = _norm_matmul(x, w["norm_mix"], layer, GLA_MAIN, side=True, w_bf16=wl["in"])
            mixed, s_stack = _gla_core(qkvg, glr, nseq, tlen, j, s_all, w, s_stack)
            w_out = w["gla_w_out"]
        final = layer == DEPTH - 1
        if cast:
            x, wl["out"] = _matmul_res(mixed, x, w_f32=w_out, j=j)
            x, wl["ffn"] = _ffn(x, w["norm_ffn"], layer, w["norm_final"], final,
                                w_f32=(w["ffn_w_up"], w["ffn_w_down"]))
            made.append(wl)
        else:
            x = _matmul_res(mixed, x, w_bf16=wl["out"])
            x = _ffn(x, w["norm_ffn"], layer, w["norm_final"], final, w_bf16=wl["ffn"])
    return (x.reshape(nseq, tlen, d), jnp.stack(hs), jnp.stack(cs), s_stack), made


def _prepare_weights(norm_mix, norm_ffn, norm_final, rg_w_in, rg_conv_w, rg_conv_b, rg_w_a, rg_b_a, rg_w_x, rg_b_x,
                     rg_lambda, rg_w_out, gla_w_in, gla_w_gk2, gla_b_gk, gla_norm_w, gla_w_out, ffn_w_up, ffn_w_down):
    glr_pad = LANES - GATE_RANK
    return {
        "norm_mix": norm_mix[:, None, :],
        "norm_ffn": norm_ffn[:, None, :],
        "norm_final": norm_final[None, :],
        "rg_w_in": rg_w_in,
        "rg_conv_w": rg_conv_w,
        "rg_conv_b": rg_conv_b[:, None, :],
        "rg_w_a": rg_w_a.astype(BF16),
        "rg_b_a": rg_b_a[:, None, :],
        "rg_w_x": rg_w_x.astype(BF16),
        "rg_b_x": rg_b_x[:, None, :],
        "rg_lambda": rg_lambda[:, None, :],
        "rg_w_out": rg_w_out,
        "gla_w_main": gla_w_in[:, :, :GLA_MAIN],
        "gla_w_side": jnp.pad(gla_w_in[:, :, GLA_MAIN:], ((0, 0), (0, 0), (0, glr_pad))),
        "gla_w_gk2": jnp.pad(gla_w_gk2, ((0, 0), (0, glr_pad), (0, 0))).astype(BF16),
        "gla_b_gk": gla_b_gk[:, None, :],
        "gla_norm_w": gla_norm_w[:, None, :],
        "gla_w_out": gla_w_out,
        "ffn_w_up": ffn_w_up,
        "ffn_w_down": ffn_w_down,
    }


def kernel(x_prompt, x_sample, state_rglru_h, state_rglru_conv, state_gla, norm_mix, norm_ffn, norm_final, rg_w_in, rg_conv_w, rg_conv_b, rg_w_a, rg_b_a, rg_w_x, rg_b_x, rg_lambda, rg_w_out, gla_w_in, gla_w_gk2, gla_b_gk, gla_norm_w, gla_w_out, ffn_w_up, ffn_w_down):
    w = _prepare_weights(norm_mix, norm_ffn, norm_final, rg_w_in, rg_conv_w, rg_conv_b, rg_w_a, rg_b_a, rg_w_x,
                         rg_b_x, rg_lambda, rg_w_out, gla_w_in, gla_w_gk2, gla_b_gk, gla_norm_w, gla_w_out,
                         ffn_w_up, ffn_w_down)
    bp = x_prompt.shape[0]
    n_rg = state_rglru_h.shape[0]
    n_gla = state_gla.shape[0]
    h0 = jnp.zeros((n_rg, bp, D_RNN), F32)
    c0 = jnp.zeros((n_rg, bp, CONV_W - 1, D_RNN), F32)
    s0 = jnp.zeros((n_gla, bp, GLA_HEADS, HEAD_K, HEAD_V), F32)
    (y_s, h_s, c_s, s_s), bf16 = _trunk(x_sample, state_rglru_h, state_rglru_conv, state_gla, w)
    (y_p, h_p, c_p, s_p), _ = _trunk(x_prompt, h0, c0, s0, w, bf16)
    return (y_p, y_s, h_p, c_p, s_p, h_s, c_s, s_s)
```

```python
import functools

import jax
import jax.numpy as jnp
from jax import lax
from jax.experimental import pallas as pl
from jax.experimental.pallas import tpu as pltpu

F32 = jnp.float32
BF16 = jnp.bfloat16

D_MODEL = 2048
DEPTH = 4
CHUNK = 64
EPS = 1e-6
D_RNN = D_MODEL
RG_BLOCKS = 8
RG_BW = D_RNN // RG_BLOCKS
CONV_W = 4
RG_C = 8.0
GLA_HEADS = 4
HEAD_K = 256
HEAD_V = 512
GATE_RANK = 16
GATE_NORM = 16.0
GLA_DQ = GLA_HEADS * HEAD_K
GLA_DV = GLA_HEADS * HEAD_V
GLA_MAIN = 2 * GLA_DQ + 2 * GLA_DV
D_FF = 5632

LANES = 128
SUBLANES = 8
NORM_ROWS = 16
NORM_UNROLL = 8
MIB = 1024 * 1024
LOG2E = 1.4426950408889634
GELU_C1 = 0.7978845608028654
GELU_C2 = 0.044715 * GELU_C1

MATMUL_TM = 1024
MATMUL_SUB_ROWS = 256
MATMUL_TN = 1024
VMEM_WINDOW_MIB = 46
FFN_TM = 1024
FFN_TF = 512
FFN_CAST_TF = 256
RGLRU_TC = 512
GLA_SEQS = 4


def _params(dims, vmem_mib):
    return pltpu.CompilerParams(dimension_semantics=dims, vmem_limit_bytes=vmem_mib * MIB)


def _rms(x, w):
    ms = jnp.mean(x * x, axis=-1, keepdims=True)
    return x * lax.rsqrt(ms + EPS) * w


def _log_sigmoid(x):
    return jnp.minimum(x, 0.0) - jnp.log(1.0 + jnp.exp(-jnp.abs(x)))


def _split3_bf16(x):
    hi = x.astype(BF16)
    r1 = x - hi.astype(F32)
    mid = r1.astype(BF16)
    lo = (r1 - mid.astype(F32)).astype(BF16)
    return hi, mid, lo


def _sqrt_nonneg(y):
    return jnp.exp2((0.5 * LOG2E) * jnp.log(y))


def _gelu_tanh(x):
    inner = x * (GELU_C1 + GELU_C2 * (x * x))
    return (0.5 * x) * (1.0 + jnp.tanh(inner))


def _norm_rows(src_ref, nw_ref, dst_ref, rows, dtype):
    nw = nw_ref[...]

    def body(r, carry):
        sl = pl.ds(pl.multiple_of(r * NORM_ROWS, NORM_ROWS), NORM_ROWS)
        dst_ref[sl, :] = _rms(src_ref[sl, :], nw).astype(dtype)
        return carry

    lax.fori_loop(0, rows // NORM_ROWS, body, 0, unroll=NORM_UNROLL)


def _norm_rows_inplace(ref, nw_ref, scale_ref, rows):
    def scales(r, carry):
        sl = pl.ds(pl.multiple_of(r * NORM_ROWS, NORM_ROWS), NORM_ROWS)
        x = ref[sl, :]
        scale_ref[sl, :] = lax.rsqrt(jnp.mean(x * x, axis=-1, keepdims=True) + EPS)
        return carry

    lax.fori_loop(0, rows // NORM_ROWS, scales, 0, unroll=NORM_UNROLL)
    nw = nw_ref[...]

    def apply(r, carry):
        sl = pl.ds(pl.multiple_of(r * NORM_ROWS, NORM_ROWS), NORM_ROWS)
        ref[sl, :] = ref[sl, :] * scale_ref[sl, :] * nw
        return carry

    lax.fori_loop(0, rows // NORM_ROWS, apply, 0, unroll=NORM_UNROLL)


def _norm_matmul_kernel(x_ref, nw_ref, w_ref, w2_ref, o_ref, o2_ref, hn_ref, *, tm):
    @pl.when(pl.program_id(1) == 0)
    def _():
        _norm_rows(x_ref, nw_ref, hn_ref, tm, BF16)
        if w2_ref is not None:
            o2_ref[...] = jnp.dot(hn_ref[...], w2_ref[...], preferred_element_type=F32)

    o_ref[...] = jnp.dot(hn_ref[...], w_ref[...], preferred_element_type=F32)


def _norm_matmul_plain_kernel(x_ref, nw_ref, w_ref, o_ref, hn_ref, *, tm):
    _norm_matmul_kernel(x_ref, nw_ref, w_ref, None, o_ref, None, hn_ref, tm=tm)


def _norm_matmul_cast_kernel(x_ref, nw_ref, w_ref, o_ref, wo_ref, hn_ref, *, tm):
    wo_ref[...] = w_ref[...].astype(BF16)
    _norm_matmul_kernel(x_ref, nw_ref, wo_ref, None, o_ref, None, hn_ref, tm=tm)


def _norm_matmul_cast2_kernel(x_ref, nw_ref, w_ref, w2_ref, o_ref, o2_ref, wo_ref, w2o_ref, hn_ref, *, tm, side):
    wo_ref[...] = w_ref[...].astype(BF16)

    @pl.when(pl.program_id(1) == 0)
    def _():
        lane = lax.broadcasted_iota(jnp.int32, w2_ref.shape, 1)
        w2o_ref[...] = jnp.where(lane < side, w2_ref[...], 0.0).astype(BF16)

    _norm_matmul_kernel(x_ref, nw_ref, wo_ref, w2o_ref, o_ref, o2_ref, hn_ref, tm=tm)


def _norm_matmul(x, nws, layer, n, side=0, w_bf16=None, w_f32=None, j=None):
    m, d = x.shape
    tm = min(m, MATMUL_TM)
    tn = MATMUL_TN
    w_buffers = 2
    if w_bf16 is not None:
        def windows_mib(rows, cols, wbuf):
            return (2 * rows * d * 4 + rows * d * 2 + wbuf * d * cols * 2 + 2 * rows * cols * 4) / MIB
        if m > tm and windows_mib(tm // 2, n, 1) <= VMEM_WINDOW_MIB:
            tm, tn, w_buffers = tm // 2, n, 1
        else:
            tn = max(c for c in range(MATMUL_TN, n + 1, MATMUL_TN // 2)
                     if n % c == 0 and windows_mib(tm, c, 2) <= VMEM_WINDOW_MIB)
    grid = (m // tm, n // tn)
    x_spec = pl.BlockSpec((tm, d), lambda i, c: (i, 0))
    nw_spec = pl.BlockSpec((None, 1, d), lambda i, c: (layer, 0, 0))
    w_spec = pl.BlockSpec((d, tn), lambda i, c: (0, c), pipeline_mode=pl.Buffered(w_buffers))
    w2_spec = pl.BlockSpec((d, LANES), lambda i, c: (0, 0))
    o_spec = pl.BlockSpec((tm, tn), lambda i, c: (i, c))
    o2_spec = pl.BlockSpec((tm, LANES), lambda i, c: (i, 0))
    o_shape = jax.ShapeDtypeStruct((m, n), F32)
    o2_shape = jax.ShapeDtypeStruct((m, LANES), F32)
    scratch = [pltpu.VMEM((tm, d), BF16)]
    cp = _params(("parallel", "arbitrary"), 56)
    if w_bf16 is not None:
        w, w2 = w_bf16
        if not side:
            return pl.pallas_call(
                functools.partial(_norm_matmul_plain_kernel, tm=tm), grid=grid,
                in_specs=[x_spec, nw_spec, w_spec], out_specs=o_spec, out_shape=o_shape,
                scratch_shapes=scratch, compiler_params=cp, name="norm_matmul")(x, nws, w)
        return pl.pallas_call(
            functools.partial(_norm_matmul_kernel, tm=tm), grid=grid,
            in_specs=[x_spec, nw_spec, w_spec, w2_spec], out_specs=[o_spec, o2_spec],
            out_shape=[o_shape, o2_shape], scratch_shapes=scratch, compiler_params=cp,
            name="norm_matmul2")(x, nws, w, w2)
    assert m == tm, "the casting variant rewrites the bf16 weights once per row block"
    wf_spec = pl.BlockSpec((None, d, tn), lambda i, c: (j, 0, c))
    w_shape = jax.ShapeDtypeStruct((d, n), BF16)
    if not side:
        out, w = pl.pallas_call(
            functools.partial(_norm_matmul_cast_kernel, tm=tm), grid=grid,
            in_specs=[x_spec, nw_spec, wf_spec], out_specs=[o_spec, w_spec], out_shape=[o_shape, w_shape],
            scratch_shapes=scratch, compiler_params=cp, name="norm_matmul_cast")(x, nws, w_f32)
        return out, (w, None)
    out, out2, w, w2 = pl.pallas_call(
        functools.partial(_norm_matmul_cast2_kernel, tm=tm, side=side), grid=grid,
        in_specs=[x_spec, nw_spec, wf_spec, pl.BlockSpec((None, d, LANES), lambda i, c: (j, 0, n // LANES))],
        out_specs=[o_spec, o2_spec, w_spec, w2_spec],
        out_shape=[o_shape, o2_shape, w_shape, jax.ShapeDtypeStruct((d, LANES), BF16)],
        scratch_shapes=scratch, compiler_params=cp, name="norm_matmul2_cast")(x, nws, w_f32, w_f32)
    return (out, out2), (w, w2)


def _matmul_res_kernel(a_ref, w_ref, x_ref, o_ref):
    rows = a_ref.shape[0]
    sub = min(rows, MATMUL_SUB_ROWS)

    def body(r, carry):
        sl = pl.ds(pl.multiple_of(r * sub, sub), sub)
        o_ref[sl, :] = x_ref[sl, :] + jnp.dot(a_ref[sl, :], w_ref[...], preferred_element_type=F32)
        return carry

    lax.fori_loop(0, rows // sub, body, 0)


def _matmul_res_cast_kernel(a_ref, w_ref, x_ref, o_ref, wo_ref):
    wo_ref[...] = w_ref[...].astype(BF16)
    _matmul_res_kernel(a_ref, wo_ref, x_ref, o_ref)


def _matmul_res(a, x, w_bf16=None, w_f32=None, j=None):
    m, k = a.shape
    n = x.shape[1]
    tm = min(m, MATMUL_TM)
    tn = n if w_bf16 is not None else MATMUL_TN
    grid = (m // tm, n // tn)
    a_spec = pl.BlockSpec((tm, k), lambda i, c: (i, 0))
    w_spec = pl.BlockSpec((k, tn), lambda i, c: (0, c))
    xo_spec = pl.BlockSpec((tm, tn), lambda i, c: (i, c))
    o_shape = jax.ShapeDtypeStruct((m, n), F32)
    cp = _params(("parallel", "arbitrary"), 56)
    if w_bf16 is not None:
        w_res_spec = pl.BlockSpec((k, n), lambda i, c: (0, 0), pipeline_mode=pl.Buffered(1))
        return pl.pallas_call(
            _matmul_res_kernel, grid=grid, in_specs=[a_spec, w_res_spec, xo_spec], out_specs=xo_spec,
            out_shape=o_shape, compiler_params=cp, name="matmul_res")(a, w_bf16, x)
    assert m == tm, "the casting variant rewrites the bf16 weight once per row block"
    return pl.pallas_call(
        _matmul_res_cast_kernel, grid=grid,
        in_specs=[a_spec, pl.BlockSpec((None, k, tn), lambda i, c: (j, 0, c)), xo_spec],
        out_specs=[xo_spec, w_spec], out_shape=[o_shape, jax.ShapeDtypeStruct((k, n), BF16)],
        compiler_params=cp, name="matmul_res_cast")(a, w_f32, x)


def _ffn_kernel(x_ref, nw_ref, wg_ref, wu_ref, wd_ref, fw_ref, o_ref, hn_ref, scale_ref, *, tm, nf, final):
    f = pl.program_id(1)

    @pl.when(f == 0)
    def _():
        _norm_rows(x_ref, nw_ref, hn_ref, tm, BF16)
        o_ref[...] = x_ref[...]

    hn = hn_ref[...]
    gt = jnp.dot(hn, wg_ref[...], preferred_element_type=F32)
    up = jnp.dot(hn, wu_ref[...], preferred_element_type=F32)
    act = ((gt * jax.nn.sigmoid(gt)) * up).astype(BF16)
    o_ref[...] += jnp.dot(act, wd_ref[...], preferred_element_type=F32)

    if final:
        @pl.when(f == nf - 1)
        def _():
            _norm_rows_inplace(o_ref, fw_ref, scale_ref, tm)


def _ffn_cast_kernel(x_ref, nw_ref, wg_ref, wu_ref, wd_ref, fw_ref, o_ref, wgo_ref, wuo_ref, wdo_ref, hn_ref, scale_ref,
                     **kw):
    wgo_ref[...] = wg_ref[...].astype(BF16)
    wuo_ref[...] = wu_ref[...].astype(BF16)
    wdo_ref[...] = wd_ref[...].astype(BF16)
    _ffn_kernel(x_ref, nw_ref, wgo_ref, wuo_ref, wdo_ref, fw_ref, o_ref, hn_ref, scale_ref, **kw)


def _ffn(x, nws, layer, fw, final, w_bf16=None, w_f32=None):
    m, d = x.shape
    tm = min(m, FFN_TM)
    tf = FFN_TF if w_bf16 is not None else FFN_CAST_TF
    nf = D_FF // tf
    kw = dict(tm=tm, nf=nf, final=final)
    x_spec = pl.BlockSpec((tm, d), lambda i, f: (i, 0), pipeline_mode=pl.Buffered(1))
    nw_spec = pl.BlockSpec((None, 1, d), lambda i, f: (layer, 0, 0))
    fw_spec = pl.BlockSpec((1, d), lambda i, f: (0, 0))
    o_spec = pl.BlockSpec((tm, d), lambda i, f: (i, 0))
    o_shape = jax.ShapeDtypeStruct((m, d), F32)
    scratch = [pltpu.VMEM((tm, d), BF16), pltpu.VMEM((tm, 1), F32)]
    cp = _params(("parallel", "arbitrary"), 56)
    up_spec = pl.BlockSpec((d, tf), lambda i, f: (0, f))
    down_spec = pl.BlockSpec((tf, d), lambda i, f: (f, 0))
    if w_bf16 is not None:
        return pl.pallas_call(
            functools.partial(_ffn_kernel, **kw), grid=(m // tm, nf),
            in_specs=[x_spec, nw_spec, up_spec, up_spec, down_spec, fw_spec],
            out_specs=o_spec, out_shape=o_shape, scratch_shapes=scratch,
            compiler_params=cp, name="ffn")(x, nws, *w_bf16, fw)
    assert m == tm, "the casting variant rewrites the bf16 weights once per row block"
    w_ups, w_downs = w_f32
    out, wg, wu, wd = pl.pallas_call(
        functools.partial(_ffn_cast_kernel, **kw), grid=(1, nf),
        in_specs=[x_spec, nw_spec,
                  pl.BlockSpec((None, d, tf), lambda i, f: (layer, 0, f)),
                  pl.BlockSpec((None, d, tf), lambda i, f: (layer, 0, f + nf)),
                  pl.BlockSpec((None, tf, d), lambda i, f: (layer, f, 0)),
                  fw_spec],
        out_specs=[o_spec, up_spec, up_spec, down_spec],
        out_shape=[o_shape, jax.ShapeDtypeStruct((d, D_FF), BF16), jax.ShapeDtypeStruct((d, D_FF), BF16),
                   jax.ShapeDtypeStruct((D_FF, d), BF16)],
        scratch_shapes=scratch, compiler_params=cp, name="ffn_cast")(x, nws, w_ups, w_ups, w_downs, fw)
    return out, (wg, wu, wd)


def _rglru_kernel(gate_ref, u_ref, h0_ref, c0_ref, cw_ref, cb_ref, wa_ref, ba_ref, wx_ref, bx_ref, lam_ref,
                  hg_ref, hout_ref, cout_ref, uext_ref, *, tc, nt):
    t = pl.program_id(2)
    pad = SUBLANES

    @pl.when(t == 0)
    def _():
        uext_ref[...] = jnp.zeros((pad, RG_BW), F32)
        uext_ref[pad - (CONV_W - 1):pad, :] = c0_ref[...]
        hout_ref[...] = h0_ref[...]

    u = u_ref[...]
    ext = jnp.concatenate([uext_ref[...], u], axis=0)
    cw = cw_ref[...]
    acc = pltpu.roll(ext, 3, 0)[pad:, :] * cw[0:1, :]
    acc = acc + pltpu.roll(ext, 2, 0)[pad:, :] * cw[1:2, :]
    acc = acc + pltpu.roll(ext, 1, 0)[pad:, :] * cw[2:3, :]
    acc = acc + u * cw[3:4, :]
    uc = cb_ref[...] + acc
    ucb = uc.astype(BF16)
    r = jax.nn.sigmoid(jnp.dot(ucb, wa_ref[...], preferred_element_type=F32) + ba_ref[...])
    i = jax.nn.sigmoid(jnp.dot(ucb, wx_ref[...], preferred_element_type=F32) + bx_ref[...])
    a = jnp.exp2(r * ((RG_C * LOG2E) * _log_sigmoid(lam_ref[...])))
    b = _sqrt_nonneg(1.0 - a * a) * (i * uc)

    groups = tc // SUBLANES
    a3 = a.reshape(groups, SUBLANES, RG_BW)
    b3 = b.reshape(groups, SUBLANES, RG_BW)
    row = lax.broadcasted_iota(jnp.int32, (groups, SUBLANES, RG_BW), 1)
    for s in (1, 2, 4):
        keep = row >= s
        a_prev = pltpu.roll(a3, s, 1)
        b_prev = pltpu.roll(b3, s, 1)
        b3 = jnp.where(keep, b3 + a3 * b_prev, b3)
        a3 = jnp.where(keep, a3 * a_prev, a3)
    h_prev = jnp.broadcast_to(hout_ref[...], (SUBLANES, RG_BW))
    hs = []
    for g in range(groups):
        h_g = a3[g] * h_prev + b3[g]
        hs.append(h_g)
        h_prev = jnp.broadcast_to(h_g[SUBLANES - 1:SUBLANES, :], (SUBLANES, RG_BW))
    h = jnp.concatenate(hs, axis=0)
    hout_ref[...] = hs[-1][SUBLANES - 1:SUBLANES, :]
    hg_ref[...] = (h * _gelu_tanh(gate_ref[...])).astype(BF16)

    uext_ref[...] = u[tc - pad:, :]

    @pl.when(t == nt - 1)
    def _():
        cout_ref[...] = uext_ref[pad - (CONV_W - 1):pad, :]


def _rglru_core(gu, nseq, tlen, j, h_all, c_all, w):
    m = nseq * tlen
    tc = min(tlen, RGLRU_TC)
    nt = tlen // tc
    bw = RG_BW
    row = lambda b, n, t: b * nt + t
    vec = pl.BlockSpec((None, 1, bw), lambda b, n, t: (j, 0, n))
    gate_spec = pl.BlockSpec((tc, bw), lambda b, n, t: (row(b, n, t), n))
    u_spec = pl.BlockSpec((tc, bw), lambda b, n, t: (row(b, n, t), RG_BLOCKS + n))
    hin_spec = pl.BlockSpec((None, None, 1, bw), lambda b, n, t: (j, b, 0, n))
    cin_spec = pl.BlockSpec((None, None, CONV_W - 1, bw), lambda b, n, t: (j, b, 0, n))
    hout_spec = pl.BlockSpec((None, 1, bw), lambda b, n, t: (b, 0, n))
    cout_spec = pl.BlockSpec((None, CONV_W - 1, bw), lambda b, n, t: (b, 0, n))
    w_spec = pl.BlockSpec((None, None, bw, bw), lambda b, n, t: (j, n, 0, 0))
    return pl.pallas_call(
        functools.partial(_rglru_kernel, tc=tc, nt=nt),
        grid=(nseq, RG_BLOCKS, nt),
        in_specs=[gate_spec, u_spec, hin_spec, cin_spec,
                  pl.BlockSpec((None, CONV_W, bw), lambda b, n, t: (j, 0, n)), vec,
                  w_spec, vec, w_spec, vec, vec],
        out_specs=[pl.BlockSpec((tc, bw), lambda b, n, t: (row(b, n, t), n)), hout_spec, cout_spec],
        out_shape=[jax.ShapeDtypeStruct((m, D_RNN), BF16),
                   jax.ShapeDtypeStruct((nseq, 1, D_RNN), F32),
                   jax.ShapeDtypeStruct((nseq, CONV_W - 1, D_RNN), F32)],
        scratch_shapes=[pltpu.VMEM((SUBLANES, bw), F32)],
        compiler_params=_params(("parallel", "parallel", "arbitrary"), 32),
        name="rglru_core")(gu, gu, h_all, c_all, w["rg_conv_w"], w["rg_conv_b"], w["rg_w_a"], w["rg_b_a"],
                           w["rg_w_x"], w["rg_b_x"], w["rg_lambda"])


def _gla_kernel(q_ref, k_ref, v_ref, g_ref, glr_ref, s0_ref, w2_ref, bgk_ref, gnw_ref,
                o_ref, sout_ref, st_ref, *, nt, nb):
    t = pl.program_id(1)

    @pl.when(t == 0)
    def _():
        for s in range(nb):
            for h in range(GLA_HEADS):
                st_ref[s, h] = s0_ref[s, h].T

    c = CHUNK
    glr = glr_ref[...].reshape(nb * c, LANES).astype(BF16)
    gk_all = jnp.dot(glr, w2_ref[...], preferred_element_type=F32) + bgk_ref[...]
    gk_all = _log_sigmoid(gk_all) * (1.0 / GATE_NORM)
    rows = lax.broadcasted_iota(jnp.int32, (c, c), 0)
    cols = lax.broadcasted_iota(jnp.int32, (c, c), 1)
    tri = rows >= cols
    tri_b = tri.astype(BF16)
    gnw = gnw_ref[...]
    nt_dims = (((1,), (1,)), ((), ()))
    tn_dims = (((0,), (0,)), ((), ()))
    prep = []
    for s in range(nb):
        gk = gk_all[s * c:(s + 1) * c, :]
        bcum = sum(jnp.dot(tri_b, piece, preferred_element_type=F32) for piece in _split3_bf16(gk))
        blast = bcum[c - 1:c, :]
        k = k_ref[s]
        qe = ((q_ref[s] * (HEAD_K ** -0.5)) * jnp.exp(bcum)).astype(BF16)
        ke = (k * jnp.exp(-bcum)).astype(BF16)
        kd = (k * jnp.exp(blast - bcum)).astype(BF16)
        prep.append((qe, ke, kd, jnp.exp(blast)))
    for h in range(GLA_HEADS):
        ks = slice(h * HEAD_K, (h + 1) * HEAD_K)
        vs = slice(h * HEAD_V, (h + 1) * HEAD_V)
        for s in range(nb):
            qe, ke, kd, decay = prep[s]
            v_h = v_ref[s, :, vs].astype(BF16)
            att = lax.dot_general(qe[:, ks], ke[:, ks], nt_dims, preferred_element_type=F32)
            att = jnp.where(tri, att, 0.0).astype(BF16)
            st = st_ref[s, h]
            o = jnp.dot(att, v_h, preferred_element_type=F32)
            o = o + lax.dot_general(qe[:, ks], st.astype(BF16), nt_dims, preferred_element_type=F32)
            st_ref[s, h] = st * decay[:, ks] + lax.dot_general(v_h, kd[:, ks], tn_dims,
                                                               preferred_element_type=F32)
            on = o * lax.rsqrt(jnp.mean(o * o, axis=-1, keepdims=True) + EPS) * gnw
            g_h = g_ref[s, :, vs]
            o_ref[s, :, vs] = (on * (g_h * jax.nn.sigmoid(g_h))).astype(BF16)

    @pl.when(t == nt - 1)
    def _():
        for s in range(nb):
            for h in range(GLA_HEADS):
                sout_ref[s, h] = st_ref[s, h].T


def _gla_aliased_kernel(*refs, **kw):
    _gla_kernel(*refs[:9], *refs[10:], **kw)


def _gla_core(qkvg, glr, nseq, tlen, j, s_all, w, s_stack=None):
    c = CHUNK
    nt = tlen // c
    nb = GLA_SEQS
    n_gla = s_all.shape[0]
    qkvg = qkvg.reshape(nseq, tlen, GLA_MAIN)
    glr = glr.reshape(nseq, tlen, LANES)
    in_specs = [pl.BlockSpec((nb, c, GLA_DQ), lambda b, t: (b, t, 0)),
                pl.BlockSpec((nb, c, GLA_DQ), lambda b, t: (b, t, 1)),
                pl.BlockSpec((nb, c, GLA_DV), lambda b, t: (b, t, 1)),
                pl.BlockSpec((nb, c, GLA_DV), lambda b, t: (b, t, 2)),
                pl.BlockSpec((nb, c, LANES), lambda b, t: (b, t, 0)),
                pl.BlockSpec((None, nb, GLA_HEADS, HEAD_K, HEAD_V), lambda b, t: (j, b, 0, 0, 0),
                             pipeline_mode=pl.Buffered(1)),
                pl.BlockSpec((None, LANES, GLA_DQ), lambda b, t: (j, 0, 0)),
                pl.BlockSpec((None, 1, GLA_DQ), lambda b, t: (j, 0, 0)),
                pl.BlockSpec((None, 1, HEAD_V), lambda b, t: (j, 0, 0))]
    args = [qkvg, qkvg, qkvg, qkvg, glr, s_all, w["gla_w_gk2"], w["gla_b_gk"], w["gla_norm_w"]]
    body, aliases = _gla_kernel, {}
    if s_stack is not None:
        in_specs.append(pl.BlockSpec(memory_space=pl.ANY))
        args.append(s_stack)
        body, aliases = _gla_aliased_kernel, {len(args) - 1: 1}
    og, s_stack = pl.pallas_call(
        functools.partial(body, nt=nt, nb=nb),
        grid=(nseq // nb, nt),
        in_specs=in_specs,
        out_specs=[pl.BlockSpec((nb, c, GLA_DV), lambda b, t: (b, t, 0)),
                   pl.BlockSpec((None, nb, GLA_HEADS, HEAD_K, HEAD_V), lambda b, t: (j, b, 0, 0, 0))],
        out_shape=[jax.ShapeDtypeStruct((nseq, tlen, GLA_DV), BF16),
                   jax.ShapeDtypeStruct((n_gla, nseq, GLA_HEADS, HEAD_K, HEAD_V), F32)],
        scratch_shapes=[pltpu.VMEM((nb, GLA_HEADS, HEAD_V, HEAD_K), F32)],
        input_output_aliases=aliases,
        compiler_params=_params(("parallel", "arbitrary"), 56),
        name="gla_core")(*args)
    return og.reshape(nseq * tlen, GLA_DV), s_stack


def _trunk(x, h_all, conv_all, s_all, w, bf16=None):
    nseq, tlen, d = x.shape
    x = x.reshape(nseq * tlen, d)
    h_all = h_all[:, :, None, :]
    cast = bf16 is None
    made = []
    hs, cs, s_stack = [], [], None
    for layer in range(DEPTH):
        j = layer // 2
        wl = {} if cast else bf16[layer]
        if layer % 2 == 0:
            if cast:
                gu, wl["in"] = _norm_matmul(x, w["norm_mix"], layer, 2 * D_RNN, w_f32=w["rg_w_in"], j=j)
            else:
                gu = _norm_matmul(x, w["norm_mix"], layer, 2 * D_RNN, w_bf16=wl["in"])
            mixed, h_new, c_new = _rglru_core(gu, nseq, tlen, j, h_all, conv_all, w)
            hs.append(h_new[:, 0, :])
            cs.append(c_new)
            w_out = w["rg_w_out"]
        else:
            if cast:
                (qkvg, glr), wl["in"] = _norm_matmul(x, w["norm_mix"], layer, GLA_MAIN, side=GATE_RANK,
                                                     w_f32=w["gla_w_in"], j=j)
            else:
                qkvg, glr = _norm_matmul(x, w["norm_mix"], layer, GLA_MAIN, side=GATE_RANK, w_bf16=wl["in"])
            mixed, s_stack = _gla_core(qkvg, glr, nseq, tlen, j, s_all, w, s_stack)
            w_out = w["gla_w_out"]
        final = layer == DEPTH - 1
        if cast:
            x, wl["out"] = _matmul_res(mixed, x, w_f32=w_out, j=j)
            x, wl["ffn"] = _ffn(x, w["norm_ffn"], layer, w["norm_final"], final,
                                w_f32=(w["ffn_w_up"], w["ffn_w_down"]))
            made.append(wl)
        else:
            x = _matmul_res(mixed, x, w_bf16=wl["out"])
            x = _ffn(x, w["norm_ffn"], layer, w["norm_final"], final, w_bf16=wl["ffn"])
    return (x.reshape(nseq, tlen, d), jnp.stack(hs), jnp.stack(cs), s_stack), made


def _prepare_weights(norm_mix, norm_ffn, norm_final, rg_w_in, rg_conv_w, rg_conv_b, rg_w_a, rg_b_a, rg_w_x, rg_b_x,
                     rg_lambda, rg_w_out, gla_w_in, gla_w_gk2, gla_b_gk, gla_norm_w, gla_w_out, ffn_w_up, ffn_w_down):
    glr_pad = LANES - GATE_RANK
    return {
        "norm_mix": norm_mix[:, None, :],
        "norm_ffn": norm_ffn[:, None, :],
        "norm_final": norm_final[None, :],
        "rg_w_in": rg_w_in,
        "rg_conv_w": rg_conv_w,
        "rg_conv_b": rg_conv_b[:, None, :],
        "rg_w_a": rg_w_a.astype(BF16),
        "rg_b_a": rg_b_a[:, None, :],
        "rg_w_x": rg_w_x.astype(BF16),
        "rg_b_x": rg_b_x[:, None, :],
        "rg_lambda": rg_lambda[:, None, :],
        "rg_w_out": rg_w_out,
        "gla_w_in": gla_w_in,
        "gla_w_gk2": jnp.pad(gla_w_gk2, ((0, 0), (0, glr_pad), (0, 0))).astype(BF16),
        "gla_b_gk": gla_b_gk[:, None, :],
        "gla_norm_w": gla_norm_w[:, None, :],
        "gla_w_out": gla_w_out,
        "ffn_w_up": ffn_w_up,
        "ffn_w_down": ffn_w_down,
    }


def kernel(x_prompt, x_sample, state_rglru_h, state_rglru_conv, state_gla, norm_mix, norm_ffn, norm_final, rg_w_in, rg_conv_w, rg_conv_b, rg_w_a, rg_b_a, rg_w_x, rg_b_x, rg_lambda, rg_w_out, gla_w_in, gla_w_gk2, gla_b_gk, gla_norm_w, gla_w_out, ffn_w_up, ffn_w_down):
    w = _prepare_weights(norm_mix, norm_ffn, norm_final, rg_w_in, rg_conv_w, rg_conv_b, rg_w_a, rg_b_a, rg_w_x,
                         rg_b_x, rg_lambda, rg_w_out, gla_w_in, gla_w_gk2, gla_b_gk, gla_norm_w, gla_w_out,
                         ffn_w_up, ffn_w_down)
    bp = x_prompt.shape[0]
    n_rg = state_rglru_h.shape[0]
    n_gla = state_gla.shape[0]
    h0 = jnp.zeros((n_rg, bp, D_RNN), F32)
    c0 = jnp.zeros((n_rg, bp, CONV_W - 1, D_RNN), F32)
    s0 = jnp.zeros((n_gla, bp, GLA_HEADS, HEAD_K, HEAD_V), F32)
    (y_s, h_s, c_s, s_s), bf16 = _trunk(x_sample, state_rglru_h, state_rglru_conv, state_gla, w)
    (y_p, h_p, c_p, s_p), _ = _trunk(x_prompt, h0, c0, s0, w, bf16)
    return (y_p, y_s, h_p, c_p, s_p, h_s, c_s, s_s)
```

```python
import functools

import jax
import jax.numpy as jnp
from jax import lax
from jax.experimental import pallas as pl
from jax.experimental.pallas import tpu as pltpu

F32 = jnp.float32
BF16 = jnp.bfloat16

D_MODEL = 2048
DEPTH = 4
CHUNK = 64
EPS = 1e-6
D_RNN = D_MODEL
RG_BLOCKS = 8
RG_BW = D_RNN // RG_BLOCKS
CONV_W = 4
RG_C = 8.0
GLA_HEADS = 4
HEAD_K = 256
HEAD_V = 512
GATE_RANK = 16
GATE_NORM = 16.0
GLA_DQ = GLA_HEADS * HEAD_K
GLA_DV = GLA_HEADS * HEAD_V
GLA_MAIN = 2 * GLA_DQ + 2 * GLA_DV
D_FF = 5632

LANES = 128
SUBLANES = 8
NORM_ROWS = 16
NORM_UNROLL = 16
MIB = 1024 * 1024
LOG2E = 1.4426950408889634
GELU_C1 = 0.7978845608028654
GELU_C2 = 0.044715 * GELU_C1

MATMUL_TM = 1024
MATMUL_TN = 1024
VMEM_WINDOW_MIB = 46
FFN_TM = 1024
FFN_TF = 512
FFN_CAST_TF = 256
RGLRU_TC = 512
GLA_SEQS = 4


def _params(dims, vmem_mib):
    return pltpu.CompilerParams(dimension_semantics=dims, vmem_limit_bytes=vmem_mib * MIB)


def _rms(x, w):
    ms = jnp.mean(x * x, axis=-1, keepdims=True)
    return x * lax.rsqrt(ms + EPS) * w


def _log_sigmoid(x):
    return jnp.minimum(x, 0.0) - jnp.log(1.0 + jnp.exp(-jnp.abs(x)))


def _split3_bf16(x):
    hi = x.astype(BF16)
    r1 = x - hi.astype(F32)
    mid = r1.astype(BF16)
    lo = (r1 - mid.astype(F32)).astype(BF16)
    return hi, mid, lo


def _sqrt_nonneg(y):
    return jnp.exp2((0.5 * LOG2E) * jnp.log(y))


def _gelu_tanh(x):
    inner = x * (GELU_C1 + GELU_C2 * (x * x))
    return (0.5 * x) * (1.0 + jnp.tanh(inner))


def _norm_rows(src_ref, nw_ref, dst_ref, rows, dtype):
    nw = nw_ref[...]

    def body(r, carry):
        sl = pl.ds(pl.multiple_of(r * NORM_ROWS, NORM_ROWS), NORM_ROWS)
        dst_ref[sl, :] = _rms(src_ref[sl, :], nw).astype(dtype)
        return carry

    lax.fori_loop(0, rows // NORM_ROWS, body, 0, unroll=NORM_UNROLL)


def _norm_rows_inplace(ref, nw_ref, scale_ref, rows):
    def scales(r, carry):
        sl = pl.ds(pl.multiple_of(r * NORM_ROWS, NORM_ROWS), NORM_ROWS)
        x = ref[sl, :]
        scale_ref[sl, :] = lax.rsqrt(jnp.mean(x * x, axis=-1, keepdims=True) + EPS)
        return carry

    lax.fori_loop(0, rows // NORM_ROWS, scales, 0, unroll=NORM_UNROLL)
    nw = nw_ref[...]

    def apply(r, carry):
        sl = pl.ds(pl.multiple_of(r * NORM_ROWS, NORM_ROWS), NORM_ROWS)
        ref[sl, :] = ref[sl, :] * scale_ref[sl, :] * nw
        return carry

    lax.fori_loop(0, rows // NORM_ROWS, apply, 0, unroll=NORM_UNROLL)


def _norm_matmul_kernel(x_ref, nw_ref, w_ref, w2_ref, o_ref, o2_ref, hn_ref, *, tm):
    @pl.when(pl.program_id(1) == 0)
    def _():
        _norm_rows(x_ref, nw_ref, hn_ref, tm, BF16)
        if w2_ref is not None:
            o2_ref[...] = jnp.dot(hn_ref[...], w2_ref[...], preferred_element_type=F32)

    o_ref[...] = jnp.dot(hn_ref[...], w_ref[...], preferred_element_type=F32)


def _norm_matmul_plain_kernel(x_ref, nw_ref, w_ref, o_ref, hn_ref, *, tm):
    _norm_matmul_kernel(x_ref, nw_ref, w_ref, None, o_ref, None, hn_ref, tm=tm)


def _norm_matmul_cast_kernel(x_ref, nw_ref, w_ref, o_ref, wo_ref, hn_ref, *, tm):
    wo_ref[...] = w_ref[...].astype(BF16)
    _norm_matmul_kernel(x_ref, nw_ref, wo_ref, None, o_ref, None, hn_ref, tm=tm)


def _norm_matmul_cast2_kernel(x_ref, nw_ref, w_ref, w2_ref, o_ref, o2_ref, wo_ref, w2o_ref, hn_ref, *, tm, side):
    wo_ref[...] = w_ref[...].astype(BF16)

    @pl.when(pl.program_id(1) == 0)
    def _():
        lane = lax.broadcasted_iota(jnp.int32, w2_ref.shape, 1)
        w2o_ref[...] = jnp.where(lane < side, w2_ref[...], 0.0).astype(BF16)

    _norm_matmul_kernel(x_ref, nw_ref, wo_ref, w2o_ref, o_ref, o2_ref, hn_ref, tm=tm)


def _norm_matmul(x, nws, layer, n, side=0, w_bf16=None, w_f32=None, j=None):
    m, d = x.shape
    tm = min(m, MATMUL_TM)
    tn = MATMUL_TN
    w_buffers = 2
    if w_bf16 is not None:
        def windows_mib(rows, cols, wbuf):
            return (2 * rows * d * 4 + rows * d * 2 + wbuf * d * cols * 2 + 2 * rows * cols * 4) / MIB
        if m > tm and windows_mib(tm // 2, n, 1) <= VMEM_WINDOW_MIB:
            tm, tn, w_buffers = tm // 2, n, 1
        else:
            tn = max(c for c in range(MATMUL_TN, n + 1, MATMUL_TN // 2)
                     if n % c == 0 and windows_mib(tm, c, 2) <= VMEM_WINDOW_MIB)
    grid = (m // tm, n // tn)
    x_spec = pl.BlockSpec((tm, d), lambda i, c: (i, 0))
    nw_spec = pl.BlockSpec((None, 1, d), lambda i, c: (layer, 0, 0))
    w_spec = pl.BlockSpec((d, tn), lambda i, c: (0, c), pipeline_mode=pl.Buffered(w_buffers))
    w2_spec = pl.BlockSpec((d, LANES), lambda i, c: (0, 0))
    o_spec = pl.BlockSpec((tm, tn), lambda i, c: (i, c))
    o2_spec = pl.BlockSpec((tm, LANES), lambda i, c: (i, 0))
    o_shape = jax.ShapeDtypeStruct((m, n), F32)
    o2_shape = jax.ShapeDtypeStruct((m, LANES), F32)
    scratch = [pltpu.VMEM((tm, d), BF16)]
    cp = _params(("parallel", "arbitrary"), 56)
    if w_bf16 is not None:
        w, w2 = w_bf16
        if not side:
            return pl.pallas_call(
                functools.partial(_norm_matmul_plain_kernel, tm=tm), grid=grid,
                in_specs=[x_spec, nw_spec, w_spec], out_specs=o_spec, out_shape=o_shape,
                scratch_shapes=scratch, compiler_params=cp, name="norm_matmul")(x, nws, w)
        return pl.pallas_call(
            functools.partial(_norm_matmul_kernel, tm=tm), grid=grid,
            in_specs=[x_spec, nw_spec, w_spec, w2_spec], out_specs=[o_spec, o2_spec],
            out_shape=[o_shape, o2_shape], scratch_shapes=scratch, compiler_params=cp,
            name="norm_matmul2")(x, nws, w, w2)
    assert m == tm, "the casting variant rewrites the bf16 weights once per row block"
    wf_spec = pl.BlockSpec((None, d, tn), lambda i, c: (j, 0, c))
    w_shape = jax.ShapeDtypeStruct((d, n), BF16)
    if not side:
        out, w = pl.pallas_call(
            functools.partial(_norm_matmul_cast_kernel, tm=tm), grid=grid,
            in_specs=[x_spec, nw_spec, wf_spec], out_specs=[o_spec, w_spec], out_shape=[o_shape, w_shape],
            scratch_shapes=scratch, compiler_params=cp, name="norm_matmul_cast")(x, nws, w_f32)
        return out, (w, None)
    out, out2, w, w2 = pl.pallas_call(
        functools.partial(_norm_matmul_cast2_kernel, tm=tm, side=side), grid=grid,
        in_specs=[x_spec, nw_spec, wf_spec, pl.BlockSpec((None, d, LANES), lambda i, c: (j, 0, n // LANES))],
        out_specs=[o_spec, o2_spec, w_spec, w2_spec],
        out_shape=[o_shape, o2_shape, w_shape, jax.ShapeDtypeStruct((d, LANES), BF16)],
        scratch_shapes=scratch, compiler_params=cp, name="norm_matmul2_cast")(x, nws, w_f32, w_f32)
    return (out, out2), (w, w2)


def _matmul_res_kernel(a_ref, w_ref, x_ref, o_ref):
    o_ref[...] = x_ref[...] + jnp.dot(a_ref[...], w_ref[...], preferred_element_type=F32)


def _matmul_res_cast_kernel(a_ref, w_ref, x_ref, o_ref, wo_ref):
    wo_ref[...] = w_ref[...].astype(BF16)
    _matmul_res_kernel(a_ref, wo_ref, x_ref, o_ref)


def _matmul_res(a, x, w_bf16=None, w_f32=None, j=None):
    m, k = a.shape
    n = x.shape[1]
    tm = min(m, MATMUL_TM)
    tn = n if w_bf16 is not None else MATMUL_TN
    grid = (m // tm, n // tn)
    a_spec = pl.BlockSpec((tm, k), lambda i, c: (i, 0))
    w_spec = pl.BlockSpec((k, tn), lambda i, c: (0, c))
    xo_spec = pl.BlockSpec((tm, tn), lambda i, c: (i, c))
    o_shape = jax.ShapeDtypeStruct((m, n), F32)
    cp = _params(("parallel", "arbitrary"), 56)
    if w_bf16 is not None:
        w_res_spec = pl.BlockSpec((k, n), lambda i, c: (0, 0), pipeline_mode=pl.Buffered(1))
        return pl.pallas_call(
            _matmul_res_kernel, grid=grid, in_specs=[a_spec, w_res_spec, xo_spec], out_specs=xo_spec,
            out_shape=o_shape, compiler_params=cp, name="matmul_res")(a, w_bf16, x)
    assert m == tm, "the casting variant rewrites the bf16 weight once per row block"
    return pl.pallas_call(
        _matmul_res_cast_kernel, grid=grid,
        in_specs=[a_spec, pl.BlockSpec((None, k, tn), lambda i, c: (j, 0, c)), xo_spec],
        out_specs=[xo_spec, w_spec], out_shape=[o_shape, jax.ShapeDtypeStruct((k, n), BF16)],
        compiler_params=cp, name="matmul_res_cast")(a, w_f32, x)


def _ffn_kernel(x_ref, nw_ref, wg_ref, wu_ref, wd_ref, fw_ref, o_ref, hn_ref, scale_ref, *, tm, nf, final):
    f = pl.program_id(1)

    @pl.when(f == 0)
    def _():
        _norm_rows(x_ref, nw_ref, hn_ref, tm, BF16)
        o_ref[...] = x_ref[...]

    hn = hn_ref[...]
    gt = jnp.dot(hn, wg_ref[...], preferred_element_type=F32)
    up = jnp.dot(hn, wu_ref[...], preferred_element_type=F32)
    act = ((gt * jax.nn.sigmoid(gt)) * up).astype(BF16)
    o_ref[...] += jnp.dot(act, wd_ref[...], preferred_element_type=F32)

    if final:
        @pl.when(f == nf - 1)
        def _():
            _norm_rows_inplace(o_ref, fw_ref, scale_ref, tm)


def _ffn_cast_kernel(x_ref, nw_ref, wg_ref, wu_ref, wd_ref, fw_ref, o_ref, wgo_ref, wuo_ref, wdo_ref, hn_ref, scale_ref,
                     **kw):
    wgo_ref[...] = wg_ref[...].astype(BF16)
    wuo_ref[...] = wu_ref[...].astype(BF16)
    wdo_ref[...] = wd_ref[...].astype(BF16)
    _ffn_kernel(x_ref, nw_ref, wgo_ref, wuo_ref, wdo_ref, fw_ref, o_ref, hn_ref, scale_ref, **kw)


def _ffn(x, nws, layer, fw, final, w_bf16=None, w_f32=None):
    m, d = x.shape
    tm = min(m, FFN_TM)
    tf = FFN_TF if w_bf16 is not None else FFN_CAST_TF
    nf = D_FF // tf
    kw = dict(tm=tm, nf=nf, final=final)
    x_spec = pl.BlockSpec((tm, d), lambda i, f: (i, 0), pipeline_mode=pl.Buffered(1))
    nw_spec = pl.BlockSpec((None, 1, d), lambda i, f: (layer, 0, 0))
    fw_spec = pl.BlockSpec((1, d), lambda i, f: (0, 0))
    o_spec = pl.BlockSpec((tm, d), lambda i, f: (i, 0))
    o_shape = jax.ShapeDtypeStruct((m, d), F32)
    scratch = [pltpu.VMEM((tm, d), BF16), pltpu.VMEM((tm, 1), F32)]
    cp = _params(("parallel", "arbitrary"), 56)
    up_spec = pl.BlockSpec((d, tf), lambda i, f: (0, f))
    down_spec = pl.BlockSpec((tf, d), lambda i, f: (f, 0))
    if w_bf16 is not None:
        return pl.pallas_call(
            functools.partial(_ffn_kernel, **kw), grid=(m // tm, nf),
            in_specs=[x_spec, nw_spec, up_spec, up_spec, down_spec, fw_spec],
            out_specs=o_spec, out_shape=o_shape, scratch_shapes=scratch,
            compiler_params=cp, name="ffn")(x, nws, *w_bf16, fw)
    assert m == tm, "the casting variant rewrites the bf16 weights once per row block"
    w_ups, w_downs = w_f32
    out, wg, wu, wd = pl.pallas_call(
        functools.partial(_ffn_cast_kernel, **kw), grid=(1, nf),
        in_specs=[x_spec, nw_spec,
                  pl.BlockSpec((None, d, tf), lambda i, f: (layer, 0, f)),
                  pl.BlockSpec((None, d, tf), lambda i, f: (layer, 0, f + nf)),
                  pl.BlockSpec((None, tf, d), lambda i, f: (layer, f, 0)),
                  fw_spec],
        out_specs=[o_spec, up_spec, up_spec, down_spec],
        out_shape=[o_shape, jax.ShapeDtypeStruct((d, D_FF), BF16), jax.ShapeDtypeStruct((d, D_FF), BF16),
                   jax.ShapeDtypeStruct((D_FF, d), BF16)],
        scratch_shapes=scratch, compiler_params=cp, name="ffn_cast")(x, nws, w_ups, w_ups, w_downs, fw)
    return out, (wg, wu, wd)


def _rglru_kernel(gate_ref, u_ref, h0_ref, c0_ref, cw_ref, cb_ref, wa_ref, ba_ref, wx_ref, bx_ref, lam_ref,
                  hg_ref, hout_ref, cout_ref, uext_ref, *, tc, nt):
    t = pl.program_id(2)
    pad = SUBLANES

    @pl.when(t == 0)
    def _():
        uext_ref[...] = jnp.zeros((pad, RG_BW), F32)
        uext_ref[pad - (CONV_W - 1):pad, :] = c0_ref[...]
        hout_ref[...] = h0_ref[...]

    u = u_ref[...]
    ext = jnp.concatenate([uext_ref[...], u], axis=0)
    cw = cw_ref[...]
    acc = pltpu.roll(ext, 3, 0)[pad:, :] * cw[0:1, :]
    acc = acc + pltpu.roll(ext, 2, 0)[pad:, :] * cw[1:2, :]
    acc = acc + pltpu.roll(ext, 1, 0)[pad:, :] * cw[2:3, :]
    acc = acc + u * cw[3:4, :]
    uc = cb_ref[...] + acc
    ucb = uc.astype(BF16)
    r = jax.nn.sigmoid(jnp.dot(ucb, wa_ref[...], preferred_element_type=F32) + ba_ref[...])
    i = jax.nn.sigmoid(jnp.dot(ucb, wx_ref[...], preferred_element_type=F32) + bx_ref[...])
    a = jnp.exp2(r * ((RG_C * LOG2E) * _log_sigmoid(lam_ref[...])))
    b = _sqrt_nonneg(1.0 - a * a) * (i * uc)

    groups = tc // SUBLANES
    a3 = a.reshape(groups, SUBLANES, RG_BW)
    b3 = b.reshape(groups, SUBLANES, RG_BW)
    row = lax.broadcasted_iota(jnp.int32, (groups, SUBLANES, RG_BW), 1)
    for s in (1, 2, 4):
        keep = row >= s
        a_prev = pltpu.roll(a3, s, 1)
        b_prev = pltpu.roll(b3, s, 1)
        b3 = jnp.where(keep, b3 + a3 * b_prev, b3)
        a3 = jnp.where(keep, a3 * a_prev, a3)
    h_prev = jnp.broadcast_to(hout_ref[...], (SUBLANES, RG_BW))
    hs = []
    for g in range(groups):
        h_g = a3[g] * h_prev + b3[g]
        hs.append(h_g)
        h_prev = jnp.broadcast_to(h_g[SUBLANES - 1:SUBLANES, :], (SUBLANES, RG_BW))
    h = jnp.concatenate(hs, axis=0)
    hout_ref[...] = hs[-1][SUBLANES - 1:SUBLANES, :]
    hg_ref[...] = (h * _gelu_tanh(gate_ref[...])).astype(BF16)

    uext_ref[...] = u[tc - pad:, :]

    @pl.when(t == nt - 1)
    def _():
        cout_ref[...] = uext_ref[pad - (CONV_W - 1):pad, :]


def _rglru_core(gu, nseq, tlen, j, h_all, c_all, w):
    m = nseq * tlen
    tc = min(tlen, RGLRU_TC)
    nt = tlen // tc
    bw = RG_BW
    row = lambda b, n, t: b * nt + t
    vec = pl.BlockSpec((None, 1, bw), lambda b, n, t: (j, 0, n))
    gate_spec = pl.BlockSpec((tc, bw), lambda b, n, t: (row(b, n, t), n))
    u_spec = pl.BlockSpec((tc, bw), lambda b, n, t: (row(b, n, t), RG_BLOCKS + n))
    hin_spec = pl.BlockSpec((None, None, 1, bw), lambda b, n, t: (j, b, 0, n))
    cin_spec = pl.BlockSpec((None, None, CONV_W - 1, bw), lambda b, n, t: (j, b, 0, n))
    hout_spec = pl.BlockSpec((None, 1, bw), lambda b, n, t: (b, 0, n))
    cout_spec = pl.BlockSpec((None, CONV_W - 1, bw), lambda b, n, t: (b, 0, n))
    w_spec = pl.BlockSpec((None, None, bw, bw), lambda b, n, t: (j, n, 0, 0))
    return pl.pallas_call(
        functools.partial(_rglru_kernel, tc=tc, nt=nt),
        grid=(nseq, RG_BLOCKS, nt),
        in_specs=[gate_spec, u_spec, hin_spec, cin_spec,
                  pl.BlockSpec((None, CONV_W, bw), lambda b, n, t: (j, 0, n)), vec,
                  w_spec, vec, w_spec, vec, vec],
        out_specs=[pl.BlockSpec((tc, bw), lambda b, n, t: (row(b, n, t), n)), hout_spec, cout_spec],
        out_shape=[jax.ShapeDtypeStruct((m, D_RNN), BF16),
                   jax.ShapeDtypeStruct((nseq, 1, D_RNN), F32),
                   jax.ShapeDtypeStruct((nseq, CONV_W - 1, D_RNN), F32)],
        scratch_shapes=[pltpu.VMEM((SUBLANES, bw), F32)],
        compiler_params=_params(("parallel", "parallel", "arbitrary"), 32),
        name="rglru_core")(gu, gu, h_all, c_all, w["rg_conv_w"], w["rg_conv_b"], w["rg_w_a"], w["rg_b_a"],
                           w["rg_w_x"], w["rg_b_x"], w["rg_lambda"])


def _gla_kernel(q_ref, k_ref, v_ref, g_ref, glr_ref, s0_ref, w2_ref, bgk_ref, gnw_ref,
                o_ref, sout_ref, st_ref, *, nt, nb):
    t = pl.program_id(1)

    @pl.when(t == 0)
    def _():
        for s in range(nb):
            for h in range(GLA_HEADS):
                st_ref[s, h] = s0_ref[s, h].T

    c = CHUNK
    glr = glr_ref[...].reshape(nb * c, LANES).astype(BF16)
    gk_all = jnp.dot(glr, w2_ref[...], preferred_element_type=F32) + bgk_ref[...]
    gk_all = _log_sigmoid(gk_all) * (1.0 / GATE_NORM)
    rows = lax.broadcasted_iota(jnp.int32, (c, c), 0)
    cols = lax.broadcasted_iota(jnp.int32, (c, c), 1)
    tri = rows >= cols
    tri_b = tri.astype(BF16)
    gnw = gnw_ref[...]
    nt_dims = (((1,), (1,)), ((), ()))
    tn_dims = (((0,), (0,)), ((), ()))
    prep = []
    for s in range(nb):
        gk = gk_all[s * c:(s + 1) * c, :]
        bcum = sum(jnp.dot(tri_b, piece, preferred_element_type=F32) for piece in _split3_bf16(gk))
        blast = bcum[c - 1:c, :]
        k = k_ref[s]
        qe = ((q_ref[s] * (HEAD_K ** -0.5)) * jnp.exp(bcum)).astype(BF16)
        ke = (k * jnp.exp(-bcum)).astype(BF16)
        kd = (k * jnp.exp(blast - bcum)).astype(BF16)
        prep.append((qe, ke, kd, jnp.exp(blast)))
    for h in range(GLA_HEADS):
        ks = slice(h * HEAD_K, (h + 1) * HEAD_K)
        vs = slice(h * HEAD_V, (h + 1) * HEAD_V)
        for s in range(nb):
            qe, ke, kd, decay = prep[s]
            v_h = v_ref[s, :, vs].astype(BF16)
            att = lax.dot_general(qe[:, ks], ke[:, ks], nt_dims, preferred_element_type=F32)
            att = jnp.where(tri, att, 0.0).astype(BF16)
            st = st_ref[s, h]
            o = jnp.dot(att, v_h, preferred_element_type=F32)
            o = o + lax.dot_general(qe[:, ks], st.astype(BF16), nt_dims, preferred_element_type=F32)
            st_ref[s, h] = st * decay[:, ks] + lax.dot_general(v_h, kd[:, ks], tn_dims,
                                                               preferred_element_type=F32)
            on = o * lax.rsqrt(jnp.mean(o * o, axis=-1, keepdims=True) + EPS) * gnw
            g_h = g_ref[s, :, vs]
            o_ref[s, :, vs] = (on * (g_h * jax.nn.sigmoid(g_h))).astype(BF16)

    @pl.when(t == nt - 1)
    def _():
        for s in range(nb):
            for h in range(GLA_HEADS):
                sout_ref[s, h] = st_ref[s, h].T


def _gla_aliased_kernel(*refs, **kw):
    _gla_kernel(*refs[:9], *refs[10:], **kw)


def _gla_core(qkvg, glr, nseq, tlen, j, s_all, w, s_stack=None):
    c = CHUNK
    nt = tlen // c
    nb = GLA_SEQS
    n_gla = s_all.shape[0]
    qkvg = qkvg.reshape(nseq, tlen, GLA_MAIN)
    glr = glr.reshape(nseq, tlen, LANES)
    in_specs = [pl.BlockSpec((nb, c, GLA_DQ), lambda b, t: (b, t, 0)),
                pl.BlockSpec((nb, c, GLA_DQ), lambda b, t: (b, t, 1)),
                pl.BlockSpec((nb, c, GLA_DV), lambda b, t: (b, t, 1)),
                pl.BlockSpec((nb, c, GLA_DV), lambda b, t: (b, t, 2)),
                pl.BlockSpec((nb, c, LANES), lambda b, t: (b, t, 0)),
                pl.BlockSpec((None, nb, GLA_HEADS, HEAD_K, HEAD_V), lambda b, t: (j, b, 0, 0, 0),
                             pipeline_mode=pl.Buffered(1)),
                pl.BlockSpec((None, LANES, GLA_DQ), lambda b, t: (j, 0, 0)),
                pl.BlockSpec((None, 1, GLA_DQ), lambda b, t: (j, 0, 0)),
                pl.BlockSpec((None, 1, HEAD_V), lambda b, t: (j, 0, 0))]
    args = [qkvg, qkvg, qkvg, qkvg, glr, s_all, w["gla_w_gk2"], w["gla_b_gk"], w["gla_norm_w"]]
    body, aliases = _gla_kernel, {}
    if s_stack is not None:
        in_specs.append(pl.BlockSpec(memory_space=pl.ANY))
        args.append(s_stack)
        body, aliases = _gla_aliased_kernel, {len(args) - 1: 1}
    og, s_stack = pl.pallas_call(
        functools.partial(body, nt=nt, nb=nb),
        grid=(nseq // nb, nt),
        in_specs=in_specs,
        out_specs=[pl.BlockSpec((nb, c, GLA_DV), lambda b, t: (b, t, 0)),
                   pl.BlockSpec((None, nb, GLA_HEADS, HEAD_K, HEAD_V), lambda b, t: (j, b, 0, 0, 0))],
        out_shape=[jax.ShapeDtypeStruct((nseq, tlen, GLA_DV), BF16),
                   jax.ShapeDtypeStruct((n_gla, nseq, GLA_HEADS, HEAD_K, HEAD_V), F32)],
        scratch_shapes=[pltpu.VMEM((nb, GLA_HEADS, HEAD_V, HEAD_K), F32)],
        input_output_aliases=aliases,
        compiler_params=_params(("parallel", "arbitrary"), 56),
        name="gla_core")(*args)
    return og.reshape(nseq * tlen, GLA_DV), s_stack


def _trunk(x, h_all, conv_all, s_all, w, bf16=None):
    nseq, tlen, d = x.shape
    x = x.reshape(nseq * tlen, d)
    h_all = h_all[:, :, None, :]
    cast = bf16 is None
    made = []
    hs, cs, s_stack = [], [], None
    for layer in range(DEPTH):
        j = layer // 2
        wl = {} if cast else bf16[layer]
        if layer % 2 == 0:
            if cast:
                gu, wl["in"] = _norm_matmul(x, w["norm_mix"], layer, 2 * D_RNN, w_f32=w["rg_w_in"], j=j)
            else:
                gu = _norm_matmul(x, w["norm_mix"], layer, 2 * D_RNN, w_bf16=wl["in"])
            mixed, h_new, c_new = _rglru_core(gu, nseq, tlen, j, h_all, conv_all, w)
            hs.append(h_new[:, 0, :])
            cs.append(c_new)
            w_out = w["rg_w_out"]
        else:
            if cast:
                (qkvg, glr), wl["in"] = _norm_matmul(x, w["norm_mix"], layer, GLA_MAIN, side=GATE_RANK,
                                                     w_f32=w["gla_w_in"], j=j)
            else:
                qkvg, glr = _norm_matmul(x, w["norm_mix"], layer, GLA_MAIN, side=GATE_RANK, w_bf16=wl["in"])
            mixed, s_stack = _gla_core(qkvg, glr, nseq, tlen, j, s_all, w, s_stack)
            w_out = w["gla_w_out"]
        final = layer == DEPTH - 1
        if cast:
            x, wl["out"] = _matmul_res(mixed, x, w_f32=w_out, j=j)
            x, wl["ffn"] = _ffn(x, w["norm_ffn"], layer, w["norm_final"], final,
                                w_f32=(w["ffn_w_up"], w["ffn_w_down"]))
            made.append(wl)
        else:
            x = _matmul_res(mixed, x, w_bf16=wl["out"])
            x = _ffn(x, w["norm_ffn"], layer, w["norm_final"], final, w_bf16=wl["ffn"])
    return (x.reshape(nseq, tlen, d), jnp.stack(hs), jnp.stack(cs), s_stack), made


def _prepare_weights(norm_mix, norm_ffn, norm_final, rg_w_in, rg_conv_w, rg_conv_b, rg_w_a, rg_b_a, rg_w_x, rg_b_x,
                     rg_lambda, rg_w_out, gla_w_in, gla_w_gk2, gla_b_gk, gla_norm_w, gla_w_out, ffn_w_up, ffn_w_down):
    glr_pad = LANES - GATE_RANK
    return {
        "norm_mix": norm_mix[:, None, :],
        "norm_ffn": norm_ffn[:, None, :],
        "norm_final": norm_final[None, :],
        "rg_w_in": rg_w_in,
        "rg_conv_w": rg_conv_w,
        "rg_conv_b": rg_conv_b[:, None, :],
        "rg_w_a": rg_w_a.astype(BF16),
        "rg_b_a": rg_b_a[:, None, :],
        "rg_w_x": rg_w_x.astype(BF16),
        "rg_b_x": rg_b_x[:, None, :],
        "rg_lambda": rg_lambda[:, None, :],
        "rg_w_out": rg_w_out,
        "gla_w_in": gla_w_in,
        "gla_w_gk2": jnp.pad(gla_w_gk2, ((0, 0), (0, glr_pad), (0, 0))).astype(BF16),
        "gla_b_gk": gla_b_gk[:, None, :],
        "gla_norm_w": gla_norm_w[:, None, :],
        "gla_w_out": gla_w_out,
        "ffn_w_up": ffn_w_up,
        "ffn_w_down": ffn_w_down,
    }


def kernel(x_prompt, x_sample, state_rglru_h, state_rglru_conv, state_gla, norm_mix, norm_ffn, norm_final, rg_w_in, rg_conv_w, rg_conv_b, rg_w_a, rg_b_a, rg_w_x, rg_b_x, rg_lambda, rg_w_out, gla_w_in, gla_w_gk2, gla_b_gk, gla_norm_w, gla_w_out, ffn_w_up, ffn_w_down):
    w = _prepare_weights(norm_mix, norm_ffn, norm_final, rg_w_in, rg_conv_w, rg_conv_b, rg_w_a, rg_b_a, rg_w_x,
                         rg_b_x, rg_lambda, rg_w_out, gla_w_in, gla_w_gk2, gla_b_gk, gla_norm_w, gla_w_out,
                         ffn_w_up, ffn_w_down)
    bp = x_prompt.shape[0]
    n_rg = state_rglru_h.shape[0]
    n_gla = state_gla.shape[0]
    h0 = jnp.zeros((n_rg, bp, D_RNN), F32)
    c0 = jnp.zeros((n_rg, bp, CONV_W - 1, D_RNN), F32)
    s0 = jnp.zeros((n_gla, bp, GLA_HEADS, HEAD_K, HEAD_V), F32)
    (y_s, h_s, c_s, s_s), bf16 = _trunk(x_sample, state_rglru_h, state_rglru_conv, state_gla, w)
    (y_p, h_p, c_p, s_p), _ = _trunk(x_prompt, h0, c0, s0, w, bf16)
    return (y_p, y_s, h_p, c_p, s_p, h_s, c_s, s_s)
```

```python
import functools

import jax
import jax.numpy as jnp
from jax import lax
from jax.experimental import pallas as pl
from jax.experimental.pallas import tpu as pltpu

F32 = jnp.float32
BF16 = jnp.bfloat16

D_MODEL = 2048
DEPTH = 4
CHUNK = 64
EPS = 1e-6
D_RNN = D_MODEL
RG_BLOCKS = 8
RG_BW = D_RNN // RG_BLOCKS
CONV_W = 4
RG_C = 8.0
GLA_HEADS = 4
HEAD_K = 256
HEAD_V = 512
GATE_RANK = 16
GATE_NORM = 16.0
GLA_DQ = GLA_HEADS * HEAD_K
GLA_DV = GLA_HEADS * HEAD_V
GLA_MAIN = 2 * GLA_DQ + 2 * GLA_DV
D_FF = 5632

LANES = 128
SUBLANES = 8
NORM_ROWS = 16
NORM_UNROLL = 16
FFN_NORM_UNROLL = 8
MIB = 1024 * 1024
LOG2E = 1.4426950408889634
GELU_C1 = 0.7978845608028654
GELU_C2 = 0.044715 * GELU_C1

MATMUL_TM = 1024
MATMUL_TN = 1024
VMEM_WINDOW_MIB = 46
FFN_TM = 1024
FFN_TF = 512
FFN_CAST_TF = 256
RGLRU_TC = 512
GLA_SEQS = 4


def _params(dims, vmem_mib):
    return pltpu.CompilerParams(dimension_semantics=dims, vmem_limit_bytes=vmem_mib * MIB)


def _rms(x, w):
    ms = jnp.mean(x * x, axis=-1, keepdims=True)
    return x * lax.rsqrt(ms + EPS) * w


def _log_sigmoid(x):
    return jnp.minimum(x, 0.0) - jnp.log(1.0 + jnp.exp(-jnp.abs(x)))


def _split3_bf16(x):
    hi = x.astype(BF16)
    r1 = x - hi.astype(F32)
    mid = r1.astype(BF16)
    lo = (r1 - mid.astype(F32)).astype(BF16)
    return hi, mid, lo


def _sqrt_nonneg(y):
    return jnp.exp2((0.5 * LOG2E) * jnp.log(y))


def _gelu_tanh(x):
    inner = x * (GELU_C1 + GELU_C2 * (x * x))
    return (0.5 * x) * (1.0 + jnp.tanh(inner))


def _norm_rows(src_ref, nw_ref, dst_ref, rows, dtype, unroll=NORM_UNROLL):
    nw = nw_ref[...]

    def body(r, carry):
        sl = pl.ds(pl.multiple_of(r * NORM_ROWS, NORM_ROWS), NORM_ROWS)
        dst_ref[sl, :] = _rms(src_ref[sl, :], nw).astype(dtype)
        return carry

    lax.fori_loop(0, rows // NORM_ROWS, body, 0, unroll=unroll)


def _norm_rows_inplace(ref, nw_ref, scale_ref, rows):
    def scales(r, carry):
        sl = pl.ds(pl.multiple_of(r * NORM_ROWS, NORM_ROWS), NORM_ROWS)
        x = ref[sl, :]
        scale_ref[sl, :] = lax.rsqrt(jnp.mean(x * x, axis=-1, keepdims=True) + EPS)
        return carry

    lax.fori_loop(0, rows // NORM_ROWS, scales, 0, unroll=NORM_UNROLL)
    nw = nw_ref[...]

    def apply(r, carry):
        sl = pl.ds(pl.multiple_of(r * NORM_ROWS, NORM_ROWS), NORM_ROWS)
        ref[sl, :] = ref[sl, :] * scale_ref[sl, :] * nw
        return carry

    lax.fori_loop(0, rows // NORM_ROWS, apply, 0, unroll=NORM_UNROLL)


def _norm_matmul_kernel(x_ref, nw_ref, w_ref, w2_ref, o_ref, o2_ref, hn_ref, *, tm):
    @pl.when(pl.program_id(1) == 0)
    def _():
        _norm_rows(x_ref, nw_ref, hn_ref, tm, BF16)
        if w2_ref is not None:
            o2_ref[...] = jnp.dot(hn_ref[...], w2_ref[...], preferred_element_type=F32)

    o_ref[...] = jnp.dot(hn_ref[...], w_ref[...], preferred_element_type=F32)


def _norm_matmul_plain_kernel(x_ref, nw_ref, w_ref, o_ref, hn_ref, *, tm):
    _norm_matmul_kernel(x_ref, nw_ref, w_ref, None, o_ref, None, hn_ref, tm=tm)


def _norm_matmul_cast_kernel(x_ref, nw_ref, w_ref, o_ref, wo_ref, hn_ref, *, tm):
    wo_ref[...] = w_ref[...].astype(BF16)
    _norm_matmul_kernel(x_ref, nw_ref, wo_ref, None, o_ref, None, hn_ref, tm=tm)


def _norm_matmul_cast2_kernel(x_ref, nw_ref, w_ref, w2_ref, o_ref, o2_ref, wo_ref, w2o_ref, hn_ref, *, tm, side):
    wo_ref[...] = w_ref[...].astype(BF16)

    @pl.when(pl.program_id(1) == 0)
    def _():
        lane = lax.broadcasted_iota(jnp.int32, w2_ref.shape, 1)
        w2o_ref[...] = jnp.where(lane < side, w2_ref[...], 0.0).astype(BF16)

    _norm_matmul_kernel(x_ref, nw_ref, wo_ref, w2o_ref, o_ref, o2_ref, hn_ref, tm=tm)


def _norm_matmul(x, nws, layer, n, side=0, w_bf16=None, w_f32=None, j=None):
    m, d = x.shape
    tm = min(m, MATMUL_TM)
    tn = MATMUL_TN
    w_buffers = 2
    if w_bf16 is not None:
        def windows_mib(rows, cols, wbuf):
            return (2 * rows * d * 4 + rows * d * 2 + wbuf * d * cols * 2 + 2 * rows * cols * 4) / MIB
        if m > tm and windows_mib(tm // 2, n, 1) <= VMEM_WINDOW_MIB:
            tm, tn, w_buffers = tm // 2, n, 1
        else:
            tn = max(c for c in range(MATMUL_TN, n + 1, MATMUL_TN // 2)
                     if n % c == 0 and windows_mib(tm, c, 2) <= VMEM_WINDOW_MIB)
    grid = (m // tm, n // tn)
    x_spec = pl.BlockSpec((tm, d), lambda i, c: (i, 0))
    nw_spec = pl.BlockSpec((None, 1, d), lambda i, c: (layer, 0, 0))
    w_spec = pl.BlockSpec((d, tn), lambda i, c: (0, c), pipeline_mode=pl.Buffered(w_buffers))
    w2_spec = pl.BlockSpec((d, LANES), lambda i, c: (0, 0))
    o_spec = pl.BlockSpec((tm, tn), lambda i, c: (i, c))
    o2_spec = pl.BlockSpec((tm, LANES), lambda i, c: (i, 0))
    o_shape = jax.ShapeDtypeStruct((m, n), F32)
    o2_shape = jax.ShapeDtypeStruct((m, LANES), F32)
    scratch = [pltpu.VMEM((tm, d), BF16)]
    cp = _params(("parallel", "arbitrary"), 56)
    if w_bf16 is not None:
        w, w2 = w_bf16
        if not side:
            return pl.pallas_call(
                functools.partial(_norm_matmul_plain_kernel, tm=tm), grid=grid,
                in_specs=[x_spec, nw_spec, w_spec], out_specs=o_spec, out_shape=o_shape,
                scratch_shapes=scratch, compiler_params=cp, name="norm_matmul")(x, nws, w)
        return pl.pallas_call(
            functools.partial(_norm_matmul_kernel, tm=tm), grid=grid,
            in_specs=[x_spec, nw_spec, w_spec, w2_spec], out_specs=[o_spec, o2_spec],
            out_shape=[o_shape, o2_shape], scratch_shapes=scratch, compiler_params=cp,
            name="norm_matmul2")(x, nws, w, w2)
    assert m == tm, "the casting variant rewrites the bf16 weights once per row block"
    wf_spec = pl.BlockSpec((None, d, tn), lambda i, c: (j, 0, c))
    w_shape = jax.ShapeDtypeStruct((d, n), BF16)
    if not side:
        out, w = pl.pallas_call(
            functools.partial(_norm_matmul_cast_kernel, tm=tm), grid=grid,
            in_specs=[x_spec, nw_spec, wf_spec], out_specs=[o_spec, w_spec], out_shape=[o_shape, w_shape],
            scratch_shapes=scratch, compiler_params=cp, name="norm_matmul_cast")(x, nws, w_f32)
        return out, (w, None)
    out, out2, w, w2 = pl.pallas_call(
        functools.partial(_norm_matmul_cast2_kernel, tm=tm, side=side), grid=grid,
        in_specs=[x_spec, nw_spec, wf_spec, pl.BlockSpec((None, d, LANES), lambda i, c: (j, 0, n // LANES))],
        out_specs=[o_spec, o2_spec, w_spec, w2_spec],
        out_shape=[o_shape, o2_shape, w_shape, jax.ShapeDtypeStruct((d, LANES), BF16)],
        scratch_shapes=scratch, compiler_params=cp, name="norm_matmul2_cast")(x, nws, w_f32, w_f32)
    return (out, out2), (w, w2)


def _matmul_res_kernel(a_ref, w_ref, x_ref, o_ref):
    o_ref[...] = x_ref[...] + jnp.dot(a_ref[...], w_ref[...], preferred_element_type=F32)


def _matmul_res_cast_kernel(a_ref, w_ref, x_ref, o_ref, wo_ref):
    wo_ref[...] = w_ref[...].astype(BF16)
    _matmul_res_kernel(a_ref, wo_ref, x_ref, o_ref)


def _matmul_res(a, x, w_bf16=None, w_f32=None, j=None):
    m, k = a.shape
    n = x.shape[1]
    tm = min(m, MATMUL_TM)
    tn = n if w_bf16 is not None else MATMUL_TN
    grid = (m // tm, n // tn)
    a_spec = pl.BlockSpec((tm, k), lambda i, c: (i, 0))
    w_spec = pl.BlockSpec((k, tn), lambda i, c: (0, c))
    xo_spec = pl.BlockSpec((tm, tn), lambda i, c: (i, c))
    o_shape = jax.ShapeDtypeStruct((m, n), F32)
    cp = _params(("parallel", "arbitrary"), 56)
    if w_bf16 is not None:
        w_res_spec = pl.BlockSpec((k, n), lambda i, c: (0, 0), pipeline_mode=pl.Buffered(1))
        return pl.pallas_call(
            _matmul_res_kernel, grid=grid, in_specs=[a_spec, w_res_spec, xo_spec], out_specs=xo_spec,
            out_shape=o_shape, compiler_params=cp, name="matmul_res")(a, w_bf16, x)
    assert m == tm, "the casting variant rewrites the bf16 weight once per row block"
    return pl.pallas_call(
        _matmul_res_cast_kernel, grid=grid,
        in_specs=[a_spec, pl.BlockSpec((None, k, tn), lambda i, c: (j, 0, c)), xo_spec],
        out_specs=[xo_spec, w_spec], out_shape=[o_shape, jax.ShapeDtypeStruct((k, n), BF16)],
        compiler_params=cp, name="matmul_res_cast")(a, w_f32, x)


def _ffn_kernel(x_ref, nw_ref, wg_ref, wu_ref, wd_ref, fw_ref, o_ref, hn_ref, scale_ref, *, tm, nf, final):
    f = pl.program_id(1)

    @pl.when(f == 0)
    def _():
        _norm_rows(x_ref, nw_ref, hn_ref, tm, BF16, unroll=FFN_NORM_UNROLL)
        o_ref[...] = x_ref[...]

    hn = hn_ref[...]
    gt = jnp.dot(hn, wg_ref[...], preferred_element_type=F32)
    up = jnp.dot(hn, wu_ref[...], preferred_element_type=F32)
    act = ((gt * jax.nn.sigmoid(gt)) * up).astype(BF16)
    o_ref[...] += jnp.dot(act, wd_ref[...], preferred_element_type=F32)

    if final:
        @pl.when(f == nf - 1)
        def _():
            _norm_rows_inplace(o_ref, fw_ref, scale_ref, tm)


def _ffn_cast_kernel(x_ref, nw_ref, wg_ref, wu_ref, wd_ref, fw_ref, o_ref, wgo_ref, wuo_ref, wdo_ref, hn_ref, scale_ref,
                     **kw):
    wgo_ref[...] = wg_ref[...].astype(BF16)
    wuo_ref[...] = wu_ref[...].astype(BF16)
    wdo_ref[...] = wd_ref[...].astype(BF16)
    _ffn_kernel(x_ref, nw_ref, wgo_ref, wuo_ref, wdo_ref, fw_ref, o_ref, hn_ref, scale_ref, **kw)


def _ffn(x, nws, layer, fw, final, w_bf16=None, w_f32=None):
    m, d = x.shape
    tm = min(m, FFN_TM)
    tf = FFN_TF if w_bf16 is not None else FFN_CAST_TF
    nf = D_FF // tf
    kw = dict(tm=tm, nf=nf, final=final)
    x_spec = pl.BlockSpec((tm, d), lambda i, f: (i, 0), pipeline_mode=pl.Buffered(1))
    nw_spec = pl.BlockSpec((None, 1, d), lambda i, f: (layer, 0, 0))
    fw_spec = pl.BlockSpec((1, d), lambda i, f: (0, 0))
    o_spec = pl.BlockSpec((tm, d), lambda i, f: (i, 0))
    o_shape = jax.ShapeDtypeStruct((m, d), F32)
    scratch = [pltpu.VMEM((tm, d), BF16), pltpu.VMEM((tm, 1), F32)]
    cp = _params(("parallel", "arbitrary"), 56)
    up_spec = pl.BlockSpec((d, tf), lambda i, f: (0, f))
    down_spec = pl.BlockSpec((tf, d), lambda i, f: (f, 0))
    if w_bf16 is not None:
        return pl.pallas_call(
            functools.partial(_ffn_kernel, **kw), grid=(m // tm, nf),
            in_specs=[x_spec, nw_spec, up_spec, up_spec, down_spec, fw_spec],
            out_specs=o_spec, out_shape=o_shape, scratch_shapes=scratch,
            compiler_params=cp, name="ffn")(x, nws, *w_bf16, fw)
    assert m == tm, "the casting variant rewrites the bf16 weights once per row block"
    w_ups, w_downs = w_f32
    out, wg, wu, wd = pl.pallas_call(
        functools.partial(_ffn_cast_kernel, **kw), grid=(1, nf),
        in_specs=[x_spec, nw_spec,
                  pl.BlockSpec((None, d, tf), lambda i, f: (layer, 0, f)),
                  pl.BlockSpec((None, d, tf), lambda i, f: (layer, 0, f + nf)),
                  pl.BlockSpec((None, tf, d), lambda i, f: (layer, f, 0)),
                  fw_spec],
        out_specs=[o_spec, up_spec, up_spec, down_spec],
        out_shape=[o_shape, jax.ShapeDtypeStruct((d, D_FF), BF16), jax.ShapeDtypeStruct((d, D_FF), BF16),
                   jax.ShapeDtypeStruct((D_FF, d), BF16)],
        scratch_shapes=scratch, compiler_params=cp, name="ffn_cast")(x, nws, w_ups, w_ups, w_downs, fw)
    return out, (wg, wu, wd)


def _rglru_kernel(gate_ref, u_ref, h0_ref, c0_ref, cw_ref, cb_ref, wa_ref, ba_ref, wx_ref, bx_ref, lam_ref,
                  hg_ref, hout_ref, cout_ref, uext_ref, *, tc, nt):
    t = pl.program_id(2)
    pad = SUBLANES

    @pl.when(t == 0)
    def _():
        uext_ref[...] = jnp.zeros((pad, RG_BW), F32)
        uext_ref[pad - (CONV_W - 1):pad, :] = c0_ref[...]
        hout_ref[...] = h0_ref[...]

    u = u_ref[...]
    ext = jnp.concatenate([uext_ref[...], u], axis=0)
    cw = cw_ref[...]
    acc = pltpu.roll(ext, 3, 0)[pad:, :] * cw[0:1, :]
    acc = acc + pltpu.roll(ext, 2, 0)[pad:, :] * cw[1:2, :]
    acc = acc + pltpu.roll(ext, 1, 0)[pad:, :] * cw[2:3, :]
    acc = acc + u * cw[3:4, :]
    uc = cb_ref[...] + acc
    ucb = uc.astype(BF16)
    r = jax.nn.sigmoid(jnp.dot(ucb, wa_ref[...], preferred_element_type=F32) + ba_ref[...])
    i = jax.nn.sigmoid(jnp.dot(ucb, wx_ref[...], preferred_element_type=F32) + bx_ref[...])
    a = jnp.exp2(r * ((RG_C * LOG2E) * _log_sigmoid(lam_ref[...])))
    b = _sqrt_nonneg(1.0 - a * a) * (i * uc)

    groups = tc // SUBLANES
    a3 = a.reshape(groups, SUBLANES, RG_BW)
    b3 = b.reshape(groups, SUBLANES, RG_BW)
    row = lax.broadcasted_iota(jnp.int32, (groups, SUBLANES, RG_BW), 1)
    for s in (1, 2, 4):
        keep = row >= s
        a_prev = pltpu.roll(a3, s, 1)
        b_prev = pltpu.roll(b3, s, 1)
        b3 = jnp.where(keep, b3 + a3 * b_prev, b3)
        a3 = jnp.where(keep, a3 * a_prev, a3)
    h_prev = jnp.broadcast_to(hout_ref[...], (SUBLANES, RG_BW))
    hs = []
    for g in range(groups):
        h_g = a3[g] * h_prev + b3[g]
        hs.append(h_g)
        h_prev = jnp.broadcast_to(h_g[SUBLANES - 1:SUBLANES, :], (SUBLANES, RG_BW))
    h = jnp.concatenate(hs, axis=0)
    hout_ref[...] = hs[-1][SUBLANES - 1:SUBLANES, :]
    hg_ref[...] = (h * _gelu_tanh(gate_ref[...])).astype(BF16)

    uext_ref[...] = u[tc - pad:, :]

    @pl.when(t == nt - 1)
    def _():
        cout_ref[...] = uext_ref[pad - (CONV_W - 1):pad, :]


def _rglru_core(gu, nseq, tlen, j, h_all, c_all, w):
    m = nseq * tlen
    tc = min(tlen, RGLRU_TC)
    nt = tlen // tc
    bw = RG_BW
    row = lambda b, n, t: b * nt + t
    vec = pl.BlockSpec((None, 1, bw), lambda b, n, t: (j, 0, n))
    gate_spec = pl.BlockSpec((tc, bw), lambda b, n, t: (row(b, n, t), n))
    u_spec = pl.BlockSpec((tc, bw), lambda b, n, t: (row(b, n, t), RG_BLOCKS + n))
    hin_spec = pl.BlockSpec((None, None, 1, bw), lambda b, n, t: (j, b, 0, n))
    cin_spec = pl.BlockSpec((None, None, CONV_W - 1, bw), lambda b, n, t: (j, b, 0, n))
    hout_spec = pl.BlockSpec((None, 1, bw), lambda b, n, t: (b, 0, n))
    cout_spec = pl.BlockSpec((None, CONV_W - 1, bw), lambda b, n, t: (b, 0, n))
    w_spec = pl.BlockSpec((None, None, bw, bw), lambda b, n, t: (j, n, 0, 0))
    return pl.pallas_call(
        functools.partial(_rglru_kernel, tc=tc, nt=nt),
        grid=(nseq, RG_BLOCKS, nt),
        in_specs=[gate_spec, u_spec, hin_spec, cin_spec,
                  pl.BlockSpec((None, CONV_W, bw), lambda b, n, t: (j, 0, n)), vec,
                  w_spec, vec, w_spec, vec, vec],
        out_specs=[pl.BlockSpec((tc, bw), lambda b, n, t: (row(b, n, t), n)), hout_spec, cout_spec],
        out_shape=[jax.ShapeDtypeStruct((m, D_RNN), BF16),
                   jax.ShapeDtypeStruct((nseq, 1, D_RNN), F32),
                   jax.ShapeDtypeStruct((nseq, CONV_W - 1, D_RNN), F32)],
        scratch_shapes=[pltpu.VMEM((SUBLANES, bw), F32)],
        compiler_params=_params(("parallel", "parallel", "arbitrary"), 32),
        name="rglru_core")(gu, gu, h_all, c_all, w["rg_conv_w"], w["rg_conv_b"], w["rg_w_a"], w["rg_b_a"],
                           w["rg_w_x"], w["rg_b_x"], w["rg_lambda"])


def _gla_kernel(q_ref, k_ref, v_ref, g_ref, glr_ref, s0_ref, w2_ref, bgk_ref, gnw_ref,
                o_ref, sout_ref, st_ref, *, nt, nb):
    t = pl.program_id(1)

    @pl.when(t == 0)
    def _():
        for s in range(nb):
            for h in range(GLA_HEADS):
                st_ref[s, h] = s0_ref[s, h].T

    c = CHUNK
    glr = glr_ref[...].reshape(nb * c, LANES).astype(BF16)
    gk_all = jnp.dot(glr, w2_ref[...], preferred_element_type=F32) + bgk_ref[...]
    gk_all = _log_sigmoid(gk_all) * (1.0 / GATE_NORM)
    rows = lax.broadcasted_iota(jnp.int32, (c, c), 0)
    cols = lax.broadcasted_iota(jnp.int32, (c, c), 1)
    tri = rows >= cols
    tri_b = tri.astype(BF16)
    gnw = gnw_ref[...]
    nt_dims = (((1,), (1,)), ((), ()))
    tn_dims = (((0,), (0,)), ((), ()))
    prep = []
    for s in range(nb):
        gk = gk_all[s * c:(s + 1) * c, :]
        bcum = sum(jnp.dot(tri_b, piece, preferred_element_type=F32) for piece in _split3_bf16(gk))
        blast = bcum[c - 1:c, :]
        k = k_ref[s]
        qe = ((q_ref[s] * (HEAD_K ** -0.5)) * jnp.exp(bcum)).astype(BF16)
        ke = (k * jnp.exp(-bcum)).astype(BF16)
        kd = (k * jnp.exp(blast - bcum)).astype(BF16)
        prep.append((qe, ke, kd, jnp.exp(blast)))
    for h in range(GLA_HEADS):
        ks = slice(h * HEAD_K, (h + 1) * HEAD_K)
        vs = slice(h * HEAD_V, (h + 1) * HEAD_V)
        for s in range(nb):
            qe, ke, kd, decay = prep[s]
            v_h = v_ref[s, :, vs].astype(BF16)
            att = lax.dot_general(qe[:, ks], ke[:, ks], nt_dims, preferred_element_type=F32)
            att = jnp.where(tri, att, 0.0).astype(BF16)
            st = st_ref[s, h]
            o = jnp.dot(att, v_h, preferred_element_type=F32)
            o = o + lax.dot_general(qe[:, ks], st.astype(BF16), nt_dims, preferred_element_type=F32)
            st_ref[s, h] = st * decay[:, ks] + lax.dot_general(v_h, kd[:, ks], tn_dims,
                                                               preferred_element_type=F32)
            on = o * lax.rsqrt(jnp.mean(o * o, axis=-1, keepdims=True) + EPS) * gnw
            g_h = g_ref[s, :, vs]
            o_ref[s, :, vs] = (on * (g_h * jax.nn.sigmoid(g_h))).astype(BF16)

    @pl.when(t == nt - 1)
    def _():
        for s in range(nb):
            for h in range(GLA_HEADS):
                sout_ref[s, h] = st_ref[s, h].T


def _gla_aliased_kernel(*refs, **kw):
    _gla_kernel(*refs[:9], *refs[10:], **kw)


def _gla_core(qkvg, glr, nseq, tlen, j, s_all, w, s_stack=None):
    c = CHUNK
    nt = tlen // c
    nb = GLA_SEQS
    n_gla = s_all.shape[0]
    qkvg = qkvg.reshape(nseq, tlen, GLA_MAIN)
    glr = glr.reshape(nseq, tlen, LANES)
    in_specs = [pl.BlockSpec((nb, c, GLA_DQ), lambda b, t: (b, t, 0)),
                pl.BlockSpec((nb, c, GLA_DQ), lambda b, t: (b, t, 1)),
                pl.BlockSpec((nb, c, GLA_DV), lambda b, t: (b, t, 1)),
                pl.BlockSpec((nb, c, GLA_DV), lambda b, t: (b, t, 2)),
                pl.BlockSpec((nb, c, LANES), lambda b, t: (b, t, 0)),
                pl.BlockSpec((None, nb, GLA_HEADS, HEAD_K, HEAD_V), lambda b, t: (j, b, 0, 0, 0),
                             pipeline_mode=pl.Buffered(1)),
                pl.BlockSpec((None, LANES, GLA_DQ), lambda b, t: (j, 0, 0)),
                pl.BlockSpec((None, 1, GLA_DQ), lambda b, t: (j, 0, 0)),
                pl.BlockSpec((None, 1, HEAD_V), lambda b, t: (j, 0, 0))]
    args = [qkvg, qkvg, qkvg, qkvg, glr, s_all, w["gla_w_gk2"], w["gla_b_gk"], w["gla_norm_w"]]
    body, aliases = _gla_kernel, {}
    if s_stack is not None:
        in_specs.append(pl.BlockSpec(memory_space=pl.ANY))
        args.append(s_stack)
        body, aliases = _gla_aliased_kernel, {len(args) - 1: 1}
    og, s_stack = pl.pallas_call(
        functools.partial(body, nt=nt, nb=nb),
        grid=(nseq // nb, nt),
        in_specs=in_specs,
        out_specs=[pl.BlockSpec((nb, c, GLA_DV), lambda b, t: (b, t, 0)),
                   pl.BlockSpec((None, nb, GLA_HEADS, HEAD_K, HEAD_V), lambda b, t: (j, b, 0, 0, 0))],
        out_shape=[jax.ShapeDtypeStruct((nseq, tlen, GLA_DV), BF16),
                   jax.ShapeDtypeStruct((n_gla, nseq, GLA_HEADS, HEAD_K, HEAD_V), F32)],
        scratch_shapes=[pltpu.VMEM((nb, GLA_HEADS, HEAD_V, HEAD_K), F32)],
        input_output_aliases=aliases,
        compiler_params=_params(("parallel", "arbitrary"), 56),
        name="gla_core")(*args)
    return og.reshape(nseq * tlen, GLA_DV), s_stack


def _trunk(x, h_all, conv_all, s_all, w, bf16=None):
    nseq, tlen, d = x.shape
    x = x.reshape(nseq * tlen, d)
    h_all = h_all[:, :, None, :]
    cast = bf16 is None
    made = []
    hs, cs, s_stack = [], [], None
    for layer in range(DEPTH):
        j = layer // 2
        wl = {} if cast else bf16[layer]
        if layer % 2 == 0:
            if cast:
                gu, wl["in"] = _norm_matmul(x, w["norm_mix"], layer, 2 * D_RNN, w_f32=w["rg_w_in"], j=j)
            else:
                gu = _norm_matmul(x, w["norm_mix"], layer, 2 * D_RNN, w_bf16=wl["in"])
            mixed, h_new, c_new = _rglru_core(gu, nseq, tlen, j, h_all, conv_all, w)
            hs.append(h_new[:, 0, :])
            cs.append(c_new)
            w_out = w["rg_w_out"]
        else:
            if cast:
                (qkvg, glr), wl["in"] = _norm_matmul(x, w["norm_mix"], layer, GLA_MAIN, side=GATE_RANK,
                                                     w_f32=w["gla_w_in"], j=j)
            else:
                qkvg, glr = _norm_matmul(x, w["norm_mix"], layer, GLA_MAIN, side=GATE_RANK, w_bf16=wl["in"])
            mixed, s_stack = _gla_core(qkvg, glr, nseq, tlen, j, s_all, w, s_stack)
            w_out = w["gla_w_out"]
        final = layer == DEPTH - 1
        if cast:
            x, wl["out"] = _matmul_res(mixed, x, w_f32=w_out, j=j)
            x, wl["ffn"] = _ffn(x, w["norm_ffn"], layer, w["norm_final"], final,
                                w_f32=(w["ffn_w_up"], w["ffn_w_down"]))
            made.append(wl)
        else:
            x = _matmul_res(mixed, x, w_bf16=wl["out"])
            x = _ffn(x, w["norm_ffn"], layer, w["norm_final"], final, w_bf16=wl["ffn"])
    return (x.reshape(nseq, tlen, d), jnp.stack(hs), jnp.stack(cs), s_stack), made


def _prepare_weights(norm_mix, norm_ffn, norm_final, rg_w_in, rg_conv_w, rg_conv_b, rg_w_a, rg_b_a, rg_w_x, rg_b_x,
                     rg_lambda, rg_w_out, gla_w_in, gla_w_gk2, gla_b_gk, gla_norm_w, gla_w_out, ffn_w_up, ffn_w_down):
    glr_pad = LANES - GATE_RANK
    return {
        "norm_mix": norm_mix[:, None, :],
        "norm_ffn": norm_ffn[:, None, :],
        "norm_final": norm_final[None, :],
        "rg_w_in": rg_w_in,
        "rg_conv_w": rg_conv_w,
        "rg_conv_b": rg_conv_b[:, None, :],
        "rg_w_a": rg_w_a.astype(BF16),
        "rg_b_a": rg_b_a[:, None, :],
        "rg_w_x": rg_w_x.astype(BF16),
        "rg_b_x": rg_b_x[:, None, :],
        "rg_lambda": rg_lambda[:, None, :],
        "rg_w_out": rg_w_out,
        "gla_w_in": gla_w_in,
        "gla_w_gk2": jnp.pad(gla_w_gk2, ((0, 0), (0, glr_pad), (0, 0))).astype(BF16),
        "gla_b_gk": gla_b_gk[:, None, :],
        "gla_norm_w": gla_norm_w[:, None, :],
        "gla_w_out": gla_w_out,
        "ffn_w_up": ffn_w_up,
        "ffn_w_down": ffn_w_down,
    }


def kernel(x_prompt, x_sample, state_rglru_h, state_rglru_conv, state_gla, norm_mix, norm_ffn, norm_final, rg_w_in, rg_conv_w, rg_conv_b, rg_w_a, rg_b_a, rg_w_x, rg_b_x, rg_lambda, rg_w_out, gla_w_in, gla_w_gk2, gla_b_gk, gla_norm_w, gla_w_out, ffn_w_up, ffn_w_down):
    w = _prepare_weights(norm_mix, norm_ffn, norm_final, rg_w_in, rg_conv_w, rg_conv_b, rg_w_a, rg_b_a, rg_w_x,
                         rg_b_x, rg_lambda, rg_w_out, gla_w_in, gla_w_gk2, gla_b_gk, gla_norm_w, gla_w_out,
                         ffn_w_up, ffn_w_down)
    bp = x_prompt.shape[0]
    n_rg = state_rglru_h.shape[0]
    n_gla = state_gla.shape[0]
    h0 = jnp.zeros((n_rg, bp, D_RNN), F32)
    c0 = jnp.zeros((n_rg, bp, CONV_W - 1, D_RNN), F32)
    s0 = jnp.zeros((n_gla, bp, GLA_HEADS, HEAD_K, HEAD_V), F32)
    (y_s, h_s, c_s, s_s), bf16 = _trunk(x_sample, state_rglru_h, state_rglru_conv, state_gla, w)
    (y_p, h_p, c_p, s_p), _ = _trunk(x_prompt, h0, c0, s0, w, bf16)
    return (y_p, y_s, h_p, c_p, s_p, h_s, c_s, s_s)
```

```python
import functools

import jax
import jax.numpy as jnp
from jax import lax
from jax.experimental import pallas as pl
from jax.experimental.pallas import tpu as pltpu

F32 = jnp.float32
BF16 = jnp.bfloat16

D_MODEL = 2048
DEPTH = 4
CHUNK = 64
EPS = 1e-6
D_RNN = D_MODEL
RG_BLOCKS = 8
RG_BW = D_RNN // RG_BLOCKS
CONV_W = 4
RG_C = 8.0
GLA_HEADS = 4
HEAD_K = 256
HEAD_V = 512
GATE_RANK = 16
GATE_NORM = 16.0
GLA_DQ = GLA_HEADS * HEAD_K
GLA_DV = GLA_HEADS * HEAD_V
GLA_MAIN = 2 * GLA_DQ + 2 * GLA_DV
D_FF = 5632

LANES = 128
SUBLANES = 8
NORM_ROWS = 16
NORM_UNROLL = 16
FFN_NORM_UNROLL = 8
MIB = 1024 * 1024
LOG2E = 1.4426950408889634
GELU_C1 = 0.7978845608028654
GELU_C2 = 0.044715 * GELU_C1

MATMUL_TM = 1024
MATMUL_TN = 1024
VMEM_WINDOW_MIB = 46
VMEM_LIMIT_MIB = 56
VMEM_LIMIT_SMALL_MIB = 32
FFN_TM = 1024
FFN_TF = 512
FFN_CAST_TF = 256
RGLRU_TC = 512
GLA_SEQS = 4


def _params(dims, vmem_mib):
    return pltpu.CompilerParams(dimension_semantics=dims, vmem_limit_bytes=vmem_mib * MIB)


def _rms(x, w):
    ms = jnp.mean(x * x, axis=-1, keepdims=True)
    return x * lax.rsqrt(ms + EPS) * w


def _log_sigmoid(x):
    return jnp.minimum(x, 0.0) - jnp.log(1.0 + jnp.exp(-jnp.abs(x)))


def _split3_bf16(x):
    hi = x.astype(BF16)
    r1 = x - hi.astype(F32)
    mid = r1.astype(BF16)
    lo = (r1 - mid.astype(F32)).astype(BF16)
    return hi, mid, lo


def _sqrt_nonneg(y):
    return jnp.exp2((0.5 * LOG2E) * jnp.log(y))


def _gelu_tanh(x):
    inner = x * (GELU_C1 + GELU_C2 * (x * x))
    return (0.5 * x) * (1.0 + jnp.tanh(inner))


def _norm_rows(src_ref, nw_ref, dst_ref, rows, dtype, unroll=NORM_UNROLL):
    nw = nw_ref[...]

    def body(r, carry):
        sl = pl.ds(pl.multiple_of(r * NORM_ROWS, NORM_ROWS), NORM_ROWS)
        dst_ref[sl, :] = _rms(src_ref[sl, :], nw).astype(dtype)
        return carry

    lax.fori_loop(0, rows // NORM_ROWS, body, 0, unroll=unroll)


def _norm_rows_inplace(ref, nw_ref, scale_ref, rows):
    def scales(r, carry):
        sl = pl.ds(pl.multiple_of(r * NORM_ROWS, NORM_ROWS), NORM_ROWS)
        x = ref[sl, :]
        scale_ref[sl, :] = lax.rsqrt(jnp.mean(x * x, axis=-1, keepdims=True) + EPS)
        return carry

    lax.fori_loop(0, rows // NORM_ROWS, scales, 0, unroll=NORM_UNROLL)
    nw = nw_ref[...]

    def apply(r, carry):
        sl = pl.ds(pl.multiple_of(r * NORM_ROWS, NORM_ROWS), NORM_ROWS)
        ref[sl, :] = ref[sl, :] * scale_ref[sl, :] * nw
        return carry

    lax.fori_loop(0, rows // NORM_ROWS, apply, 0, unroll=NORM_UNROLL)


def _norm_matmul_kernel(x_ref, nw_ref, w_ref, w2_ref, o_ref, o2_ref, hn_ref, *, tm):
    @pl.when(pl.program_id(1) == 0)
    def _():
        _norm_rows(x_ref, nw_ref, hn_ref, tm, BF16)
        if w2_ref is not None:
            o2_ref[...] = jnp.dot(hn_ref[...], w2_ref[...], preferred_element_type=F32)

    o_ref[...] = jnp.dot(hn_ref[...], w_ref[...], preferred_element_type=F32)


def _norm_matmul_plain_kernel(x_ref, nw_ref, w_ref, o_ref, hn_ref, *, tm):
    _norm_matmul_kernel(x_ref, nw_ref, w_ref, None, o_ref, None, hn_ref, tm=tm)


def _norm_matmul_cast_kernel(x_ref, nw_ref, w_ref, o_ref, wo_ref, hn_ref, *, tm):
    wo_ref[...] = w_ref[...].astype(BF16)
    _norm_matmul_kernel(x_ref, nw_ref, wo_ref, None, o_ref, None, hn_ref, tm=tm)


def _norm_matmul_cast2_kernel(x_ref, nw_ref, w_ref, w2_ref, o_ref, o2_ref, wo_ref, w2o_ref, hn_ref, *, tm, side):
    wo_ref[...] = w_ref[...].astype(BF16)

    @pl.when(pl.program_id(1) == 0)
    def _():
        lane = lax.broadcasted_iota(jnp.int32, w2_ref.shape, 1)
        w2o_ref[...] = jnp.where(lane < side, w2_ref[...], 0.0).astype(BF16)

    _norm_matmul_kernel(x_ref, nw_ref, wo_ref, w2o_ref, o_ref, o2_ref, hn_ref, tm=tm)


def _norm_matmul(x, nws, layer, n, side=0, w_bf16=None, w_f32=None, j=None):
    m, d = x.shape
    tm = min(m, MATMUL_TM)
    tn = MATMUL_TN
    w_buffers = 2
    if w_bf16 is not None:
        def windows_mib(rows, cols, wbuf):
            return (2 * rows * d * 4 + rows * d * 2 + wbuf * d * cols * 2 + 2 * rows * cols * 4) / MIB
        if m > tm and windows_mib(tm // 2, n, 1) <= VMEM_WINDOW_MIB:
            tm, tn, w_buffers = tm // 2, n, 1
        else:
            tn = max(c for c in range(MATMUL_TN, n + 1, MATMUL_TN // 2)
                     if n % c == 0 and windows_mib(tm, c, 2) <= VMEM_WINDOW_MIB)
    grid = (m // tm, n // tn)
    x_spec = pl.BlockSpec((tm, d), lambda i, c: (i, 0))
    nw_spec = pl.BlockSpec((None, 1, d), lambda i, c: (layer, 0, 0))
    w_spec = pl.BlockSpec((d, tn), lambda i, c: (0, c), pipeline_mode=pl.Buffered(w_buffers))
    w2_spec = pl.BlockSpec((d, LANES), lambda i, c: (0, 0))
    o_spec = pl.BlockSpec((tm, tn), lambda i, c: (i, c))
    o2_spec = pl.BlockSpec((tm, LANES), lambda i, c: (i, 0))
    o_shape = jax.ShapeDtypeStruct((m, n), F32)
    o2_shape = jax.ShapeDtypeStruct((m, LANES), F32)
    scratch = [pltpu.VMEM((tm, d), BF16)]
    cp = _params(("parallel", "arbitrary"), VMEM_LIMIT_MIB)
    if w_bf16 is not None:
        w, w2 = w_bf16
        if not side:
            return pl.pallas_call(
                functools.partial(_norm_matmul_plain_kernel, tm=tm), grid=grid,
                in_specs=[x_spec, nw_spec, w_spec], out_specs=o_spec, out_shape=o_shape,
                scratch_shapes=scratch, compiler_params=cp, name="norm_matmul")(x, nws, w)
        return pl.pallas_call(
            functools.partial(_norm_matmul_kernel, tm=tm), grid=grid,
            in_specs=[x_spec, nw_spec, w_spec, w2_spec], out_specs=[o_spec, o2_spec],
            out_shape=[o_shape, o2_shape], scratch_shapes=scratch, compiler_params=cp,
            name="norm_matmul2")(x, nws, w, w2)
    assert m == tm, "the casting variant rewrites the bf16 weights once per row block"
    wf_spec = pl.BlockSpec((None, d, tn), lambda i, c: (j, 0, c))
    w_shape = jax.ShapeDtypeStruct((d, n), BF16)
    if not side:
        out, w = pl.pallas_call(
            functools.partial(_norm_matmul_cast_kernel, tm=tm), grid=grid,
            in_specs=[x_spec, nw_spec, wf_spec], out_specs=[o_spec, w_spec], out_shape=[o_shape, w_shape],
            scratch_shapes=scratch, compiler_params=cp, name="norm_matmul_cast")(x, nws, w_f32)
        return out, (w, None)
    out, out2, w, w2 = pl.pallas_call(
        functools.partial(_norm_matmul_cast2_kernel, tm=tm, side=side), grid=grid,
        in_specs=[x_spec, nw_spec, wf_spec, pl.BlockSpec((None, d, LANES), lambda i, c: (j, 0, n // LANES))],
        out_specs=[o_spec, o2_spec, w_spec, w2_spec],
        out_shape=[o_shape, o2_shape, w_shape, jax.ShapeDtypeStruct((d, LANES), BF16)],
        scratch_shapes=scratch, compiler_params=cp, name="norm_matmul2_cast")(x, nws, w_f32, w_f32)
    return (out, out2), (w, w2)


def _matmul_res_kernel(a_ref, w_ref, x_ref, o_ref):
    o_ref[...] = x_ref[...] + jnp.dot(a_ref[...], w_ref[...], preferred_element_type=F32)


def _matmul_res_cast_kernel(a_ref, w_ref, x_ref, o_ref, wo_ref):
    wo_ref[...] = w_ref[...].astype(BF16)
    _matmul_res_kernel(a_ref, wo_ref, x_ref, o_ref)


def _matmul_res(a, x, w_bf16=None, w_f32=None, j=None):
    m, k = a.shape
    n = x.shape[1]
    tm = min(m, MATMUL_TM)
    tn = n if w_bf16 is not None else MATMUL_TN
    grid = (m // tm, n // tn)
    a_spec = pl.BlockSpec((tm, k), lambda i, c: (i, 0))
    w_spec = pl.BlockSpec((k, tn), lambda i, c: (0, c))
    xo_spec = pl.BlockSpec((tm, tn), lambda i, c: (i, c))
    o_shape = jax.ShapeDtypeStruct((m, n), F32)
    cp = _params(("parallel", "arbitrary"), VMEM_LIMIT_MIB)
    if w_bf16 is not None:
        w_res_spec = pl.BlockSpec((k, n), lambda i, c: (0, 0), pipeline_mode=pl.Buffered(1))
        return pl.pallas_call(
            _matmul_res_kernel, grid=grid, in_specs=[a_spec, w_res_spec, xo_spec], out_specs=xo_spec,
            out_shape=o_shape, compiler_params=cp, name="matmul_res")(a, w_bf16, x)
    assert m == tm, "the casting variant rewrites the bf16 weight once per row block"
    return pl.pallas_call(
        _matmul_res_cast_kernel, grid=grid,
        in_specs=[a_spec, pl.BlockSpec((None, k, tn), lambda i, c: (j, 0, c)), xo_spec],
        out_specs=[xo_spec, w_spec], out_shape=[o_shape, jax.ShapeDtypeStruct((k, n), BF16)],
        compiler_params=cp, name="matmul_res_cast")(a, w_f32, x)


def _ffn_kernel(x_ref, nw_ref, wg_ref, wu_ref, wd_ref, fw_ref, o_ref, hn_ref, scale_ref, *, tm, nf, final):
    f = pl.program_id(1)

    @pl.when(f == 0)
    def _():
        _norm_rows(x_ref, nw_ref, hn_ref, tm, BF16, unroll=FFN_NORM_UNROLL)
        o_ref[...] = x_ref[...]

    hn = hn_ref[...]
    gt = jnp.dot(hn, wg_ref[...], preferred_element_type=F32)
    up = jnp.dot(hn, wu_ref[...], preferred_element_type=F32)
    act = ((gt * jax.nn.sigmoid(gt)) * up).astype(BF16)
    o_ref[...] += jnp.dot(act, wd_ref[...], preferred_element_type=F32)

    if final:
        @pl.when(f == nf - 1)
        def _():
            _norm_rows_inplace(o_ref, fw_ref, scale_ref, tm)


def _ffn_cast_kernel(x_ref, nw_ref, wg_ref, wu_ref, wd_ref, fw_ref, o_ref, wgo_ref, wuo_ref, wdo_ref, hn_ref, scale_ref,
                     **kw):
    wgo_ref[...] = wg_ref[...].astype(BF16)
    wuo_ref[...] = wu_ref[...].astype(BF16)
    wdo_ref[...] = wd_ref[...].astype(BF16)
    _ffn_kernel(x_ref, nw_ref, wgo_ref, wuo_ref, wdo_ref, fw_ref, o_ref, hn_ref, scale_ref, **kw)


def _ffn(x, nws, layer, fw, final, w_bf16=None, w_f32=None):
    m, d = x.shape
    tm = min(m, FFN_TM)
    tf = FFN_TF if w_bf16 is not None else FFN_CAST_TF
    nf = D_FF // tf
    kw = dict(tm=tm, nf=nf, final=final)
    x_spec = pl.BlockSpec((tm, d), lambda i, f: (i, 0), pipeline_mode=pl.Buffered(1))
    nw_spec = pl.BlockSpec((None, 1, d), lambda i, f: (layer, 0, 0))
    fw_spec = pl.BlockSpec((1, d), lambda i, f: (0, 0))
    o_spec = pl.BlockSpec((tm, d), lambda i, f: (i, 0))
    o_shape = jax.ShapeDtypeStruct((m, d), F32)
    scratch = [pltpu.VMEM((tm, d), BF16), pltpu.VMEM((tm, 1), F32)]
    cp = _params(("parallel", "arbitrary"), VMEM_LIMIT_MIB)
    up_spec = pl.BlockSpec((d, tf), lambda i, f: (0, f))
    down_spec = pl.BlockSpec((tf, d), lambda i, f: (f, 0))
    if w_bf16 is not None:
        return pl.pallas_call(
            functools.partial(_ffn_kernel, **kw), grid=(m // tm, nf),
            in_specs=[x_spec, nw_spec, up_spec, up_spec, down_spec, fw_spec],
            out_specs=o_spec, out_shape=o_shape, scratch_shapes=scratch,
            compiler_params=cp, name="ffn")(x, nws, *w_bf16, fw)
    assert m == tm, "the casting variant rewrites the bf16 weights once per row block"
    w_ups, w_downs = w_f32
    out, wg, wu, wd = pl.pallas_call(
        functools.partial(_ffn_cast_kernel, **kw), grid=(1, nf),
        in_specs=[x_spec, nw_spec,
                  pl.BlockSpec((None, d, tf), lambda i, f: (layer, 0, f)),
                  pl.BlockSpec((None, d, tf), lambda i, f: (layer, 0, f + nf)),
                  pl.BlockSpec((None, tf, d), lambda i, f: (layer, f, 0)),
                  fw_spec],
        out_specs=[o_spec, up_spec, up_spec, down_spec],
        out_shape=[o_shape, jax.ShapeDtypeStruct((d, D_FF), BF16), jax.ShapeDtypeStruct((d, D_FF), BF16),
                   jax.ShapeDtypeStruct((D_FF, d), BF16)],
        scratch_shapes=scratch, compiler_params=cp, name="ffn_cast")(x, nws, w_ups, w_ups, w_downs, fw)
    return out, (wg, wu, wd)


def _rglru_kernel(gate_ref, u_ref, h0_ref, c0_ref, cw_ref, cb_ref, wa_ref, ba_ref, wx_ref, bx_ref, lam_ref,
                  hg_ref, hout_ref, cout_ref, uext_ref, *, tc, nt):
    t = pl.program_id(2)
    pad = SUBLANES

    @pl.when(t == 0)
    def _():
        uext_ref[...] = jnp.zeros((pad, RG_BW), F32)
        uext_ref[pad - (CONV_W - 1):pad, :] = c0_ref[...]
        hout_ref[...] = h0_ref[...]

    u = u_ref[...]
    ext = jnp.concatenate([uext_ref[...], u], axis=0)
    cw = cw_ref[...]
    acc = pltpu.roll(ext, 3, 0)[pad:, :] * cw[0:1, :]
    acc = acc + pltpu.roll(ext, 2, 0)[pad:, :] * cw[1:2, :]
    acc = acc + pltpu.roll(ext, 1, 0)[pad:, :] * cw[2:3, :]
    acc = acc + u * cw[3:4, :]
    uc = cb_ref[...] + acc
    ucb = uc.astype(BF16)
    r = jax.nn.sigmoid(jnp.dot(ucb, wa_ref[...], preferred_element_type=F32) + ba_ref[...])
    i = jax.nn.sigmoid(jnp.dot(ucb, wx_ref[...], preferred_element_type=F32) + bx_ref[...])
    a = jnp.exp2(r * ((RG_C * LOG2E) * _log_sigmoid(lam_ref[...])))
    b = _sqrt_nonneg(1.0 - a * a) * (i * uc)

    groups = tc // SUBLANES
    a3 = a.reshape(groups, SUBLANES, RG_BW)
    b3 = b.reshape(groups, SUBLANES, RG_BW)
    row = lax.broadcasted_iota(jnp.int32, (groups, SUBLANES, RG_BW), 1)
    for s in (1, 2, 4):
        keep = row >= s
        a_prev = pltpu.roll(a3, s, 1)
        b_prev = pltpu.roll(b3, s, 1)
        b3 = jnp.where(keep, b3 + a3 * b_prev, b3)
        a3 = jnp.where(keep, a3 * a_prev, a3)
    h_prev = jnp.broadcast_to(hout_ref[...], (SUBLANES, RG_BW))
    hs = []
    for g in range(groups):
        h_g = a3[g] * h_prev + b3[g]
        hs.append(h_g)
        h_prev = jnp.broadcast_to(h_g[SUBLANES - 1:SUBLANES, :], (SUBLANES, RG_BW))
    h = jnp.concatenate(hs, axis=0)
    hout_ref[...] = hs[-1][SUBLANES - 1:SUBLANES, :]
    hg_ref[...] = (h * _gelu_tanh(gate_ref[...])).astype(BF16)

    uext_ref[...] = u[tc - pad:, :]

    @pl.when(t == nt - 1)
    def _():
        cout_ref[...] = uext_ref[pad - (CONV_W - 1):pad, :]


def _rglru_core(gu, nseq, tlen, j, h_all, c_all, w):
    m = nseq * tlen
    tc = min(tlen, RGLRU_TC)
    nt = tlen // tc
    bw = RG_BW
    row = lambda b, n, t: b * nt + t
    vec = pl.BlockSpec((None, 1, bw), lambda b, n, t: (j, 0, n))
    gate_spec = pl.BlockSpec((tc, bw), lambda b, n, t: (row(b, n, t), n))
    u_spec = pl.BlockSpec((tc, bw), lambda b, n, t: (row(b, n, t), RG_BLOCKS + n))
    hin_spec = pl.BlockSpec((None, None, 1, bw), lambda b, n, t: (j, b, 0, n))
    cin_spec = pl.BlockSpec((None, None, CONV_W - 1, bw), lambda b, n, t: (j, b, 0, n))
    hout_spec = pl.BlockSpec((None, 1, bw), lambda b, n, t: (b, 0, n))
    cout_spec = pl.BlockSpec((None, CONV_W - 1, bw), lambda b, n, t: (b, 0, n))
    w_spec = pl.BlockSpec((None, None, bw, bw), lambda b, n, t: (j, n, 0, 0))
    return pl.pallas_call(
        functools.partial(_rglru_kernel, tc=tc, nt=nt),
        grid=(nseq, RG_BLOCKS, nt),
        in_specs=[gate_spec, u_spec, hin_spec, cin_spec,
                  pl.BlockSpec((None, CONV_W, bw), lambda b, n, t: (j, 0, n)), vec,
                  w_spec, vec, w_spec, vec, vec],
        out_specs=[pl.BlockSpec((tc, bw), lambda b, n, t: (row(b, n, t), n)), hout_spec, cout_spec],
        out_shape=[jax.ShapeDtypeStruct((m, D_RNN), BF16),
                   jax.ShapeDtypeStruct((nseq, 1, D_RNN), F32),
                   jax.ShapeDtypeStruct((nseq, CONV_W - 1, D_RNN), F32)],
        scratch_shapes=[pltpu.VMEM((SUBLANES, bw), F32)],
        compiler_params=_params(("parallel", "parallel", "arbitrary"), VMEM_LIMIT_SMALL_MIB),
        name="rglru_core")(gu, gu, h_all, c_all, w["rg_conv_w"], w["rg_conv_b"], w["rg_w_a"], w["rg_b_a"],
                           w["rg_w_x"], w["rg_b_x"], w["rg_lambda"])


def _gla_kernel(q_ref, k_ref, v_ref, g_ref, glr_ref, s0_ref, w2_ref, bgk_ref, gnw_ref,
                o_ref, sout_ref, st_ref, *, nt, nb):
    t = pl.program_id(1)

    @pl.when(t == 0)
    def _():
        for s in range(nb):
            for h in range(GLA_HEADS):
                st_ref[s, h] = s0_ref[s, h].T

    c = CHUNK
    glr = glr_ref[...].reshape(nb * c, LANES).astype(BF16)
    gk_all = jnp.dot(glr, w2_ref[...], preferred_element_type=F32) + bgk_ref[...]
    gk_all = _log_sigmoid(gk_all) * (1.0 / GATE_NORM)
    rows = lax.broadcasted_iota(jnp.int32, (c, c), 0)
    cols = lax.broadcasted_iota(jnp.int32, (c, c), 1)
    tri = rows >= cols
    tri_b = tri.astype(BF16)
    gnw = gnw_ref[...]
    nt_dims = (((1,), (1,)), ((), ()))
    tn_dims = (((0,), (0,)), ((), ()))
    prep = []
    for s in range(nb):
        gk = gk_all[s * c:(s + 1) * c, :]
        bcum = sum(jnp.dot(tri_b, piece, preferred_element_type=F32) for piece in _split3_bf16(gk))
        blast = bcum[c - 1:c, :]
        k = k_ref[s]
        qe = ((q_ref[s] * (HEAD_K ** -0.5)) * jnp.exp(bcum)).astype(BF16)
        ke = (k * jnp.exp(-bcum)).astype(BF16)
        kd = (k * jnp.exp(blast - bcum)).astype(BF16)
        prep.append((qe, ke, kd, jnp.exp(blast)))
    for h in range(GLA_HEADS):
        ks = slice(h * HEAD_K, (h + 1) * HEAD_K)
        vs = slice(h * HEAD_V, (h + 1) * HEAD_V)
        for s in range(nb):
            qe, ke, kd, decay = prep[s]
            v_h = v_ref[s, :, vs].astype(BF16)
            att = lax.dot_general(qe[:, ks], ke[:, ks], nt_dims, preferred_element_type=F32)
            att = jnp.where(tri, att, 0.0).astype(BF16)
            st = st_ref[s, h]
            o = jnp.dot(att, v_h, preferred_element_type=F32)
            o = o + lax.dot_general(qe[:, ks], st.astype(BF16), nt_dims, preferred_element_type=F32)
            st_ref[s, h] = st * decay[:, ks] + lax.dot_general(v_h, kd[:, ks], tn_dims,
                                                               preferred_element_type=F32)
            on = o * lax.rsqrt(jnp.mean(o * o, axis=-1, keepdims=True) + EPS) * gnw
            g_h = g_ref[s, :, vs]
            o_ref[s, :, vs] = (on * (g_h * jax.nn.sigmoid(g_h))).astype(BF16)

    @pl.when(t == nt - 1)
    def _():
        for s in range(nb):
            for h in range(GLA_HEADS):
                sout_ref[s, h] = st_ref[s, h].T


def _gla_aliased_kernel(*refs, **kw):
    _gla_kernel(*refs[:9], *refs[10:], **kw)


def _gla_core(qkvg, glr, nseq, tlen, j, s_all, w, s_stack=None):
    c = CHUNK
    nt = tlen // c
    nb = GLA_SEQS
    n_gla = s_all.shape[0]
    qkvg = qkvg.reshape(nseq, tlen, GLA_MAIN)
    glr = glr.reshape(nseq, tlen, LANES)
    in_specs = [pl.BlockSpec((nb, c, GLA_DQ), lambda b, t: (b, t, 0)),
                pl.BlockSpec((nb, c, GLA_DQ), lambda b, t: (b, t, 1)),
                pl.BlockSpec((nb, c, GLA_DV), lambda b, t: (b, t, 1)),
                pl.BlockSpec((nb, c, GLA_DV), lambda b, t: (b, t, 2)),
                pl.BlockSpec((nb, c, LANES), lambda b, t: (b, t, 0)),
                pl.BlockSpec((None, nb, GLA_HEADS, HEAD_K, HEAD_V), lambda b, t: (j, b, 0, 0, 0),
                             pipeline_mode=pl.Buffered(1)),
                pl.BlockSpec((None, LANES, GLA_DQ), lambda b, t: (j, 0, 0)),
                pl.BlockSpec((None, 1, GLA_DQ), lambda b, t: (j, 0, 0)),
                pl.BlockSpec((None, 1, HEAD_V), lambda b, t: (j, 0, 0))]
    args = [qkvg, qkvg, qkvg, qkvg, glr, s_all, w["gla_w_gk2"], w["gla_b_gk"], w["gla_norm_w"]]
    body, aliases = _gla_kernel, {}
    if s_stack is not None:
        in_specs.append(pl.BlockSpec(memory_space=pl.ANY))
        args.append(s_stack)
        body, aliases = _gla_aliased_kernel, {len(args) - 1: 1}
    og, s_stack = pl.pallas_call(
        functools.partial(body, nt=nt, nb=nb),
        grid=(nseq // nb, nt),
        in_specs=in_specs,
        out_specs=[pl.BlockSpec((nb, c, GLA_DV), lambda b, t: (b, t, 0)),
                   pl.BlockSpec((None, nb, GLA_HEADS, HEAD_K, HEAD_V), lambda b, t: (j, b, 0, 0, 0))],
        out_shape=[jax.ShapeDtypeStruct((nseq, tlen, GLA_DV), BF16),
                   jax.ShapeDtypeStruct((n_gla, nseq, GLA_HEADS, HEAD_K, HEAD_V), F32)],
        scratch_shapes=[pltpu.VMEM((nb, GLA_HEADS, HEAD_V, HEAD_K), F32)],
        input_output_aliases=aliases,
        compiler_params=_params(("parallel", "arbitrary"), VMEM_LIMIT_MIB),
        name="gla_core")(*args)
    return og.reshape(nseq * tlen, GLA_DV), s_stack


def _trunk(x, h_all, conv_all, s_all, w, bf16=None):
    nseq, tlen, d = x.shape
    x = x.reshape(nseq * tlen, d)
    h_all = h_all[:, :, None, :]
    cast = bf16 is None
    made = []
    hs, cs, s_stack = [], [], None
    for layer in range(DEPTH):
        j = layer // 2
        wl = {} if cast else bf16[layer]
        if layer % 2 == 0:
            if cast:
                gu, wl["in"] = _norm_matmul(x, w["norm_mix"], layer, 2 * D_RNN, w_f32=w["rg_w_in"], j=j)
            else:
                gu = _norm_matmul(x, w["norm_mix"], layer, 2 * D_RNN, w_bf16=wl["in"])
            mixed, h_new, c_new = _rglru_core(gu, nseq, tlen, j, h_all, conv_all, w)
            hs.append(h_new[:, 0, :])
            cs.append(c_new)
            w_out = w["rg_w_out"]
        else:
            if cast:
                (qkvg, glr), wl["in"] = _norm_matmul(x, w["norm_mix"], layer, GLA_MAIN, side=GATE_RANK,
                                                     w_f32=w["gla_w_in"], j=j)
            else:
                qkvg, glr = _norm_matmul(x, w["norm_mix"], layer, GLA_MAIN, side=GATE_RANK, w_bf16=wl["in"])
            mixed, s_stack = _gla_core(qkvg, glr, nseq, tlen, j, s_all, w, s_stack)
            w_out = w["gla_w_out"]
        final = layer == DEPTH - 1
        if cast:
            x, wl["out"] = _matmul_res(mixed, x, w_f32=w_out, j=j)
            x, wl["ffn"] = _ffn(x, w["norm_ffn"], layer, w["norm_final"], final,
                                w_f32=(w["ffn_w_up"], w["ffn_w_down"]))
            made.append(wl)
        else:
            x = _matmul_res(mixed, x, w_bf16=wl["out"])
            x = _ffn(x, w["norm_ffn"], layer, w["norm_final"], final, w_bf16=wl["ffn"])
    return (x.reshape(nseq, tlen, d), jnp.stack(hs), jnp.stack(cs), s_stack), made


def _prepare_weights(norm_mix, norm_ffn, norm_final, rg_w_in, rg_conv_w, rg_conv_b, rg_w_a, rg_b_a, rg_w_x, rg_b_x,
                     rg_lambda, rg_w_out, gla_w_in, gla_w_gk2, gla_b_gk, gla_norm_w, gla_w_out, ffn_w_up, ffn_w_down):
    glr_pad = LANES - GATE_RANK
    return {
        "norm_mix": norm_mix[:, None, :],
        "norm_ffn": norm_ffn[:, None, :],
        "norm_final": norm_final[None, :],
        "rg_w_in": rg_w_in,
        "rg_conv_w": rg_conv_w,
        "rg_conv_b": rg_conv_b[:, None, :],
        "rg_w_a": rg_w_a.astype(BF16),
        "rg_b_a": rg_b_a[:, None, :],
        "rg_w_x": rg_w_x.astype(BF16),
        "rg_b_x": rg_b_x[:, None, :],
        "rg_lambda": rg_lambda[:, None, :],
        "rg_w_out": rg_w_out,
        "gla_w_in": gla_w_in,
        "gla_w_gk2": jnp.pad(gla_w_gk2, ((0, 0), (0, glr_pad), (0, 0))).astype(BF16),
        "gla_b_gk": gla_b_gk[:, None, :],
        "gla_norm_w": gla_norm_w[:, None, :],
        "gla_w_out": gla_w_out,
        "ffn_w_up": ffn_w_up,
        "ffn_w_down": ffn_w_down,
    }


def kernel(x_prompt, x_sample, state_rglru_h, state_rglru_conv, state_gla, norm_mix, norm_ffn, norm_final, rg_w_in, rg_conv_w, rg_conv_b, rg_w_a, rg_b_a, rg_w_x, rg_b_x, rg_lambda, rg_w_out, gla_w_in, gla_w_gk2, gla_b_gk, gla_norm_w, gla_w_out, ffn_w_up, ffn_w_down):
    w = _prepare_weights(norm_mix, norm_ffn, norm_final, rg_w_in, rg_conv_w, rg_conv_b, rg_w_a, rg_b_a, rg_w_x,
                         rg_b_x, rg_lambda, rg_w_out, gla_w_in, gla_w_gk2, gla_b_gk, gla_norm_w, gla_w_out,
                         ffn_w_up, ffn_w_down)
    bp = x_prompt.shape[0]
    n_rg = state_rglru_h.shape[0]
    n_gla = state_gla.shape[0]
    h0 = jnp.zeros((n_rg, bp, D_RNN), F32)
    c0 = jnp.zeros((n_rg, bp, CONV_W - 1, D_RNN), F32)
    s0 = jnp.zeros((n_gla, bp, GLA_HEADS, HEAD_K, HEAD_V), F32)
    (y_s, h_s, c_s, s_s), bf16 = _trunk(x_sample, state_rglru_h, state_rglru_conv, state_gla, w)
    (y_p, h_p, c_p, s_p), _ = _trunk(x_prompt, h0, c0, s0, w, bf16)
    return (y_p, y_s, h_p, c_p, s_p, h_s, c_s, s_s)
```

```python
import functools

import jax
import jax.numpy as jnp
from jax import lax
from jax.experimental import pallas as pl
from jax.experimental.pallas import tpu as pltpu

F32 = jnp.float32
BF16 = jnp.bfloat16

D_MODEL = 2048
DEPTH = 4
CHUNK = 64
EPS = 1e-6
D_RNN = D_MODEL
RG_BLOCKS = 8
RG_BW = D_RNN // RG_BLOCKS
CONV_W = 4
RG_C = 8.0
GLA_HEADS = 4
HEAD_K = 256
HEAD_V = 512
GATE_RANK = 16
GATE_NORM = 16.0
GLA_DQ = GLA_HEADS * HEAD_K
GLA_DV = GLA_HEADS * HEAD_V
GLA_MAIN = 2 * GLA_DQ + 2 * GLA_DV
D_FF = 5632

LANES = 128
SUBLANES = 8
NORM_ROWS = 16
NORM_UNROLL = 16
FFN_NORM_UNROLL = 8
MIB = 1024 * 1024
LOG2E = 1.4426950408889634
GELU_C1 = 0.7978845608028654
GELU_C2 = 0.044715 * GELU_C1

MATMUL_TM = 1024
MATMUL_TN = 1024
VMEM_WINDOW_MIB = 46
VMEM_LIMIT_MIB = 56
VMEM_LIMIT_SMALL_MIB = 32
FFN_TM = 1024
FFN_TF = 512
FFN_CAST_TF = 256
RGLRU_TC = 512
GLA_SEQS = 4


def _params(dims, vmem_mib):
    return pltpu.CompilerParams(dimension_semantics=dims, vmem_limit_bytes=vmem_mib * MIB)


def _rms(x, w):
    ms = jnp.mean(x * x, axis=-1, keepdims=True)
    return x * lax.rsqrt(ms + EPS) * w


def _log_sigmoid(x):
    return jnp.minimum(x, 0.0) - jnp.log(1.0 + jnp.exp(-jnp.abs(x)))


def _split3_bf16(x):
    hi = x.astype(BF16)
    r1 = x - hi.astype(F32)
    mid = r1.astype(BF16)
    lo = (r1 - mid.astype(F32)).astype(BF16)
    return hi, mid, lo


def _sqrt_nonneg(y):
    return jnp.exp2((0.5 * LOG2E) * jnp.log(y))


def _gelu_tanh(x):
    inner = x * (GELU_C1 + GELU_C2 * (x * x))
    return (0.5 * x) * (1.0 + jnp.tanh(inner))


def _norm_rows(src_ref, nw_ref, dst_ref, rows, dtype, unroll=NORM_UNROLL):
    nw = nw_ref[...]

    def body(r, carry):
        sl = pl.ds(pl.multiple_of(r * NORM_ROWS, NORM_ROWS), NORM_ROWS)
        dst_ref[sl, :] = _rms(src_ref[sl, :], nw).astype(dtype)
        return carry

    lax.fori_loop(0, rows // NORM_ROWS, body, 0, unroll=unroll)


def _norm_rows_inplace(ref, nw_ref, scale_ref, rows):
    def scales(r, carry):
        sl = pl.ds(pl.multiple_of(r * NORM_ROWS, NORM_ROWS), NORM_ROWS)
        x = ref[sl, :]
        scale_ref[sl, :] = lax.rsqrt(jnp.mean(x * x, axis=-1, keepdims=True) + EPS)
        return carry

    lax.fori_loop(0, rows // NORM_ROWS, scales, 0, unroll=NORM_UNROLL)
    nw = nw_ref[...]

    def apply(r, carry):
        sl = pl.ds(pl.multiple_of(r * NORM_ROWS, NORM_ROWS), NORM_ROWS)
        ref[sl, :] = ref[sl, :] * scale_ref[sl, :] * nw
        return carry

    lax.fori_loop(0, rows // NORM_ROWS, apply, 0, unroll=NORM_UNROLL)


def _norm_matmul_kernel(x_ref, nw_ref, w_ref, w2_ref, o_ref, o2_ref, hn_ref, *, tm):
    @pl.when(pl.program_id(1) == 0)
    def _():
        _norm_rows(x_ref, nw_ref, hn_ref, tm, BF16)
        if w2_ref is not None:
            o2_ref[...] = jnp.dot(hn_ref[...], w2_ref[...], preferred_element_type=F32)

    o_ref[...] = jnp.dot(hn_ref[...], w_ref[...], preferred_element_type=F32)


def _norm_matmul_plain_kernel(x_ref, nw_ref, w_ref, o_ref, hn_ref, *, tm):
    _norm_matmul_kernel(x_ref, nw_ref, w_ref, None, o_ref, None, hn_ref, tm=tm)


def _norm_matmul_cast_kernel(x_ref, nw_ref, w_ref, o_ref, wo_ref, hn_ref, *, tm):
    wo_ref[...] = w_ref[...].astype(BF16)
    _norm_matmul_kernel(x_ref, nw_ref, wo_ref, None, o_ref, None, hn_ref, tm=tm)


def _norm_matmul_cast2_kernel(x_ref, nw_ref, w_ref, w2_ref, o_ref, o2_ref, wo_ref, w2o_ref, hn_ref, *, tm, side):
    wo_ref[...] = w_ref[...].astype(BF16)

    @pl.when(pl.program_id(1) == 0)
    def _():
        lane = lax.broadcasted_iota(jnp.int32, w2_ref.shape, 1)
        w2o_ref[...] = jnp.where(lane < side, w2_ref[...], 0.0).astype(BF16)

    _norm_matmul_kernel(x_ref, nw_ref, wo_ref, w2o_ref, o_ref, o2_ref, hn_ref, tm=tm)


def _norm_matmul(x, nws, layer, n, side=0, w_bf16=None, w_f32=None, j=None):
    m, d = x.shape
    tm = min(m, MATMUL_TM)
    tn = MATMUL_TN
    w_buffers = 2
    if w_bf16 is not None:
        def windows_mib(rows, cols, wbuf):
            return (2 * rows * d * 4 + rows * d * 2 + wbuf * d * cols * 2 + 2 * rows * cols * 4) / MIB
        if m > tm and windows_mib(tm // 2, n, 1) <= VMEM_WINDOW_MIB:
            tm, tn, w_buffers = tm // 2, n, 1
        else:
            tn = max(c for c in range(MATMUL_TN, n + 1, MATMUL_TN // 2)
                     if n % c == 0 and windows_mib(tm, c, 2) <= VMEM_WINDOW_MIB)
    grid = (m // tm, n // tn)
    x_spec = pl.BlockSpec((tm, d), lambda i, c: (i, 0))
    nw_spec = pl.BlockSpec((None, 1, d), lambda i, c: (layer, 0, 0))
    w_spec = pl.BlockSpec((d, tn), lambda i, c: (0, c), pipeline_mode=pl.Buffered(w_buffers))
    w2_spec = pl.BlockSpec((d, LANES), lambda i, c: (0, 0))
    o_spec = pl.BlockSpec((tm, tn), lambda i, c: (i, c))
    o2_spec = pl.BlockSpec((tm, LANES), lambda i, c: (i, 0))
    o_shape = jax.ShapeDtypeStruct((m, n), F32)
    o2_shape = jax.ShapeDtypeStruct((m, LANES), F32)
    scratch = [pltpu.VMEM((tm, d), BF16)]
    cp = _params(("parallel", "arbitrary"), VMEM_LIMIT_MIB)
    if w_bf16 is not None:
        w, w2 = w_bf16
        if not side:
            return pl.pallas_call(
                functools.partial(_norm_matmul_plain_kernel, tm=tm), grid=grid,
                in_specs=[x_spec, nw_spec, w_spec], out_specs=o_spec, out_shape=o_shape,
                scratch_shapes=scratch, compiler_params=cp, name="norm_matmul")(x, nws, w)
        return pl.pallas_call(
            functools.partial(_norm_matmul_kernel, tm=tm), grid=grid,
            in_specs=[x_spec, nw_spec, w_spec, w2_spec], out_specs=[o_spec, o2_spec],
            out_shape=[o_shape, o2_shape], scratch_shapes=scratch, compiler_params=cp,
            name="norm_matmul2")(x, nws, w, w2)
    assert m == tm, "the casting variant rewrites the bf16 weights once per row block"
    wf_spec = pl.BlockSpec((None, d, tn), lambda i, c: (j, 0, c))
    w_shape = jax.ShapeDtypeStruct((d, n), BF16)
    if not side:
        out, w = pl.pallas_call(
            functools.partial(_norm_matmul_cast_kernel, tm=tm), grid=grid,
            in_specs=[x_spec, nw_spec, wf_spec], out_specs=[o_spec, w_spec], out_shape=[o_shape, w_shape],
            scratch_shapes=scratch, compiler_params=cp, name="norm_matmul_cast")(x, nws, w_f32)
        return out, (w, None)
    out, out2, w, w2 = pl.pallas_call(
        functools.partial(_norm_matmul_cast2_kernel, tm=tm, side=side), grid=grid,
        in_specs=[x_spec, nw_spec, wf_spec, pl.BlockSpec((None, d, LANES), lambda i, c: (j, 0, n // LANES))],
        out_specs=[o_spec, o2_spec, w_spec, w2_spec],
        out_shape=[o_shape, o2_shape, w_shape, jax.ShapeDtypeStruct((d, LANES), BF16)],
        scratch_shapes=scratch, compiler_params=cp, name="norm_matmul2_cast")(x, nws, w_f32, w_f32)
    return (out, out2), (w, w2)


def _matmul_res_kernel(a_ref, w_ref, x_ref, o_ref):
    o_ref[...] = x_ref[...] + jnp.dot(a_ref[...], w_ref[...], preferred_element_type=F32)


def _matmul_res_cast_kernel(a_ref, w_ref, x_ref, o_ref, wo_ref):
    wo_ref[...] = w_ref[...].astype(BF16)
    _matmul_res_kernel(a_ref, wo_ref, x_ref, o_ref)


def _matmul_res(a, x, w_bf16=None, w_f32=None, j=None):
    m, k = a.shape
    n = x.shape[1]
    tm = min(m, MATMUL_TM)
    tn = n if w_bf16 is not None else MATMUL_TN
    grid = (m // tm, n // tn)
    a_spec = pl.BlockSpec((tm, k), lambda i, c: (i, 0))
    w_spec = pl.BlockSpec((k, tn), lambda i, c: (0, c))
    xo_spec = pl.BlockSpec((tm, tn), lambda i, c: (i, c))
    o_shape = jax.ShapeDtypeStruct((m, n), F32)
    cp = _params(("parallel", "arbitrary"), VMEM_LIMIT_MIB)
    if w_bf16 is not None:
        w_res_spec = pl.BlockSpec((k, n), lambda i, c: (0, 0), pipeline_mode=pl.Buffered(1))
        return pl.pallas_call(
            _matmul_res_kernel, grid=grid, in_specs=[a_spec, w_res_spec, xo_spec], out_specs=xo_spec,
            out_shape=o_shape, compiler_params=cp, name="matmul_res")(a, w_bf16, x)
    assert m == tm, "the casting variant rewrites the bf16 weight once per row block"
    return pl.pallas_call(
        _matmul_res_cast_kernel, grid=grid,
        in_specs=[a_spec, pl.BlockSpec((None, k, tn), lambda i, c: (j, 0, c)), xo_spec],
        out_specs=[xo_spec, w_spec], out_shape=[o_shape, jax.ShapeDtypeStruct((k, n), BF16)],
        compiler_params=cp, name="matmul_res_cast")(a, w_f32, x)


def _ffn_kernel(x_ref, nw_ref, wg_ref, wu_ref, wd_ref, fw_ref, o_ref, hn_ref, scale_ref, *, tm, nf, final):
    f = pl.program_id(1)

    @pl.when(f == 0)
    def _():
        _norm_rows(x_ref, nw_ref, hn_ref, tm, BF16, unroll=FFN_NORM_UNROLL)
        o_ref[...] = x_ref[...]

    hn = hn_ref[...]
    gt = jnp.dot(hn, wg_ref[...], preferred_element_type=F32)
    up = jnp.dot(hn, wu_ref[...], preferred_element_type=F32)
    act = ((gt * jax.nn.sigmoid(gt)) * up).astype(BF16)
    o_ref[...] += jnp.dot(act, wd_ref[...], preferred_element_type=F32)

    if final:
        @pl.when(f == nf - 1)
        def _():
            _norm_rows_inplace(o_ref, fw_ref, scale_ref, tm)


def _ffn_cast_kernel(x_ref, nw_ref, wg_ref, wu_ref, wd_ref, fw_ref, o_ref, wgo_ref, wuo_ref, wdo_ref, hn_ref, scale_ref,
                     **kw):
    wgo_ref[...] = wg_ref[...].astype(BF16)
    wuo_ref[...] = wu_ref[...].astype(BF16)
    wdo_ref[...] = wd_ref[...].astype(BF16)
    _ffn_kernel(x_ref, nw_ref, wgo_ref, wuo_ref, wdo_ref, fw_ref, o_ref, hn_ref, scale_ref, **kw)


def _ffn(x, nws, layer, fw, final, w_bf16=None, w_f32=None):
    m, d = x.shape
    tm = min(m, FFN_TM)
    tf = FFN_TF if w_bf16 is not None else FFN_CAST_TF
    nf = D_FF // tf
    kw = dict(tm=tm, nf=nf, final=final)
    x_spec = pl.BlockSpec((tm, d), lambda i, f: (i, 0), pipeline_mode=pl.Buffered(1))
    nw_spec = pl.BlockSpec((None, 1, d), lambda i, f: (layer, 0, 0))
    fw_spec = pl.BlockSpec((1, d), lambda i, f: (0, 0))
    o_spec = pl.BlockSpec((tm, d), lambda i, f: (i, 0))
    o_shape = jax.ShapeDtypeStruct((m, d), F32)
    scratch = [pltpu.VMEM((tm, d), BF16), pltpu.VMEM((tm, 1), F32)]
    cp = _params(("parallel", "arbitrary"), VMEM_LIMIT_MIB)
    up_spec = pl.BlockSpec((d, tf), lambda i, f: (0, f))
    down_spec = pl.BlockSpec((tf, d), lambda i, f: (f, 0))
    if w_bf16 is not None:
        return pl.pallas_call(
            functools.partial(_ffn_kernel, **kw), grid=(m // tm, nf),
            in_specs=[x_spec, nw_spec, up_spec, up_spec, down_spec, fw_spec],
            out_specs=o_spec, out_shape=o_shape, scratch_shapes=scratch,
            compiler_params=cp, name="ffn")(x, nws, *w_bf16, fw)
    assert m == tm, "the casting variant rewrites the bf16 weights once per row block"
    w_ups, w_downs = w_f32
    out, wg, wu, wd = pl.pallas_call(
        functools.partial(_ffn_cast_kernel, **kw), grid=(1, nf),
        in_specs=[x_spec, nw_spec,
                  pl.BlockSpec((None, d, tf), lambda i, f: (layer, 0, f)),
                  pl.BlockSpec((None, d, tf), lambda i, f: (layer, 0, f + nf)),
                  pl.BlockSpec((None, tf, d), lambda i, f: (layer, f, 0)),
                  fw_spec],
        out_specs=[o_spec, up_spec, up_spec, down_spec],
        out_shape=[o_shape, jax.ShapeDtypeStruct((d, D_FF), BF16), jax.ShapeDtypeStruct((d, D_FF), BF16),
                   jax.ShapeDtypeStruct((D_FF, d), BF16)],
        scratch_shapes=scratch, compiler_params=cp, name="ffn_cast")(x, nws, w_ups, w_ups, w_downs, fw)
    return out, (wg, wu, wd)


def _rglru_kernel(gate_ref, u_ref, h0_ref, c0_ref, cw_ref, cb_ref, wa_ref, ba_ref, wx_ref, bx_ref, lam_ref,
                  hg_ref, hout_ref, cout_ref, uext_ref, *, tc, nt):
    t = pl.program_id(2)
    pad = SUBLANES

    @pl.when(t == 0)
    def _():
        uext_ref[...] = jnp.zeros((pad, RG_BW), F32)
        uext_ref[pad - (CONV_W - 1):pad, :] = c0_ref[...]
        hout_ref[...] = h0_ref[...]

    u = u_ref[...]
    ext = jnp.concatenate([uext_ref[...], u], axis=0)
    cw = cw_ref[...]
    acc = pltpu.roll(ext, 3, 0)[pad:, :] * cw[0:1, :]
    acc = acc + pltpu.roll(ext, 2, 0)[pad:, :] * cw[1:2, :]
    acc = acc + pltpu.roll(ext, 1, 0)[pad:, :] * cw[2:3, :]
    acc = acc + u * cw[3:4, :]
    uc = cb_ref[...] + acc
    ucb = uc.astype(BF16)
    r = jax.nn.sigmoid(jnp.dot(ucb, wa_ref[...], preferred_element_type=F32) + ba_ref[...])
    i = jax.nn.sigmoid(jnp.dot(ucb, wx_ref[...], preferred_element_type=F32) + bx_ref[...])
    a = jnp.exp2(r * ((RG_C * LOG2E) * _log_sigmoid(lam_ref[...])))
    b = _sqrt_nonneg(1.0 - a * a) * (i * uc)

    groups = tc // SUBLANES
    a3 = a.reshape(groups, SUBLANES, RG_BW)
    b3 = b.reshape(groups, SUBLANES, RG_BW)
    row = lax.broadcasted_iota(jnp.int32, (groups, SUBLANES, RG_BW), 1)
    for s in (1, 2, 4):
        keep = row >= s
        a_prev = pltpu.roll(a3, s, 1)
        b_prev = pltpu.roll(b3, s, 1)
        b3 = jnp.where(keep, b3 + a3 * b_prev, b3)
        a3 = jnp.where(keep, a3 * a_prev, a3)
    h_prev = jnp.broadcast_to(hout_ref[...], (SUBLANES, RG_BW))
    hs = []
    for g in range(groups):
        h_g = a3[g] * h_prev + b3[g]
        hs.append(h_g)
        h_prev = jnp.broadcast_to(h_g[SUBLANES - 1:SUBLANES, :], (SUBLANES, RG_BW))
    h = jnp.concatenate(hs, axis=0)
    hout_ref[...] = hs[-1][SUBLANES - 1:SUBLANES, :]
    hg_ref[...] = (h * _gelu_tanh(gate_ref[...])).astype(BF16)

    uext_ref[...] = u[tc - pad:, :]

    @pl.when(t == nt - 1)
    def _():
        cout_ref[...] = uext_ref[pad - (CONV_W - 1):pad, :]


def _rglru_core(gu, nseq, tlen, j, h_all, c_all, w):
    m = nseq * tlen
    tc = min(tlen, RGLRU_TC)
    nt = tlen // tc
    bw = RG_BW
    row = lambda b, n, t: b * nt + t
    vec = pl.BlockSpec((None, 1, bw), lambda b, n, t: (j, 0, n))
    gate_spec = pl.BlockSpec((tc, bw), lambda b, n, t: (row(b, n, t), n))
    u_spec = pl.BlockSpec((tc, bw), lambda b, n, t: (row(b, n, t), RG_BLOCKS + n))
    hin_spec = pl.BlockSpec((None, None, 1, bw), lambda b, n, t: (j, b, 0, n))
    cin_spec = pl.BlockSpec((None, None, CONV_W - 1, bw), lambda b, n, t: (j, b, 0, n))
    hout_spec = pl.BlockSpec((None, 1, bw), lambda b, n, t: (b, 0, n))
    cout_spec = pl.BlockSpec((None, CONV_W - 1, bw), lambda b, n, t: (b, 0, n))
    w_spec = pl.BlockSpec((None, None, bw, bw), lambda b, n, t: (j, n, 0, 0))
    return pl.pallas_call(
        functools.partial(_rglru_kernel, tc=tc, nt=nt),
        grid=(nseq, RG_BLOCKS, nt),
        in_specs=[gate_spec, u_spec, hin_spec, cin_spec,
                  pl.BlockSpec((None, CONV_W, bw), lambda b, n, t: (j, 0, n)), vec,
                  w_spec, vec, w_spec, vec, vec],
        out_specs=[pl.BlockSpec((tc, bw), lambda b, n, t: (row(b, n, t), n)), hout_spec, cout_spec],
        out_shape=[jax.ShapeDtypeStruct((m, D_RNN), BF16),
                   jax.ShapeDtypeStruct((nseq, 1, D_RNN), F32),
                   jax.ShapeDtypeStruct((nseq, CONV_W - 1, D_RNN), F32)],
        scratch_shapes=[pltpu.VMEM((SUBLANES, bw), F32)],
        compiler_params=_params(("parallel", "parallel", "arbitrary"), VMEM_LIMIT_SMALL_MIB),
        name="rglru_core")(gu, gu, h_all, c_all, w["rg_conv_w"], w["rg_conv_b"], w["rg_w_a"], w["rg_b_a"],
                           w["rg_w_x"], w["rg_b_x"], w["rg_lambda"])


def _gla_kernel(q_ref, k_ref, v_ref, g_ref, glr_ref, s0_ref, w2_ref, bgk_ref, gnw_ref,
                o_ref, sout_ref, st_ref, *, nt, nb):
    t = pl.program_id(1)

    @pl.when(t == 0)
    def _():
        for s in range(nb):
            for h in range(GLA_HEADS):
                st_ref[s, h] = s0_ref[s, h].T

    c = CHUNK
    glr = glr_ref[...].reshape(nb * c, LANES).astype(BF16)
    gk_all = jnp.dot(glr, w2_ref[...], preferred_element_type=F32) + bgk_ref[...]
    gk_all = _log_sigmoid(gk_all) * (1.0 / GATE_NORM)
    rows = lax.broadcasted_iota(jnp.int32, (c, c), 0)
    cols = lax.broadcasted_iota(jnp.int32, (c, c), 1)
    tri = rows >= cols
    tri_b = tri.astype(BF16)
    gnw = gnw_ref[...]
    nt_dims = (((1,), (1,)), ((), ()))
    tn_dims = (((0,), (0,)), ((), ()))
    prep = []
    for s in range(nb):
        gk = gk_all[s * c:(s + 1) * c, :]
        bcum = sum(jnp.dot(tri_b, piece, preferred_element_type=F32) for piece in _split3_bf16(gk))
        blast = bcum[c - 1:c, :]
        k = k_ref[s]
        qe = ((q_ref[s] * (HEAD_K ** -0.5)) * jnp.exp(bcum)).astype(BF16)
        ke = (k * jnp.exp(-bcum)).astype(BF16)
        kd = (k * jnp.exp(blast - bcum)).astype(BF16)
        prep.append((qe, ke, kd, jnp.exp(blast)))
    for h in range(GLA_HEADS):
        ks = slice(h * HEAD_K, (h + 1) * HEAD_K)
        vs = slice(h * HEAD_V, (h + 1) * HEAD_V)
        for s in range(nb):
            qe, ke, kd, decay = prep[s]
            v_h = v_ref[s, :, vs].astype(BF16)
            att = lax.dot_general(qe[:, ks], ke[:, ks], nt_dims, preferred_element_type=F32)
            att = jnp.where(tri, att, 0.0).astype(BF16)
            st = st_ref[s, h]
            o = jnp.dot(att, v_h, preferred_element_type=F32)
            o = o + lax.dot_general(qe[:, ks], st.astype(BF16), nt_dims, preferred_element_type=F32)
            st_ref[s, h] = st * decay[:, ks] + lax.dot_general(v_h, kd[:, ks], tn_dims,
                                                               preferred_element_type=F32)
            on = o * lax.rsqrt(jnp.mean(o * o, axis=-1, keepdims=True) + EPS) * gnw
            g_h = g_ref[s, :, vs]
            o_ref[s, :, vs] = (on * (g_h * jax.nn.sigmoid(g_h))).astype(BF16)

    @pl.when(t == nt - 1)
    def _():
        for s in range(nb):
            for h in range(GLA_HEADS):
                sout_ref[s, h] = st_ref[s, h].T


def _gla_aliased_kernel(*refs, **kw):
    _gla_kernel(*refs[:9], *refs[10:], **kw)


def _gla_core(qkvg, glr, nseq, tlen, j, s_all, w, s_stack):
    c = CHUNK
    nt = tlen // c
    nb = GLA_SEQS
    n_gla = s_all.shape[0]
    qkvg = qkvg.reshape(nseq, tlen, GLA_MAIN)
    glr = glr.reshape(nseq, tlen, LANES)
    in_specs = [pl.BlockSpec((nb, c, GLA_DQ), lambda b, t: (b, t, 0)),
                pl.BlockSpec((nb, c, GLA_DQ), lambda b, t: (b, t, 1)),
                pl.BlockSpec((nb, c, GLA_DV), lambda b, t: (b, t, 1)),
                pl.BlockSpec((nb, c, GLA_DV), lambda b, t: (b, t, 2)),
                pl.BlockSpec((nb, c, LANES), lambda b, t: (b, t, 0)),
                pl.BlockSpec((None, nb, GLA_HEADS, HEAD_K, HEAD_V), lambda b, t: (j, b, 0, 0, 0),
                             pipeline_mode=pl.Buffered(1)),
                pl.BlockSpec((None, LANES, GLA_DQ), lambda b, t: (j, 0, 0)),
                pl.BlockSpec((None, 1, GLA_DQ), lambda b, t: (j, 0, 0)),
                pl.BlockSpec((None, 1, HEAD_V), lambda b, t: (j, 0, 0)),
                pl.BlockSpec(memory_space=pl.ANY)]
    args = [qkvg, qkvg, qkvg, qkvg, glr, s_all, w["gla_w_gk2"], w["gla_b_gk"], w["gla_norm_w"], s_stack]
    aliases = {len(args) - 1: 1}
    og, s_stack = pl.pallas_call(
        functools.partial(_gla_aliased_kernel, nt=nt, nb=nb),
        grid=(nseq // nb, nt),
        in_specs=in_specs,
        out_specs=[pl.BlockSpec((nb, c, GLA_DV), lambda b, t: (b, t, 0)),
                   pl.BlockSpec((None, nb, GLA_HEADS, HEAD_K, HEAD_V), lambda b, t: (j, b, 0, 0, 0))],
        out_shape=[jax.ShapeDtypeStruct((nseq, tlen, GLA_DV), BF16),
                   jax.ShapeDtypeStruct((n_gla, nseq, GLA_HEADS, HEAD_K, HEAD_V), F32)],
        scratch_shapes=[pltpu.VMEM((nb, GLA_HEADS, HEAD_V, HEAD_K), F32)],
        input_output_aliases=aliases,
        compiler_params=_params(("parallel", "arbitrary"), VMEM_LIMIT_MIB),
        name="gla_core")(*args)
    return og.reshape(nseq * tlen, GLA_DV), s_stack


def _trunk(x, h_all, conv_all, s_all, w, bf16=None):
    nseq, tlen, d = x.shape
    x = x.reshape(nseq * tlen, d)
    h_all = h_all[:, :, None, :]
    cast = bf16 is None
    made = []
    hs, cs, s_stack = [], [], jnp.zeros_like(s_all)
    for layer in range(DEPTH):
        j = layer // 2
        wl = {} if cast else bf16[layer]
        if layer % 2 == 0:
            if cast:
                gu, wl["in"] = _norm_matmul(x, w["norm_mix"], layer, 2 * D_RNN, w_f32=w["rg_w_in"], j=j)
            else:
                gu = _norm_matmul(x, w["norm_mix"], layer, 2 * D_RNN, w_bf16=wl["in"])
            mixed, h_new, c_new = _rglru_core(gu, nseq, tlen, j, h_all, conv_all, w)
            hs.append(h_new[:, 0, :])
            cs.append(c_new)
            w_out = w["rg_w_out"]
        else:
            if cast:
                (qkvg, glr), wl["in"] = _norm_matmul(x, w["norm_mix"], layer, GLA_MAIN, side=GATE_RANK,
                                                     w_f32=w["gla_w_in"], j=j)
            else:
                qkvg, glr = _norm_matmul(x, w["norm_mix"], layer, GLA_MAIN, side=GATE_RANK, w_bf16=wl["in"])
            mixed, s_stack = _gla_core(qkvg, glr, nseq, tlen, j, s_all, w, s_stack)
            w_out = w["gla_w_out"]
        final = layer == DEPTH - 1
        if cast:
            x, wl["out"] = _matmul_res(mixed, x, w_f32=w_out, j=j)
            x, wl["ffn"] = _ffn(x, w["norm_ffn"], layer, w["norm_final"], final,
                                w_f32=(w["ffn_w_up"], w["ffn_w_down"]))
            made.append(wl)
        else:
            x = _matmul_res(mixed, x, w_bf16=wl["out"])
            x = _ffn(x, w["norm_ffn"], layer, w["norm_final"], final, w_bf16=wl["ffn"])
    return (x.reshape(nseq, tlen, d), jnp.stack(hs), jnp.stack(cs), s_stack), made


def _prepare_weights(norm_mix, norm_ffn, norm_final, rg_w_in, rg_conv_w, rg_conv_b, rg_w_a, rg_b_a, rg_w_x, rg_b_x,
                     rg_lambda, rg_w_out, gla_w_in, gla_w_gk2, gla_b_gk, gla_norm_w, gla_w_out, ffn_w_up, ffn_w_down):
    glr_pad = LANES - GATE_RANK
    return {
        "norm_mix": norm_mix[:, None, :],
        "norm_ffn": norm_ffn[:, None, :],
        "norm_final": norm_final[None, :],
        "rg_w_in": rg_w_in,
        "rg_conv_w": rg_conv_w,
        "rg_conv_b": rg_conv_b[:, None, :],
        "rg_w_a": rg_w_a.astype(BF16),
        "rg_b_a": rg_b_a[:, None, :],
        "rg_w_x": rg_w_x.astype(BF16),
        "rg_b_x": rg_b_x[:, None, :],
        "rg_lambda": rg_lambda[:, None, :],
        "rg_w_out": rg_w_out,
        "gla_w_in": gla_w_in,
        "gla_w_gk2": jnp.pad(gla_w_gk2, ((0, 0), (0, glr_pad), (0, 0))).astype(BF16),
        "gla_b_gk": gla_b_gk[:, None, :],
        "gla_norm_w": gla_norm_w[:, None, :],
        "gla_w_out": gla_w_out,
        "ffn_w_up": ffn_w_up,
        "ffn_w_down": ffn_w_down,
    }


def kernel(x_prompt, x_sample, state_rglru_h, state_rglru_conv, state_gla, norm_mix, norm_ffn, norm_final, rg_w_in, rg_conv_w, rg_conv_b, rg_w_a, rg_b_a, rg_w_x, rg_b_x, rg_lambda, rg_w_out, gla_w_in, gla_w_gk2, gla_b_gk, gla_norm_w, gla_w_out, ffn_w_up, ffn_w_down):
    w = _prepare_weights(norm_mix, norm_ffn, norm_final, rg_w_in, rg_conv_w, rg_conv_b, rg_w_a, rg_b_a, rg_w_x,
                         rg_b_x, rg_lambda, rg_w_out, gla_w_in, gla_w_gk2, gla_b_gk, gla_norm_w, gla_w_out,
                         ffn_w_up, ffn_w_down)
    bp = x_prompt.shape[0]
    n_rg = state_rglru_h.shape[0]
    n_gla = state_gla.shape[0]
    h0 = jnp.zeros((n_rg, bp, D_RNN), F32)
    c0 = jnp.zeros((n_rg, bp, CONV_W - 1, D_RNN), F32)
    s0 = jnp.zeros((n_gla, bp, GLA_HEADS, HEAD_K, HEAD_V), F32)
    (y_s, h_s, c_s, s_s), bf16 = _trunk(x_sample, state_rglru_h, state_rglru_conv, state_gla, w)
    (y_p, h_p, c_p, s_p), _ = _trunk(x_prompt, h0, c0, s0, w, bf16)
    return (y_p, y_s, h_p, c_p, s_p, h_s, c_s, s_s)
```

```python
import functools

import jax
import jax.numpy as jnp
from jax import lax
from jax.experimental import pallas as pl
from jax.experimental.pallas import tpu as pltpu

F32 = jnp.float32
BF16 = jnp.bfloat16

D_MODEL = 2048
DEPTH = 4
CHUNK = 64
EPS = 1e-6
D_RNN = D_MODEL
RG_BLOCKS = 8
RG_BW = D_RNN // RG_BLOCKS
CONV_W = 4
RG_C = 8.0
GLA_HEADS = 4
HEAD_K = 256
HEAD_V = 512
GATE_RANK = 16
GATE_NORM = 16.0
GLA_DQ = GLA_HEADS * HEAD_K
GLA_DV = GLA_HEADS * HEAD_V
GLA_MAIN = 2 * GLA_DQ + 2 * GLA_DV
D_FF = 5632

LANES = 128
SUBLANES = 8
NORM_ROWS = 16
NORM_UNROLL = 16
FFN_NORM_UNROLL = 8
MIB = 1024 * 1024
LOG2E = 1.4426950408889634
GELU_C1 = 0.7978845608028654
GELU_C2 = 0.044715 * GELU_C1

MATMUL_TM = 1024
MATMUL_TN = 1024
VMEM_WINDOW_MIB = 46
VMEM_LIMIT_MIB = 56
VMEM_LIMIT_SMALL_MIB = 32
FFN_TM = 1024
FFN_TF = 512
FFN_CAST_TF = 256
RGLRU_TC = 512
GLA_SEQS = 4


def _params(dims, vmem_mib):
    return pltpu.CompilerParams(dimension_semantics=dims, vmem_limit_bytes=vmem_mib * MIB)


def _rms(x, w):
    ms = jnp.mean(x * x, axis=-1, keepdims=True)
    return x * lax.rsqrt(ms + EPS) * w


def _log_sigmoid(x):
    return jnp.minimum(x, 0.0) - jnp.log(1.0 + jnp.exp(-jnp.abs(x)))


def _split3_bf16(x):
    hi = x.astype(BF16)
    r1 = x - hi.astype(F32)
    mid = r1.astype(BF16)
    lo = (r1 - mid.astype(F32)).astype(BF16)
    return hi, mid, lo


def _sqrt_nonneg(y):
    return jnp.exp2((0.5 * LOG2E) * jnp.log(y))


def _gelu_tanh(x):
    inner = x * (GELU_C1 + GELU_C2 * (x * x))
    return (0.5 * x) * (1.0 + jnp.tanh(inner))


def _norm_rows(src_ref, nw_ref, dst_ref, rows, dtype, unroll=NORM_UNROLL):
    nw = nw_ref[...]

    def body(r, carry):
        sl = pl.ds(pl.multiple_of(r * NORM_ROWS, NORM_ROWS), NORM_ROWS)
        dst_ref[sl, :] = _rms(src_ref[sl, :], nw).astype(dtype)
        return carry

    lax.fori_loop(0, rows // NORM_ROWS, body, 0, unroll=unroll)


def _norm_rows_inplace(ref, nw_ref, scale_ref, rows):
    def scales(r, carry):
        sl = pl.ds(pl.multiple_of(r * NORM_ROWS, NORM_ROWS), NORM_ROWS)
        x = ref[sl, :]
        scale_ref[sl, :] = lax.rsqrt(jnp.mean(x * x, axis=-1, keepdims=True) + EPS)
        return carry

    lax.fori_loop(0, rows // NORM_ROWS, scales, 0, unroll=NORM_UNROLL)
    nw = nw_ref[...]

    def apply(r, carry):
        sl = pl.ds(pl.multiple_of(r * NORM_ROWS, NORM_ROWS), NORM_ROWS)
        ref[sl, :] = ref[sl, :] * scale_ref[sl, :] * nw
        return carry

    lax.fori_loop(0, rows // NORM_ROWS, apply, 0, unroll=NORM_UNROLL)


def _norm_matmul_kernel(x_ref, nw_ref, w_ref, w2_ref, o_ref, o2_ref, hn_ref, *, tm):
    @pl.when(pl.program_id(1) == 0)
    def _():
        _norm_rows(x_ref, nw_ref, hn_ref, tm, BF16)
        if w2_ref is not None:
            o2_ref[...] = jnp.dot(hn_ref[...], w2_ref[...], preferred_element_type=F32)

    o_ref[...] = jnp.dot(hn_ref[...], w_ref[...], preferred_element_type=F32)


def _norm_matmul_plain_kernel(x_ref, nw_ref, w_ref, o_ref, hn_ref, *, tm):
    _norm_matmul_kernel(x_ref, nw_ref, w_ref, None, o_ref, None, hn_ref, tm=tm)


def _norm_matmul_cast_kernel(x_ref, nw_ref, w_ref, o_ref, wo_ref, hn_ref, *, tm):
    wo_ref[...] = w_ref[...].astype(BF16)
    _norm_matmul_kernel(x_ref, nw_ref, wo_ref, None, o_ref, None, hn_ref, tm=tm)


def _norm_matmul_cast2_kernel(x_ref, nw_ref, w_ref, w2_ref, o_ref, o2_ref, wo_ref, w2o_ref, hn_ref, *, tm, side):
    wo_ref[...] = w_ref[...].astype(BF16)

    @pl.when(pl.program_id(1) == 0)
    def _():
        lane = lax.broadcasted_iota(jnp.int32, w2_ref.shape, 1)
        w2o_ref[...] = jnp.where(lane < side, w2_ref[...], 0.0).astype(BF16)

    _norm_matmul_kernel(x_ref, nw_ref, wo_ref, w2o_ref, o_ref, o2_ref, hn_ref, tm=tm)


def _norm_matmul(x, nws, layer, n, side=0, w_bf16=None, w_f32=None, j=None):
    m, d = x.shape
    tm = min(m, MATMUL_TM)
    tn = MATMUL_TN
    w_buffers = 2
    if w_bf16 is not None:
        def windows_mib(rows, cols, wbuf):
            return (2 * rows * d * 4 + rows * d * 2 + wbuf * d * cols * 2 + 2 * rows * cols * 4) / MIB
        if m > tm and windows_mib(tm // 2, n, 1) <= VMEM_WINDOW_MIB:
            tm, tn, w_buffers = tm // 2, n, 1
        else:
            tn = max(c for c in range(MATMUL_TN, n + 1, MATMUL_TN // 2)
                     if n % c == 0 and windows_mib(tm, c, 2) <= VMEM_WINDOW_MIB)
    grid = (m // tm, n // tn)
    x_spec = pl.BlockSpec((tm, d), lambda i, c: (i, 0))
    nw_spec = pl.BlockSpec((None, 1, d), lambda i, c: (layer, 0, 0))
    w_spec = pl.BlockSpec((d, tn), lambda i, c: (0, c), pipeline_mode=pl.Buffered(w_buffers))
    w2_spec = pl.BlockSpec((d, LANES), lambda i, c: (0, 0))
    o_spec = pl.BlockSpec((tm, tn), lambda i, c: (i, c))
    o2_spec = pl.BlockSpec((tm, LANES), lambda i, c: (i, 0))
    o_shape = jax.ShapeDtypeStruct((m, n), F32)
    o2_shape = jax.ShapeDtypeStruct((m, LANES), F32)
    scratch = [pltpu.VMEM((tm, d), BF16)]
    cp = _params(("parallel", "arbitrary"), VMEM_LIMIT_MIB)
    if w_bf16 is not None:
        w, w2 = w_bf16
        if not side:
            return pl.pallas_call(
                functools.partial(_norm_matmul_plain_kernel, tm=tm), grid=grid,
                in_specs=[x_spec, nw_spec, w_spec], out_specs=o_spec, out_shape=o_shape,
                scratch_shapes=scratch, compiler_params=cp, name="norm_matmul")(x, nws, w)
        return pl.pallas_call(
            functools.partial(_norm_matmul_kernel, tm=tm), grid=grid,
            in_specs=[x_spec, nw_spec, w_spec, w2_spec], out_specs=[o_spec, o2_spec],
            out_shape=[o_shape, o2_shape], scratch_shapes=scratch, compiler_params=cp,
            name="norm_matmul2")(x, nws, w, w2)
    assert m == tm, "the casting variant rewrites the bf16 weights once per row block"
    wf_spec = pl.BlockSpec((None, d, tn), lambda i, c: (j, 0, c))
    w_shape = jax.ShapeDtypeStruct((d, n), BF16)
    if not side:
        out, w = pl.pallas_call(
            functools.partial(_norm_matmul_cast_kernel, tm=tm), grid=grid,
            in_specs=[x_spec, nw_spec, wf_spec], out_specs=[o_spec, w_spec], out_shape=[o_shape, w_shape],
            scratch_shapes=scratch, compiler_params=cp, name="norm_matmul_cast")(x, nws, w_f32)
        return out, (w, None)
    out, out2, w, w2 = pl.pallas_call(
        functools.partial(_norm_matmul_cast2_kernel, tm=tm, side=side), grid=grid,
        in_specs=[x_spec, nw_spec, wf_spec, pl.BlockSpec((None, d, LANES), lambda i, c: (j, 0, n // LANES))],
        out_specs=[o_spec, o2_spec, w_spec, w2_spec],
        out_shape=[o_shape, o2_shape, w_shape, jax.ShapeDtypeStruct((d, LANES), BF16)],
        scratch_shapes=scratch, compiler_params=cp, name="norm_matmul2_cast")(x, nws, w_f32, w_f32)
    return (out, out2), (w, w2)


def _matmul_res_kernel(a_ref, w_ref, x_ref, o_ref):
    o_ref[...] = x_ref[...] + jnp.dot(a_ref[...], w_ref[...], preferred_element_type=F32)


def _matmul_res_cast_kernel(a_ref, w_ref, x_ref, o_ref, wo_ref):
    wo_ref[...] = w_ref[...].astype(BF16)
    _matmul_res_kernel(a_ref, wo_ref, x_ref, o_ref)


def _matmul_res(a, x, w_bf16=None, w_f32=None, j=None):
    m, k = a.shape
    n = x.shape[1]
    tm = min(m, MATMUL_TM)
    tn = n if w_bf16 is not None else MATMUL_TN
    grid = (m // tm, n // tn)
    a_spec = pl.BlockSpec((tm, k), lambda i, c: (i, 0))
    w_spec = pl.BlockSpec((k, tn), lambda i, c: (0, c))
    xo_spec = pl.BlockSpec((tm, tn), lambda i, c: (i, c))
    o_shape = jax.ShapeDtypeStruct((m, n), F32)
    cp = _params(("parallel", "arbitrary"), VMEM_LIMIT_MIB)
    if w_bf16 is not None:
        w_res_spec = pl.BlockSpec((k, n), lambda i, c: (0, 0), pipeline_mode=pl.Buffered(1))
        return pl.pallas_call(
            _matmul_res_kernel, grid=grid, in_specs=[a_spec, w_res_spec, xo_spec], out_specs=xo_spec,
            out_shape=o_shape, compiler_params=cp, name="matmul_res")(a, w_bf16, x)
    assert m == tm, "the casting variant rewrites the bf16 weight once per row block"
    return pl.pallas_call(
        _matmul_res_cast_kernel, grid=grid,
        in_specs=[a_spec, pl.BlockSpec((None, k, tn), lambda i, c: (j, 0, c)), xo_spec],
        out_specs=[xo_spec, w_spec], out_shape=[o_shape, jax.ShapeDtypeStruct((k, n), BF16)],
        compiler_params=cp, name="matmul_res_cast")(a, w_f32, x)


def _ffn_kernel(x_ref, nw_ref, wg_ref, wu_ref, wd_ref, fw_ref, o_ref, hn_ref, scale_ref, *, tm, nf, final):
    f = pl.program_id(1)

    @pl.when(f == 0)
    def _():
        _norm_rows(x_ref, nw_ref, hn_ref, tm, BF16, unroll=FFN_NORM_UNROLL)
        o_ref[...] = x_ref[...]

    hn = hn_ref[...]
    gt = jnp.dot(hn, wg_ref[...], preferred_element_type=F32)
    up = jnp.dot(hn, wu_ref[...], preferred_element_type=F32)
    act = ((gt * jax.nn.sigmoid(gt)) * up).astype(BF16)
    o_ref[...] += jnp.dot(act, wd_ref[...], preferred_element_type=F32)

    if final:
        @pl.when(f == nf - 1)
        def _():
            _norm_rows_inplace(o_ref, fw_ref, scale_ref, tm)


def _ffn_cast_kernel(x_ref, nw_ref, wg_ref, wu_ref, wd_ref, fw_ref, o_ref, wgo_ref, wuo_ref, wdo_ref, hn_ref, scale_ref,
                     **kw):
    wgo_ref[...] = wg_ref[...].astype(BF16)
    wuo_ref[...] = wu_ref[...].astype(BF16)
    wdo_ref[...] = wd_ref[...].astype(BF16)
    _ffn_kernel(x_ref, nw_ref, wgo_ref, wuo_ref, wdo_ref, fw_ref, o_ref, hn_ref, scale_ref, **kw)


def _ffn(x, nws, layer, fw, final, w_bf16=None, w_f32=None):
    m, d = x.shape
    tm = min(m, FFN_TM)
    tf = FFN_TF if w_bf16 is not None else FFN_CAST_TF
    nf = D_FF // tf
    kw = dict(tm=tm, nf=nf, final=final)
    x_spec = pl.BlockSpec((tm, d), lambda i, f: (i, 0), pipeline_mode=pl.Buffered(1))
    nw_spec = pl.BlockSpec((None, 1, d), lambda i, f: (layer, 0, 0))
    fw_spec = pl.BlockSpec((1, d), lambda i, f: (0, 0))
    o_spec = pl.BlockSpec((tm, d), lambda i, f: (i, 0))
    o_shape = jax.ShapeDtypeStruct((m, d), F32)
    scratch = [pltpu.VMEM((tm, d), BF16), pltpu.VMEM((tm, 1), F32)]
    cp = _params(("parallel", "arbitrary"), VMEM_LIMIT_MIB)
    up_spec = pl.BlockSpec((d, tf), lambda i, f: (0, f))
    down_spec = pl.BlockSpec((tf, d), lambda i, f: (f, 0))
    if w_bf16 is not None:
        return pl.pallas_call(
            functools.partial(_ffn_kernel, **kw), grid=(m // tm, nf),
            in_specs=[x_spec, nw_spec, up_spec, up_spec, down_spec, fw_spec],
            out_specs=o_spec, out_shape=o_shape, scratch_shapes=scratch,
            compiler_params=cp, name="ffn")(x, nws, *w_bf16, fw)
    assert m == tm, "the casting variant rewrites the bf16 weights once per row block"
    w_ups, w_downs = w_f32
    out, wg, wu, wd = pl.pallas_call(
        functools.partial(_ffn_cast_kernel, **kw), grid=(1, nf),
        in_specs=[x_spec, nw_spec,
                  pl.BlockSpec((None, d, tf), lambda i, f: (layer, 0, f)),
                  pl.BlockSpec((None, d, tf), lambda i, f: (layer, 0, f + nf)),
                  pl.BlockSpec((None, tf, d), lambda i, f: (layer, f, 0)),
                  fw_spec],
        out_specs=[o_spec, up_spec, up_spec, down_spec],
        out_shape=[o_shape, jax.ShapeDtypeStruct((d, D_FF), BF16), jax.ShapeDtypeStruct((d, D_FF), BF16),
                   jax.ShapeDtypeStruct((D_FF, d), BF16)],
        scratch_shapes=scratch, compiler_params=cp, name="ffn_cast")(x, nws, w_ups, w_ups, w_downs, fw)
    return out, (wg, wu, wd)


def _rglru_kernel(gate_ref, u_ref, h0_ref, c0_ref, cw_ref, cb_ref, wa_ref, ba_ref, wx_ref, bx_ref, lam_ref,
                  hg_ref, hout_ref, cout_ref, uext_ref, *, tc, nt):
    t = pl.program_id(2)
    pad = SUBLANES

    @pl.when(t == 0)
    def _():
        uext_ref[...] = jnp.zeros((pad, RG_BW), F32)
        uext_ref[pad - (CONV_W - 1):pad, :] = c0_ref[...]
        hout_ref[...] = h0_ref[...]

    u = u_ref[...]
    ext = jnp.concatenate([uext_ref[...], u], axis=0)
    cw = cw_ref[...]
    acc = pltpu.roll(ext, 3, 0)[pad:, :] * cw[0:1, :]
    acc = acc + pltpu.roll(ext, 2, 0)[pad:, :] * cw[1:2, :]
    acc = acc + pltpu.roll(ext, 1, 0)[pad:, :] * cw[2:3, :]
    acc = acc + u * cw[3:4, :]
    uc = cb_ref[...] + acc
    ucb = uc.astype(BF16)
    r = jax.nn.sigmoid(jnp.dot(ucb, wa_ref[...], preferred_element_type=F32) + ba_ref[...])
    i = jax.nn.sigmoid(jnp.dot(ucb, wx_ref[...], preferred_element_type=F32) + bx_ref[...])
    a = jnp.exp2(r * ((RG_C * LOG2E) * _log_sigmoid(lam_ref[...])))
    b = _sqrt_nonneg(1.0 - a * a) * (i * uc)

    groups = tc // SUBLANES
    a3 = a.reshape(groups, SUBLANES, RG_BW)
    b3 = b.reshape(groups, SUBLANES, RG_BW)
    row = lax.broadcasted_iota(jnp.int32, (groups, SUBLANES, RG_BW), 1)
    for s in (1, 2, 4):
        keep = row >= s
        a_prev = pltpu.roll(a3, s, 1)
        b_prev = pltpu.roll(b3, s, 1)
        b3 = jnp.where(keep, b3 + a3 * b_prev, b3)
        a3 = jnp.where(keep, a3 * a_prev, a3)
    h_prev = jnp.broadcast_to(hout_ref[...], (SUBLANES, RG_BW))
    hs = []
    for g in range(groups):
        h_g = a3[g] * h_prev + b3[g]
        hs.append(h_g)
        h_prev = jnp.broadcast_to(h_g[SUBLANES - 1:SUBLANES, :], (SUBLANES, RG_BW))
    h = jnp.concatenate(hs, axis=0)
    hout_ref[...] = hs[-1][SUBLANES - 1:SUBLANES, :]
    hg_ref[...] = (h * _gelu_tanh(gate_ref[...])).astype(BF16)

    uext_ref[...] = u[tc - pad:, :]

    @pl.when(t == nt - 1)
    def _():
        cout_ref[...] = uext_ref[pad - (CONV_W - 1):pad, :]


def _rglru_core(gu, nseq, tlen, j, h_all, c_all, w):
    m = nseq * tlen
    tc = min(tlen, RGLRU_TC)
    nt = tlen // tc
    bw = RG_BW
    row = lambda b, n, t: b * nt + t
    vec = pl.BlockSpec((None, 1, bw), lambda b, n, t: (j, 0, n))
    gate_spec = pl.BlockSpec((tc, bw), lambda b, n, t: (row(b, n, t), n))
    u_spec = pl.BlockSpec((tc, bw), lambda b, n, t: (row(b, n, t), RG_BLOCKS + n))
    hin_spec = pl.BlockSpec((None, None, 1, bw), lambda b, n, t: (j, b, 0, n))
    cin_spec = pl.BlockSpec((None, None, CONV_W - 1, bw), lambda b, n, t: (j, b, 0, n))
    hout_spec = pl.BlockSpec((None, 1, bw), lambda b, n, t: (b, 0, n))
    cout_spec = pl.BlockSpec((None, CONV_W - 1, bw), lambda b, n, t: (b, 0, n))
    w_spec = pl.BlockSpec((None, None, bw, bw), lambda b, n, t: (j, n, 0, 0))
    return pl.pallas_call(
        functools.partial(_rglru_kernel, tc=tc, nt=nt),
        grid=(nseq, RG_BLOCKS, nt),
        in_specs=[gate_spec, u_spec, hin_spec, cin_spec,
                  pl.BlockSpec((None, CONV_W, bw), lambda b, n, t: (j, 0, n)), vec,
                  w_spec, vec, w_spec, vec, vec],
        out_specs=[pl.BlockSpec((tc, bw), lambda b, n, t: (row(b, n, t), n)), hout_spec, cout_spec],
        out_shape=[jax.ShapeDtypeStruct((m, D_RNN), BF16),
                   jax.ShapeDtypeStruct((nseq, 1, D_RNN), F32),
                   jax.ShapeDtypeStruct((nseq, CONV_W - 1, D_RNN), F32)],
        scratch_shapes=[pltpu.VMEM((SUBLANES, bw), F32)],
        compiler_params=_params(("parallel", "parallel", "arbitrary"), VMEM_LIMIT_SMALL_MIB),
        name="rglru_core")(gu, gu, h_all, c_all, w["rg_conv_w"], w["rg_conv_b"], w["rg_w_a"], w["rg_b_a"],
                           w["rg_w_x"], w["rg_b_x"], w["rg_lambda"])


def _gla_kernel(q_ref, k_ref, v_ref, g_ref, glr_ref, s0_ref, w2_ref, bgk_ref, gnw_ref,
                o_ref, sout_ref, st_ref, *, nt, nb):
    t = pl.program_id(1)

    @pl.when(t == 0)
    def _():
        for s in range(nb):
            for h in range(GLA_HEADS):
                st_ref[s, h] = jnp.zeros((HEAD_V, HEAD_K), F32) if s0_ref is None else s0_ref[s, h].T

    c = CHUNK
    glr = glr_ref[...].reshape(nb * c, LANES).astype(BF16)
    gk_all = jnp.dot(glr, w2_ref[...], preferred_element_type=F32) + bgk_ref[...]
    gk_all = _log_sigmoid(gk_all) * (1.0 / GATE_NORM)
    rows = lax.broadcasted_iota(jnp.int32, (c, c), 0)
    cols = lax.broadcasted_iota(jnp.int32, (c, c), 1)
    tri = rows >= cols
    tri_b = tri.astype(BF16)
    gnw = gnw_ref[...]
    nt_dims = (((1,), (1,)), ((), ()))
    tn_dims = (((0,), (0,)), ((), ()))
    prep = []
    for s in range(nb):
        gk = gk_all[s * c:(s + 1) * c, :]
        bcum = sum(jnp.dot(tri_b, piece, preferred_element_type=F32) for piece in _split3_bf16(gk))
        blast = bcum[c - 1:c, :]
        k = k_ref[s]
        qe = ((q_ref[s] * (HEAD_K ** -0.5)) * jnp.exp(bcum)).astype(BF16)
        ke = (k * jnp.exp(-bcum)).astype(BF16)
        kd = (k * jnp.exp(blast - bcum)).astype(BF16)
        prep.append((qe, ke, kd, jnp.exp(blast)))
    for h in range(GLA_HEADS):
        ks = slice(h * HEAD_K, (h + 1) * HEAD_K)
        vs = slice(h * HEAD_V, (h + 1) * HEAD_V)
        for s in range(nb):
            qe, ke, kd, decay = prep[s]
            v_h = v_ref[s, :, vs].astype(BF16)
            att = lax.dot_general(qe[:, ks], ke[:, ks], nt_dims, preferred_element_type=F32)
            att = jnp.where(tri, att, 0.0).astype(BF16)
            st = st_ref[s, h]
            o = jnp.dot(att, v_h, preferred_element_type=F32)
            o = o + lax.dot_general(qe[:, ks], st.astype(BF16), nt_dims, preferred_element_type=F32)
            st_ref[s, h] = st * decay[:, ks] + lax.dot_general(v_h, kd[:, ks], tn_dims,
                                                               preferred_element_type=F32)
            on = o * lax.rsqrt(jnp.mean(o * o, axis=-1, keepdims=True) + EPS) * gnw
            g_h = g_ref[s, :, vs]
            o_ref[s, :, vs] = (on * (g_h * jax.nn.sigmoid(g_h))).astype(BF16)

    @pl.when(t == nt - 1)
    def _():
        for s in range(nb):
            for h in range(GLA_HEADS):
                sout_ref[s, h] = st_ref[s, h].T


def _gla_entry_kernel(*refs, has_s0, **kw):
    lead, rest = refs[:5], refs[5:]
    s0_ref = None
    if has_s0:
        s0_ref, rest = rest[0], rest[1:]
    w2_ref, bgk_ref, gnw_ref, _, o_ref, sout_ref, st_ref = rest
    _gla_kernel(*lead, s0_ref, w2_ref, bgk_ref, gnw_ref, o_ref, sout_ref, st_ref, **kw)


def _gla_core(qkvg, glr, nseq, tlen, j, s_all, w, s_stack):
    c = CHUNK
    nt = tlen // c
    nb = GLA_SEQS
    n_gla = s_stack.shape[0]
    qkvg = qkvg.reshape(nseq, tlen, GLA_MAIN)
    glr = glr.reshape(nseq, tlen, LANES)
    in_specs = [pl.BlockSpec((nb, c, GLA_DQ), lambda b, t: (b, t, 0)),
                pl.BlockSpec((nb, c, GLA_DQ), lambda b, t: (b, t, 1)),
                pl.BlockSpec((nb, c, GLA_DV), lambda b, t: (b, t, 1)),
                pl.BlockSpec((nb, c, GLA_DV), lambda b, t: (b, t, 2)),
                pl.BlockSpec((nb, c, LANES), lambda b, t: (b, t, 0))]
    args = [qkvg, qkvg, qkvg, qkvg, glr]
    if s_all is not None:
        in_specs.append(pl.BlockSpec((None, nb, GLA_HEADS, HEAD_K, HEAD_V), lambda b, t: (j, b, 0, 0, 0),
                                     pipeline_mode=pl.Buffered(1)))
        args.append(s_all)
    in_specs += [pl.BlockSpec((None, LANES, GLA_DQ), lambda b, t: (j, 0, 0)),
                 pl.BlockSpec((None, 1, GLA_DQ), lambda b, t: (j, 0, 0)),
                 pl.BlockSpec((None, 1, HEAD_V), lambda b, t: (j, 0, 0)),
                 pl.BlockSpec(memory_space=pl.ANY)]
    args += [w["gla_w_gk2"], w["gla_b_gk"], w["gla_norm_w"], s_stack]
    aliases = {len(args) - 1: 1}
    og, s_stack = pl.pallas_call(
        functools.partial(_gla_entry_kernel, has_s0=s_all is not None, nt=nt, nb=nb),
        grid=(nseq // nb, nt),
        in_specs=in_specs,
        out_specs=[pl.BlockSpec((nb, c, GLA_DV), lambda b, t: (b, t, 0)),
                   pl.BlockSpec((None, nb, GLA_HEADS, HEAD_K, HEAD_V), lambda b, t: (j, b, 0, 0, 0))],
        out_shape=[jax.ShapeDtypeStruct((nseq, tlen, GLA_DV), BF16),
                   jax.ShapeDtypeStruct((n_gla, nseq, GLA_HEADS, HEAD_K, HEAD_V), F32)],
        scratch_shapes=[pltpu.VMEM((nb, GLA_HEADS, HEAD_V, HEAD_K), F32)],
        input_output_aliases=aliases,
        compiler_params=_params(("parallel", "arbitrary"), VMEM_LIMIT_MIB),
        name="gla_core")(*args)
    return og.reshape(nseq * tlen, GLA_DV), s_stack


def _trunk(x, h_all, conv_all, s_all, w, bf16=None):
    nseq, tlen, d = x.shape
    x = x.reshape(nseq * tlen, d)
    h_all = h_all[:, :, None, :]
    cast = bf16 is None
    made = []
    n_gla = w["gla_w_gk2"].shape[0]
    hs, cs, s_stack = [], [], jnp.zeros((n_gla, nseq, GLA_HEADS, HEAD_K, HEAD_V), F32)
    for layer in range(DEPTH):
        j = layer // 2
        wl = {} if cast else bf16[layer]
        if layer % 2 == 0:
            if cast:
                gu, wl["in"] = _norm_matmul(x, w["norm_mix"], layer, 2 * D_RNN, w_f32=w["rg_w_in"], j=j)
            else:
                gu = _norm_matmul(x, w["norm_mix"], layer, 2 * D_RNN, w_bf16=wl["in"])
            mixed, h_new, c_new = _rglru_core(gu, nseq, tlen, j, h_all, conv_all, w)
            hs.append(h_new[:, 0, :])
            cs.append(c_new)
            w_out = w["rg_w_out"]
        else:
            if cast:
                (qkvg, glr), wl["in"] = _norm_matmul(x, w["norm_mix"], layer, GLA_MAIN, side=GATE_RANK,
                                                     w_f32=w["gla_w_in"], j=j)
            else:
                qkvg, glr = _norm_matmul(x, w["norm_mix"], layer, GLA_MAIN, side=GATE_RANK, w_bf16=wl["in"])
            mixed, s_stack = _gla_core(qkvg, glr, nseq, tlen, j, s_all, w, s_stack)
            w_out = w["gla_w_out"]
        final = layer == DEPTH - 1
        if cast:
            x, wl["out"] = _matmul_res(mixed, x, w_f32=w_out, j=j)
            x, wl["ffn"] = _ffn(x, w["norm_ffn"], layer, w["norm_final"], final,
                                w_f32=(w["ffn_w_up"], w["ffn_w_down"]))
            made.append(wl)
        else:
            x = _matmul_res(mixed, x, w_bf16=wl["out"])
            x = _ffn(x, w["norm_ffn"], layer, w["norm_final"], final, w_bf16=wl["ffn"])
    return (x.reshape(nseq, tlen, d), jnp.stack(hs), jnp.stack(cs), s_stack), made


def _prepare_weights(norm_mix, norm_ffn, norm_final, rg_w_in, rg_conv_w, rg_conv_b, rg_w_a, rg_b_a, rg_w_x, rg_b_x,
                     rg_lambda, rg_w_out, gla_w_in, gla_w_gk2, gla_b_gk, gla_norm_w, gla_w_out, ffn_w_up, ffn_w_down):
    glr_pad = LANES - GATE_RANK
    return {
        "norm_mix": norm_mix[:, None, :],
        "norm_ffn": norm_ffn[:, None, :],
        "norm_final": norm_final[None, :],
        "rg_w_in": rg_w_in,
        "rg_conv_w": rg_conv_w,
        "rg_conv_b": rg_conv_b[:, None, :],
        "rg_w_a": rg_w_a.astype(BF16),
        "rg_b_a": rg_b_a[:, None, :],
        "rg_w_x": rg_w_x.astype(BF16),
        "rg_b_x": rg_b_x[:, None, :],
        "rg_lambda": rg_lambda[:, None, :],
        "rg_w_out": rg_w_out,
        "gla_w_in": gla_w_in,
        "gla_w_gk2": jnp.pad(gla_w_gk2, ((0, 0), (0, glr_pad), (0, 0))).astype(BF16),
        "gla_b_gk": gla_b_gk[:, None, :],
        "gla_norm_w": gla_norm_w[:, None, :],
        "gla_w_out": gla_w_out,
        "ffn_w_up": ffn_w_up,
        "ffn_w_down": ffn_w_down,
    }


def kernel(x_prompt, x_sample, state_rglru_h, state_rglru_conv, state_gla, norm_mix, norm_ffn, norm_final, rg_w_in, rg_conv_w, rg_conv_b, rg_w_a, rg_b_a, rg_w_x, rg_b_x, rg_lambda, rg_w_out, gla_w_in, gla_w_gk2, gla_b_gk, gla_norm_w, gla_w_out, ffn_w_up, ffn_w_down):
    w = _prepare_weights(norm_mix, norm_ffn, norm_final, rg_w_in, rg_conv_w, rg_conv_b, rg_w_a, rg_b_a, rg_w_x,
                         rg_b_x, rg_lambda, rg_w_out, gla_w_in, gla_w_gk2, gla_b_gk, gla_norm_w, gla_w_out,
                         ffn_w_up, ffn_w_down)
    bp = x_prompt.shape[0]
    n_rg = state_rglru_h.shape[0]
    h0 = jnp.zeros((n_rg, bp, D_RNN), F32)
    c0 = jnp.zeros((n_rg, bp, CONV_W - 1, D_RNN), F32)
    (y_s, h_s, c_s, s_s), bf16 = _trunk(x_sample, state_rglru_h, state_rglru_conv, state_gla, w)
    (y_p, h_p, c_p, s_p), _ = _trunk(x_prompt, h0, c0, None, w, bf16)
    return (y_p, y_s, h_p, c_p, s_p, h_s, c_s, s_s)
```
